```python
import jax, jax.numpy as jnp
from jax import lax
import numpy as np

D_MODEL = 2048
BATCH = 4
SEQ = 4096
DEPTH = 2

MEM_LEN = 256
EPS = 1e-6
N_BRANCH = 3
BRANCH_WIDTH = 1024

CHUNK = 128
A_GROUPS = 8
A_GROUP_DIM = 128
A_WIDTH = A_GROUPS * A_GROUP_DIM

WINDOW = 128
BLOCK = 128
B_HEADS = 16
B_KV_HEADS = 4
B_HEAD_DIM = 64
B_Q_WIDTH = B_HEADS * B_HEAD_DIM
B_KV_WIDTH = B_KV_HEADS * B_HEAD_DIM

C_HEADS = 4
C_HEAD_DIM = 256
C_WIDTH = C_HEADS * C_HEAD_DIM

SPLIT_POINTS = (
    A_WIDTH,
    2 * A_WIDTH,
    2 * A_WIDTH + B_Q_WIDTH,
    2 * A_WIDTH + B_Q_WIDTH + B_KV_WIDTH,
    2 * A_WIDTH + B_Q_WIDTH + 2 * B_KV_WIDTH,
    2 * A_WIDTH + B_Q_WIDTH + 2 * B_KV_WIDTH + C_WIDTH,
)
IN_WIDTH = 2 * A_WIDTH + B_Q_WIDTH + 2 * B_KV_WIDTH + C_WIDTH + N_BRANCH * D_MODEL

N_GROUPS = 4
EXPERTS_PER_GROUP = 4
N_EXPERTS = N_GROUPS * EXPERTS_PER_GROUP
TOP_K = 2
D_FF_EXPERT = 512

kernel_name = "hybrid_gmlp_swa_sink_memxattn_hmoe"


def rms_norm(x, gain):
    xf = x.astype(jnp.float32)
    y = xf * lax.rsqrt(jnp.mean(xf * xf, axis=-1, keepdims=True) + EPS)
    return (y * gain.astype(jnp.float32)).astype(x.dtype)


def chunked_spatial_gating(u, v, v_gain, w_s, b_s):
    b, s, _ = u.shape
    n = s // CHUNK
    v = rms_norm(v.reshape(b, n, CHUNK, A_GROUPS, A_GROUP_DIM), v_gain)
    causal = jnp.tril(jnp.ones((CHUNK, CHUNK), dtype=bool))
    w = jnp.where(causal[None], w_s, 0.0).astype(v.dtype)
    mixed = jnp.einsum('gts,bnsgc->bntgc', w, v) + b_s.T[:, :, None].astype(v.dtype)
    out = u.reshape(b, n, CHUNK, A_GROUPS, A_GROUP_DIM) * mixed
    return out.reshape(b, s, A_WIDTH)


def sliding_window_sink_attention(q, k, v, sinks):
    b, s, _, _ = q.shape
    n = s // BLOCK
    rep = B_HEADS // B_KV_HEADS
    qb = q.reshape(b, n, BLOCK, B_KV_HEADS, rep, B_HEAD_DIM)
    kb = k.reshape(b, n, BLOCK, B_KV_HEADS, B_HEAD_DIM)
    vb = v.reshape(b, n, BLOCK, B_KV_HEADS, B_HEAD_DIM)

    def with_prev(t):
        prev = jnp.pad(t[:, :-1], ((0, 0), (1, 0), (0, 0), (0, 0), (0, 0)))
        return jnp.concatenate([prev, t], axis=2)

    kw, vw = with_prev(kb), with_prev(vb)
    scores = jnp.einsum('bnqkrd,bnskd->bnkrqs', qb, kw,
                        preferred_element_type=jnp.float32) * (B_HEAD_DIM ** -0.5)
    n_idx = jnp.arange(n)[:, None, None]
    qi = jnp.arange(BLOCK)[None, :, None]
    kj = jnp.arange(2 * BLOCK)[None, None, :]
    rel = BLOCK + qi - kj
    mask = (rel >= 0) & (rel < WINDOW) & ((n_idx > 0) | (kj >= BLOCK))
    scores = jnp.where(mask[None, :, None, None], scores, -jnp.inf)
    sink = sinks.astype(jnp.float32).reshape(1, 1, B_KV_HEADS, rep, 1, 1)
    m = jnp.maximum(jnp.max(scores, axis=-1, keepdims=True), sink)
    p = jnp.exp(scores - m)
    denom = jnp.sum(p, axis=-1, keepdims=True) + jnp.exp(sink - m)
    probs = (p / denom).astype(v.dtype)
    out = jnp.einsum('bnkrqs,bnskd->bnqkrd', probs, vw)
    return out.reshape(b, s, B_Q_WIDTH)


def memory_cross_attention(q, k, v):
    b, s, _, _ = q.shape
    scores = jnp.einsum('bshd,bmhd->bhsm', q, k,
                        preferred_element_type=jnp.float32) * (C_HEAD_DIM ** -0.5)
    probs = jax.nn.softmax(scores, axis=-1).astype(v.dtype)
    return jnp.einsum('bhsm,bmhd->bshd', probs, v).reshape(b, s, C_WIDTH)


def hierarchical_moe(h, w_rg, b_rg, w_re, b_re, w_gate, w_up, w_down):
    b, s, d = h.shape
    t = h.reshape(b * s, d)
    group_logits = (t @ w_rg).astype(jnp.float32) + b_rg.astype(jnp.float32)
    group_probs = jax.nn.softmax(group_logits, axis=-1)
    g = jnp.argmax(group_logits, axis=-1)
    p_g = jnp.take_along_axis(group_probs, g[:, None], axis=-1)
    expert_logits = ((t @ w_re).astype(jnp.float32) + b_re.astype(jnp.float32)
                     ).reshape(-1, N_GROUPS, EXPERTS_PER_GROUP)
    in_group = jnp.take_along_axis(expert_logits, g[:, None, None], axis=1)[:, 0]
    top_vals, top_idx = lax.top_k(in_group, TOP_K)
    weights = jax.nn.softmax(top_vals, axis=-1) * p_g
    expert_id = g[:, None] * EXPERTS_PER_GROUP + top_idx
    gates = jnp.sum(jax.nn.one_hot(expert_id, N_EXPERTS, dtype=jnp.float32)
                    * weights[..., None], axis=1)
    hidden = (jax.nn.silu(jnp.einsum('td,edf->tef', t, w_gate))
              * jnp.einsum('td,edf->tef', t, w_up))
    hidden = hidden * gates[..., None].astype(hidden.dtype)
    out = jnp.einsum('tef,efd->td', hidden, w_down)
    return out.reshape(b, s, d)


def setup_inputs(seed: int = 0) -> dict:
    key = jax.random.key(seed)
    ks = jax.random.split(key, 26)

    def nrm(k, shape, scale):
        return jax.random.normal(k, shape, dtype=jnp.float32) * scale

    def gain(k, shape):
        return 1.0 + 0.05 * jax.random.normal(k, shape, dtype=jnp.float32)

    L = DEPTH
    return {
        "x": nrm(ks[0], (BATCH, SEQ, D_MODEL), 1.0),
        "mem": nrm(ks[1], (BATCH, MEM_LEN, D_MODEL), 1.0),
        "norm_mix": gain(ks[2], (L, D_MODEL)),
        "norm_mem": gain(ks[3], (L, D_MODEL)),
        "norm_ffn": gain(ks[4], (L, D_MODEL)),
        "w_in": nrm(ks[5], (L, D_MODEL, IN_WIDTH), D_MODEL ** -0.5),
        "v_gain": gain(ks[6], (L, A_GROUPS, A_GROUP_DIM)),
        "w_spatial": nrm(ks[7], (L, A_GROUPS, CHUNK, CHUNK), CHUNK ** -0.5),
        "b_spatial": 1.0 + 0.02 * jax.random.normal(ks[8], (L, A_GROUPS, CHUNK), dtype=jnp.float32),
        "q_gain_b": gain(ks[9], (L, B_HEAD_DIM)),
        "k_gain_b": gain(ks[10], (L, B_HEAD_DIM)),
        "sinks": nrm(ks[11], (L, B_HEADS), 0.5),
        "q_gain_c": gain(ks[12], (L, C_HEAD_DIM)),
        "k_gain_c": gain(ks[13], (L, C_HEAD_DIM)),
        "w_mem_kv": nrm(ks[14], (L, D_MODEL, 2 * C_WIDTH), D_MODEL ** -0.5),
        "w_branch": nrm(ks[15], (L, N_BRANCH, BRANCH_WIDTH, D_MODEL), BRANCH_WIDTH ** -0.5),
        "w_out": nrm(ks[16], (L, D_MODEL, D_MODEL), D_MODEL ** -0.5),
        "w_router_group": nrm(ks[17], (L, D_MODEL, N_GROUPS), D_MODEL ** -0.5),
        "b_router_group": nrm(ks[18], (L, N_GROUPS), 0.01),
        "w_router_expert": nrm(ks[19], (L, D_MODEL, N_EXPERTS), D_MODEL ** -0.5),
        "b_router_expert": nrm(ks[20], (L, N_EXPERTS), 0.01),
        "w_gate_e": nrm(ks[21], (L, N_EXPERTS, D_MODEL, D_FF_EXPERT), D_MODEL ** -0.5),
        "w_up_e": nrm(ks[22], (L, N_EXPERTS, D_MODEL, D_FF_EXPERT), D_MODEL ** -0.5),
        "w_down_e": nrm(ks[23], (L, N_EXPERTS, D_FF_EXPERT, D_MODEL), D_FF_EXPERT ** -0.5),
    }


def reference(x, mem, norm_mix, norm_mem, norm_ffn, w_in, v_gain, w_spatial, b_spatial,
              q_gain_b, k_gain_b, sinks, q_gain_c, k_gain_c, w_mem_kv, w_branch, w_out,
              w_router_group, b_router_group, w_router_expert, b_router_expert,
              w_gate_e, w_up_e, w_down_e):
    b, s, _ = x.shape
    m_len = mem.shape[1]
    for layer in range(DEPTH):
        h = rms_norm(x, norm_mix[layer])
        proj = h @ w_in[layer]
        u_a, v_a, q_b, k_b, v_b, q_c, gate_logits = jnp.split(proj, SPLIT_POINTS, axis=-1)

        y_a = chunked_spatial_gating(jax.nn.gelu(u_a, approximate=False),
                                     jax.nn.gelu(v_a, approximate=False),
                                     v_gain[layer], w_spatial[layer], b_spatial[layer])

        qb = rms_norm(q_b.reshape(b, s, B_HEADS, B_HEAD_DIM), q_gain_b[layer])
        kb = rms_norm(k_b.reshape(b, s, B_KV_HEADS, B_HEAD_DIM), k_gain_b[layer])
        vb = v_b.reshape(b, s, B_KV_HEADS, B_HEAD_DIM)
        y_b = sliding_window_sink_attention(qb, kb, vb, sinks[layer])

        mem_h = rms_norm(mem, norm_mem[layer])
        k_c, v_c = jnp.split(mem_h @ w_mem_kv[layer], 2, axis=-1)
        kc = rms_norm(k_c.reshape(b, m_len, C_HEADS, C_HEAD_DIM), k_gain_c[layer])
        vc = v_c.reshape(b, m_len, C_HEADS, C_HEAD_DIM)
        qc = rms_norm(q_c.reshape(b, s, C_HEADS, C_HEAD_DIM), q_gain_c[layer])
        y_c = memory_cross_attention(qc, kc, vc)

        branches = jnp.stack([y_a, y_b, y_c], axis=2)
        widened = jnp.einsum('bsnw,nwd->bsnd', branches, w_branch[layer])
        gates = jax.nn.sigmoid(gate_logits.reshape(b, s, N_BRANCH, D_MODEL))
        merged = jnp.sum(gates * widened, axis=2)
        x = x + merged @ w_out[layer]

        h2 = rms_norm(x, norm_ffn[layer])
        x = x + hierarchical_moe(h2, w_router_group[layer], b_router_group[layer],
                                 w_router_expert[layer], b_router_expert[layer],
                                 w_gate_e[layer], w_up_e[layer], w_down_e[layer])
    return x
```

```python
import functools

import jax
import jax.numpy as jnp
from jax import lax
from jax.experimental import pallas as pl
from jax.experimental.pallas import tpu as pltpu

F32 = jnp.float32
BF16 = jnp.bfloat16
I32 = jnp.int32

D_MODEL = 2048
DEPTH = 2
MEM_LEN = 256
EPS = 1e-6
BRANCH_WIDTH = 1024
N_BRANCH = 3

CHUNK = 128
A_GROUPS = 8
A_GROUP_DIM = 128
A_WIDTH = A_GROUPS * A_GROUP_DIM

BLOCK = 128
B_HEADS = 16
B_KV_HEADS = 4
B_REP = B_HEADS // B_KV_HEADS
B_HEAD_DIM = 64
B_Q_WIDTH = B_HEADS * B_HEAD_DIM
B_KV_WIDTH = B_KV_HEADS * B_HEAD_DIM

C_HEADS = 4
C_HEAD_DIM = 256
C_WIDTH = C_HEADS * C_HEAD_DIM

COL_U = 0
COL_V = A_WIDTH
COL_QB = 2 * A_WIDTH
COL_KB = COL_QB + B_Q_WIDTH
COL_VB = COL_KB + B_KV_WIDTH
COL_QC = COL_VB + B_KV_WIDTH
COL_GATE = COL_QC + C_WIDTH
IN_WIDTH = COL_GATE + N_BRANCH * D_MODEL
MIX_WIDTH = COL_GATE

N_GROUPS = 4
EXPERTS_PER_GROUP = 4
N_EXPERTS = N_GROUPS * EXPERTS_PER_GROUP
TOP_K = 2
D_FF_EXPERT = 512

LANES = 128
VMEM_LIMIT = 56 * 1024 * 1024
SQRT_HALF = 0.7071067811865476

TM_NORM = 512
TM_PROJ, TN_PROJ = 1024, 512
TM_MERGE, TN_MERGE = 1024, 512
TM_OUT = 512
TM_ROUTE = 512
TK_DISPATCH = 2048
DISPATCH_WINDOW = 32
TM_EXPERT = 256
TM_COMBINE = 256


def _params(sem):
    return pltpu.CompilerParams(dimension_semantics=sem, vmem_limit_bytes=VMEM_LIMIT)


def _rms(x, gain):
    ms = jnp.mean(x * x, axis=-1, keepdims=True)
    return x * lax.rsqrt(ms + EPS) * gain


def _norm_body(x_ref, g_ref, o_ref):
    o_ref[...] = _rms(x_ref[...], g_ref[...]).astype(o_ref.dtype)


def _norm_call(x, gain_row):
    t = x.shape[0]
    return pl.pallas_call(
        _norm_body,
        grid=(t // TM_NORM,),
        in_specs=[pl.BlockSpec((TM_NORM, D_MODEL), lambda i: (i, 0)),
                  pl.BlockSpec((1, D_MODEL), lambda i: (0, 0))],
        out_specs=pl.BlockSpec((TM_NORM, D_MODEL), lambda i: (i, 0)),
        out_shape=jax.ShapeDtypeStruct((t, D_MODEL), BF16),
        compiler_params=_params(("arbitrary",)),
        name="norm",
    )(x, gain_row)


TN_MEMKV = 512


def _memkv_body(mem_ref, gm_ref, w_ref, gk_ref, o_ref):
    n = pl.program_id(0)
    h = _rms(mem_ref[...], gm_ref[...]).astype(BF16)
    acc = jnp.dot(h, w_ref[...].astype(BF16), preferred_element_type=F32)

    @pl.when(n < C_WIDTH // TN_MEMKV)
    def _():
        for j in range(TN_MEMKV // C_HEAD_DIM):
            sl = slice(j * C_HEAD_DIM, (j + 1) * C_HEAD_DIM)
            o_ref[:, sl] = _rms(acc[:, sl], gk_ref[...]).astype(o_ref.dtype)

    @pl.when(n >= C_WIDTH // TN_MEMKV)
    def _():
        o_ref[...] = acc.astype(o_ref.dtype)


def _memkv_call(mem2d, gm_row, w_mem_kv, gk_row, layer):
    rows = mem2d.shape[0]
    return pl.pallas_call(
        _memkv_body,
        grid=(2 * C_WIDTH // TN_MEMKV,),
        in_specs=[pl.BlockSpec((rows, D_MODEL), lambda n: (0, 0)),
                  pl.BlockSpec((1, D_MODEL), lambda n: (0, 0)),
                  pl.BlockSpec((None, D_MODEL, TN_MEMKV), lambda n: (layer, 0, n)),
                  pl.BlockSpec((1, C_HEAD_DIM), lambda n: (0, 0))],
        out_specs=pl.BlockSpec((rows, TN_MEMKV), lambda n: (0, n)),
        out_shape=jax.ShapeDtypeStruct((rows, 2 * C_WIDTH), BF16),
        compiler_params=_params(("arbitrary",)),
        name="memkv",
    )(mem2d, gm_row, w_mem_kv, gk_row)


def _inproj_body(h_ref, w_ref, o_ref, wb_ref):
    n = pl.program_id(0)
    m = pl.program_id(1)

    @pl.when(m == 0)
    def _():
        wb_ref[...] = w_ref[...].astype(BF16)

    acc = jnp.dot(h_ref[...], wb_ref[...], preferred_element_type=F32)
    n_gelu = COL_QB // TN_PROJ
    n_gate = COL_GATE // TN_PROJ

    @pl.when(n < n_gelu)
    def _():
        o_ref[...] = (0.5 * acc * (1.0 + lax.erf(acc * SQRT_HALF))).astype(o_ref.dtype)

    @pl.when(jnp.logical_and(n >= n_gelu, n < n_gate))
    def _():
        o_ref[...] = acc.astype(o_ref.dtype)

    @pl.when(n >= n_gate)
    def _():
        o_ref[...] = (1.0 / (1.0 + jnp.exp(-acc))).astype(o_ref.dtype)


def _inproj_call(h, w_in, layer):
    t = h.shape[0]
    return pl.pallas_call(
        _inproj_body,
        grid=(IN_WIDTH // TN_PROJ, t // TM_PROJ),
        in_specs=[pl.BlockSpec((TM_PROJ, D_MODEL), lambda n, m: (m, 0)),
                  pl.BlockSpec((None, D_MODEL, TN_PROJ), lambda n, m: (layer, 0, n))],
        out_specs=pl.BlockSpec((TM_PROJ, TN_PROJ), lambda n, m: (m, n)),
        out_shape=jax.ShapeDtypeStruct((t, IN_WIDTH), BF16),
        scratch_shapes=[pltpu.VMEM((D_MODEL, TN_PROJ), BF16)],
        compiler_params=_params(("arbitrary", "arbitrary")),
        name="inproj",
    )(h, w_in)


def _mixer_body(cur_ref, prev_ref, kvc_ref, vgain_ref, ws_ref, bs_ref, qgb_ref, kgb_ref,
                sink_ref, qgc_ref, y_ref, *, blocks_per_seq):
    i = pl.program_id(0)
    n = lax.rem(i, blocks_per_seq)

    row = lax.broadcasted_iota(I32, (CHUNK, CHUNK), 0)
    col = lax.broadcasted_iota(I32, (CHUNK, CHUNK), 1)
    causal = col <= row
    for g in range(A_GROUPS):
        sl = slice(g * A_GROUP_DIM, (g + 1) * A_GROUP_DIM)
        u = cur_ref[:, COL_U + g * A_GROUP_DIM:COL_U + (g + 1) * A_GROUP_DIM].astype(F32)
        v = cur_ref[:, COL_V + g * A_GROUP_DIM:COL_V + (g + 1) * A_GROUP_DIM].astype(F32)
        vn = _rms(v, vgain_ref[:, sl]).astype(BF16)
        w = jnp.where(causal, ws_ref[g], 0.0).astype(BF16)
        mixed = jnp.dot(w, vn, preferred_element_type=F32) + bs_ref[:, g:g + 1]
        y_ref[:, sl] = (u * mixed).astype(y_ref.dtype)

    qi = lax.broadcasted_iota(I32, (B_REP * BLOCK, 2 * BLOCK), 0) & (BLOCK - 1)
    kj = lax.broadcasted_iota(I32, (B_REP * BLOCK, 2 * BLOCK), 1)
    rel = BLOCK + qi - kj
    valid = (rel >= 0) & (rel < BLOCK) & ((n > 0) | (kj >= BLOCK))
    q_all = cur_ref[:, COL_QB:COL_QB + B_Q_WIDTH].astype(F32)
    k_all = jnp.concatenate([prev_ref[:, 0:B_KV_WIDTH], cur_ref[:, COL_KB:COL_KB + B_KV_WIDTH]],
                            axis=0).astype(F32)
    v_all = jnp.concatenate([prev_ref[:, B_KV_WIDTH:2 * B_KV_WIDTH],
                             cur_ref[:, COL_VB:COL_VB + B_KV_WIDTH]], axis=0)
    qgain = qgb_ref[...] * (B_HEAD_DIM ** -0.5)
    for kv in range(B_KV_HEADS):
        ksl = slice(kv * B_HEAD_DIM, (kv + 1) * B_HEAD_DIM)
        kh = _rms(k_all[:, ksl], kgb_ref[...]).astype(BF16)
        vh = v_all[:, ksl]
        qs, sk = [], []
        for r in range(B_REP):
            h = kv * B_REP + r
            qh = q_all[:, h * B_HEAD_DIM:(h + 1) * B_HEAD_DIM]
            qs.append(_rms(qh, qgain).astype(BF16))
            sk.append(jnp.full((BLOCK, 1), sink_ref[h], F32))
        q4 = jnp.concatenate(qs, axis=0)
        sink = jnp.concatenate(sk, axis=0)
        s = lax.dot_general(q4, kh, (((1,), (1,)), ((), ())), preferred_element_type=F32)
        s = jnp.where(valid, s, -jnp.inf)
        mx = jnp.maximum(jnp.max(s, axis=-1, keepdims=True), sink)
        p = jnp.exp(s - mx)
        denom = jnp.sum(p, axis=-1, keepdims=True) + jnp.exp(sink - mx)
        probs = (p / denom).astype(BF16)
        o = jnp.dot(probs, vh, preferred_element_type=F32)
        for r in range(B_REP):
            h = kv * B_REP + r
            y_ref[:, A_WIDTH + h * B_HEAD_DIM:A_WIDTH + (h + 1) * B_HEAD_DIM] = (
                o[r * BLOCK:(r + 1) * BLOCK].astype(y_ref.dtype))

    qgc = qgc_ref[...] * (C_HEAD_DIM ** -0.5)
    for h in range(C_HEADS):
        sl = slice(h * C_HEAD_DIM, (h + 1) * C_HEAD_DIM)
        qh = _rms(cur_ref[:, COL_QC + h * C_HEAD_DIM:COL_QC + (h + 1) * C_HEAD_DIM].astype(F32),
                  qgc).astype(BF16)
        kh = kvc_ref[:, h * C_HEAD_DIM:(h + 1) * C_HEAD_DIM]
        vh = kvc_ref[:, C_WIDTH + h * C_HEAD_DIM:C_WIDTH + (h + 1) * C_HEAD_DIM]
        s = lax.dot_general(qh, kh, (((1,), (1,)), ((), ())), preferred_element_type=F32)
        p = jnp.exp(s - jnp.max(s, axis=-1, keepdims=True))
        probs = (p / jnp.sum(p, axis=-1, keepdims=True)).astype(BF16)
        o = jnp.dot(probs, vh, preferred_element_type=F32)
        y_ref[:, A_WIDTH + B_Q_WIDTH + h * C_HEAD_DIM:
              A_WIDTH + B_Q_WIDTH + (h + 1) * C_HEAD_DIM] = o.astype(y_ref.dtype)


def _mixer_call(proj, kvc, vgain_row, w_spatial, bs_t, qgb_row, kgb_row, sinks, qgc_row,
                layer, seq):
    t = proj.shape[0]
    bps = seq // BLOCK
    kvb = COL_KB // (2 * B_KV_WIDTH)
    return pl.pallas_call(
        functools.partial(_mixer_body, blocks_per_seq=bps),
        grid=(t // BLOCK,),
        in_specs=[
            pl.BlockSpec((BLOCK, MIX_WIDTH), lambda i: (i, 0)),
            pl.BlockSpec((BLOCK, 2 * B_KV_WIDTH), lambda i: (jnp.maximum(i - 1, 0), kvb)),
            pl.BlockSpec((MEM_LEN, 2 * C_WIDTH), lambda i: (i // bps, 0)),
            pl.BlockSpec((1, A_WIDTH), lambda i: (0, 0)),
            pl.BlockSpec((None, A_GROUPS, CHUNK, CHUNK), lambda i: (layer, 0, 0, 0)),
            pl.BlockSpec((CHUNK, A_GROUPS), lambda i: (0, 0)),
            pl.BlockSpec((1, B_HEAD_DIM), lambda i: (0, 0)),
            pl.BlockSpec((1, B_HEAD_DIM), lambda i: (0, 0)),
            pl.BlockSpec(memory_space=pltpu.SMEM),
            pl.BlockSpec((1, C_HEAD_DIM), lambda i: (0, 0)),
        ],
        out_specs=pl.BlockSpec((BLOCK, N_BRANCH * BRANCH_WIDTH), lambda i: (i, 0)),
        out_shape=jax.ShapeDtypeStruct((t, N_BRANCH * BRANCH_WIDTH), BF16),
        compiler_params=_params(("arbitrary",)),
        name="mixer",
    )(proj, proj, kvc, vgain_row, w_spatial, bs_t, qgb_row, kgb_row, sinks, qgc_row)


def _merge_body(y_ref, g0_ref, g1_ref, g2_ref, w_ref, o_ref, wb_ref):
    m = pl.program_id(1)

    @pl.when(m == 0)
    def _():
        wb_ref[...] = w_ref[...].astype(BF16)

    acc = None
    for b, g_ref in enumerate((g0_ref, g1_ref, g2_ref)):
        wide = jnp.dot(y_ref[:, b * BRANCH_WIDTH:(b + 1) * BRANCH_WIDTH], wb_ref[b],
                       preferred_element_type=F32)
        term = g_ref[...].astype(F32) * wide
        acc = term if acc is None else acc + term
    o_ref[...] = acc.astype(o_ref.dtype)


def _merge_call(y, proj, w_branch, layer):
    t = y.shape[0]
    gate0 = COL_GATE // TN_MERGE
    per = D_MODEL // TN_MERGE

    def gate_spec(b):
        return pl.BlockSpec((TM_MERGE, TN_MERGE), lambda n, m: (m, gate0 + b * per + n))

    return pl.pallas_call(
        _merge_body,
        grid=(D_MODEL // TN_MERGE, t // TM_MERGE),
        in_specs=[pl.BlockSpec((TM_MERGE, N_BRANCH * BRANCH_WIDTH), lambda n, m: (m, 0)),
                  gate_spec(0), gate_spec(1), gate_spec(2),
                  pl.BlockSpec((None, N_BRANCH, BRANCH_WIDTH, TN_MERGE),
                               lambda n, m: (layer, 0, 0, n))],
        out_specs=pl.BlockSpec((TM_MERGE, TN_MERGE), lambda n, m: (m, n)),
        out_shape=jax.ShapeDtypeStruct((t, D_MODEL), BF16),
        scratch_shapes=[pltpu.VMEM((N_BRANCH, BRANCH_WIDTH, TN_MERGE), BF16)],
        compiler_params=_params(("arbitrary", "arbitrary")),
        name="merge",
    )(y, proj, proj, proj, w_branch)


def _outproj_body(mg_ref, x_ref, w_ref, gn_ref, wr_ref, br_ref, x1_ref, lg_ref,
                  wb_ref, wrh_ref, wrl_ref):
    i = pl.program_id(0)

    @pl.when(i == 0)
    def _():
        wb_ref[...] = w_ref[...].astype(BF16)
        wr = wr_ref[...]
        hi = wr.astype(BF16)
        wrh_ref[...] = hi
        wrl_ref[...] = (wr - hi.astype(F32)).astype(BF16)

    x1 = x_ref[...] + jnp.dot(mg_ref[...], wb_ref[...], preferred_element_type=F32)
    x1_ref[...] = x1
    h2 = _rms(x1, gn_ref[...])
    hi = h2.astype(BF16)
    lo = (h2 - hi.astype(F32)).astype(BF16)
    lg = (jnp.dot(hi, wrh_ref[...], preferred_element_type=F32)
          + jnp.dot(lo, wrh_ref[...], preferred_element_type=F32)
          + jnp.dot(hi, wrl_ref[...], preferred_element_type=F32))
    lg_ref[...] = lg + br_ref[...]


def _outproj_call(merged, x, w_out, gn_row, w_router, b_router, layer):
    t = x.shape[0]
    return pl.pallas_call(
        _outproj_body,
        grid=(t // TM_OUT,),
        in_specs=[pl.BlockSpec((TM_OUT, D_MODEL), lambda i: (i, 0)),
                  pl.BlockSpec((TM_OUT, D_MODEL), lambda i: (i, 0)),
                  pl.BlockSpec((None, D_MODEL, D_MODEL), lambda i: (layer, 0, 0),
                               pipeline_mode=pl.Buffered(1)),
                  pl.BlockSpec((1, D_MODEL), lambda i: (0, 0)),
                  pl.BlockSpec((D_MODEL, LANES), lambda i: (0, 0)),
                  pl.BlockSpec((1, LANES), lambda i: (0, 0))],
        out_specs=[pl.BlockSpec((TM_OUT, D_MODEL), lambda i: (i, 0)),
                   pl.BlockSpec((TM_OUT, LANES), lambda i: (i, 0))],
        out_shape=[jax.ShapeDtypeStruct((t, D_MODEL), F32),
                   jax.ShapeDtypeStruct((t, LANES), F32)],
        scratch_shapes=[pltpu.VMEM((D_MODEL, D_MODEL), BF16),
                        pltpu.VMEM((D_MODEL, LANES), BF16),
                        pltpu.VMEM((D_MODEL, LANES), BF16)],
        compiler_params=_params(("arbitrary",)),
        name="outproj",
    )(merged, x, w_out, gn_row, w_router, b_router)


def _route_body(lg_ref, ri_ref, rw_ref, cnt_ref, carry_ref):
    i = pl.program_id(0)

    @pl.when(i == 0)
    def _():
        carry_ref[...] = jnp.zeros_like(carry_ref)

    lg = lg_ref[...]
    tm = lg.shape[0]
    lane = lax.broadcasted_iota(I32, lg.shape, 1)
    neg = -jnp.inf
    big = jnp.int32(LANES)

    is_g = lane < N_GROUPS
    gl = jnp.where(is_g, lg, neg)
    gmax = jnp.max(gl, axis=-1, keepdims=True)
    gidx = jnp.min(jnp.where(gl == gmax, lane, big), axis=-1, keepdims=True)
    p_g = 1.0 / jnp.sum(jnp.where(is_g, jnp.exp(lg - gmax), 0.0), axis=-1, keepdims=True)

    e_lane = lane - N_GROUPS
    in_group = (e_lane >= 0) & (e_lane < N_EXPERTS) & ((e_lane >> 2) == gidx)
    el = jnp.where(in_group, lg, neg)
    v0 = jnp.max(el, axis=-1, keepdims=True)
    i0 = jnp.min(jnp.where(el == v0, lane, big), axis=-1, keepdims=True)
    el1 = jnp.where(lane == i0, neg, el)
    v1 = jnp.max(el1, axis=-1, keepdims=True)
    i1 = jnp.min(jnp.where(el1 == v1, lane, big), axis=-1, keepdims=True)
    e0 = i0 - N_GROUPS
    e1 = i1 - N_GROUPS
    ex = jnp.exp(v1 - v0)
    w0 = p_g / (1.0 + ex)
    w1 = p_g * ex / (1.0 + ex)

    oh0 = lane == e0
    oh1 = lane == e1
    oh = jnp.where(oh0 | oh1, 1.0, 0.0)
    r_i = lax.broadcasted_iota(I32, (tm, tm), 0)
    c_i = lax.broadcasted_iota(I32, (tm, tm), 1)
    lower = jnp.where(c_i < r_i, 1.0, 0.0).astype(BF16)
    before = jnp.dot(lower, oh.astype(BF16), preferred_element_type=F32) + carry_ref[...]
    rank0 = jnp.sum(jnp.where(oh0, before, 0.0), axis=-1, keepdims=True).astype(I32)
    rank1 = jnp.sum(jnp.where(oh1, before, 0.0), axis=-1, keepdims=True).astype(I32)
    carry_ref[...] = carry_ref[...] + jnp.sum(oh, axis=0, keepdims=True)

    zero_i = jnp.zeros_like(lane)
    ri_ref[...] = jnp.where(lane == 0, e0, jnp.where(lane == 1, e1,
                            jnp.where(lane == 2, rank0, jnp.where(lane == 3, rank1, zero_i))))
    rw_ref[...] = jnp.where(lane == 0, w0, jnp.where(lane == 1, w1, 0.0))
    cnt_ref[...] = jnp.broadcast_to(carry_ref[...], cnt_ref.shape).astype(I32)


def _route_call(logits):
    t = logits.shape[0]
    return pl.pallas_call(
        _route_body,
        grid=(t // TM_ROUTE,),
        in_specs=[pl.BlockSpec((TM_ROUTE, LANES), lambda i: (i, 0))],
        out_specs=[pl.BlockSpec((TM_ROUTE, LANES), lambda i: (i, 0)),
                   pl.BlockSpec((TM_ROUTE, LANES), lambda i: (i, 0)),
                   pl.BlockSpec((8, LANES), lambda i: (0, 0))],
        out_shape=[jax.ShapeDtypeStruct((t, LANES), I32),
                   jax.ShapeDtypeStruct((t, LANES), F32),
                   jax.ShapeDtypeStruct((8, LANES), I32)],
        scratch_shapes=[pltpu.VMEM((1, LANES), F32)],
        compiler_params=_params(("arbitrary",)),
        name="route",
    )(logits)


def _plan(route_i, counts, n_tiles):
    c = counts[0, :N_EXPERTS]
    tiles = (c + TM_EXPERT - 1) // TM_EXPERT
    ctiles = jnp.cumsum(tiles)
    start = (ctiles - tiles) * TM_EXPERT
    eid = jnp.arange(N_EXPERTS, dtype=I32)

    def slot(e, r):
        return jnp.sum(jnp.where(e[:, None] == eid[None, :], start[None, :], 0), axis=1) + r

    pos0 = slot(route_i[:, 0], route_i[:, 2]).astype(I32)
    pos1 = slot(route_i[:, 1], route_i[:, 3]).astype(I32)
    n_used = ctiles[-1]
    tid = jnp.minimum(jnp.arange(n_tiles, dtype=I32), n_used - 1)
    tile_expert = jnp.sum(tid[:, None] >= ctiles[None, :], axis=1).astype(I32)
    return pos0, pos1, tile_expert, n_used.reshape(1).astype(I32)


def _dispatch_body(p0_ref, p1_ref, x_hbm, xs_in_hbm, xs_hbm, sem):
    del xs_in_hbm
    base = pl.program_id(0) * TK_DISPATCH

    def row_copy(t, p_ref):
        return pltpu.make_async_copy(x_hbm.at[pl.ds(base + t, 1)],
                                     xs_hbm.at[pl.ds(p_ref[t], 1)], sem)

    def start(t):
        row_copy(t, p0_ref).start()
        row_copy(t, p1_ref).start()

    def wait(t):
        row_copy(t, p0_ref).wait()
        row_copy(t, p1_ref).wait()

    def prologue(t, c):
        start(t)
        return c

    def steady(t, c):
        start(t)
        wait(t - DISPATCH_WINDOW)
        return c

    def epilogue(t, c):
        wait(t)
        return c

    lax.fori_loop(0, DISPATCH_WINDOW, prologue, 0)
    lax.fori_loop(DISPATCH_WINDOW, TK_DISPATCH, steady, 0, unroll=8)
    lax.fori_loop(TK_DISPATCH - DISPATCH_WINDOW, TK_DISPATCH, epilogue, 0)


def _dispatch_call(x1, pos0, pos1, n_rows):
    t = x1.shape[0]
    xs0 = jnp.zeros((n_rows, D_MODEL), F32)
    return pl.pallas_call(
        _dispatch_body,
        grid=(t // TK_DISPATCH,),
        in_specs=[pl.BlockSpec((TK_DISPATCH,), lambda i: (i,), memory_space=pltpu.SMEM),
                  pl.BlockSpec((TK_DISPATCH,), lambda i: (i,), memory_space=pltpu.SMEM),
                  pl.BlockSpec(memory_space=pl.ANY),
                  pl.BlockSpec(memory_space=pl.ANY)],
        out_specs=pl.BlockSpec(memory_space=pl.ANY),
        out_shape=jax.ShapeDtypeStruct((n_rows, D_MODEL), F32),
        scratch_shapes=[pltpu.SemaphoreType.DMA(())],
        input_output_aliases={3: 0},
        compiler_params=_params(("arbitrary",)),
        name="dispatch",
    )(pos0, pos1, x1, xs0)


def _expert_body(te_ref, nu_ref, x_ref, gn_ref, wg_ref, wu_ref, wd_ref, y_ref,
                 wgb_ref, wub_ref, wdb_ref):
    i = pl.program_id(0)

    @pl.when(i < nu_ref[0])
    def _():
        prev = te_ref[jnp.maximum(i - 1, 0)]

        @pl.when(jnp.logical_or(i == 0, te_ref[i] != prev))
        def _():
            wgb_ref[...] = wg_ref[...].astype(BF16)
            wub_ref[...] = wu_ref[...].astype(BF16)
            wdb_ref[...] = wd_ref[...].astype(BF16)

        h = _rms(x_ref[...], gn_ref[...]).astype(BF16)
        g = jnp.dot(h, wgb_ref[...], preferred_element_type=F32)
        u = jnp.dot(h, wub_ref[...], preferred_element_type=F32)
        hid = (g * (1.0 / (1.0 + jnp.exp(-g))) * u).astype(BF16)
        y_ref[...] = jnp.dot(hid, wdb_ref[...], preferred_element_type=F32)

    @pl.when(i >= nu_ref[0])
    def _():
        y_ref[...] = jnp.zeros_like(y_ref)


def _expert_call(xs, tile_expert, n_used, gn_row, w_gate, w_up, w_down, layer):
    n_rows = xs.shape[0]
    n_tiles = n_rows // TM_EXPERT

    def row_map(i, te, nu):
        return (jnp.minimum(i, nu[0] - 1), 0)

    def w_map(i, te, nu):
        return (layer, te[i], 0, 0)

    grid_spec = pltpu.PrefetchScalarGridSpec(
        num_scalar_prefetch=2,
        grid=(n_tiles,),
        in_specs=[pl.BlockSpec((TM_EXPERT, D_MODEL), row_map),
                  pl.BlockSpec((1, D_MODEL), lambda i, te, nu: (0, 0)),
                  pl.BlockSpec((None, None, D_MODEL, D_FF_EXPERT), w_map),
                  pl.BlockSpec((None, None, D_MODEL, D_FF_EXPERT), w_map),
                  pl.BlockSpec((None, None, D_FF_EXPERT, D_MODEL), w_map)],
        out_specs=pl.BlockSpec((TM_EXPERT, D_MODEL), lambda i, te, nu: (i, 0)),
        scratch_shapes=[pltpu.VMEM((D_MODEL, D_FF_EXPERT), BF16),
                        pltpu.VMEM((D_MODEL, D_FF_EXPERT), BF16),
                        pltpu.VMEM((D_FF_EXPERT, D_MODEL), BF16)],
    )
    return pl.pallas_call(
        _expert_body,
        grid_spec=grid_spec,
        out_shape=jax.ShapeDtypeStruct((n_rows, D_MODEL), F32),
        compiler_params=_params(("arbitrary",)),
        name="experts",
    )(tile_expert, n_used, xs, gn_row, w_gate, w_up, w_down)


def _combine_body(p0c_ref, p1c_ref, p0n_ref, p1n_ref, x1_ref, rw_ref, gn_ref, ys_hbm,
                  x2_ref, h_ref, buf_ref, sems, *, n_steps):
    i = pl.program_id(0)
    slot = lax.rem(i, 2)

    def row_copy(p_ref, k, t, s):
        return pltpu.make_async_copy(ys_hbm.at[pl.ds(p_ref[t], 1)],
                                     buf_ref.at[s, k, pl.ds(t, 1)], sems.at[s])

    def issue(p0_ref, p1_ref, s):
        def body(t, c):
            row_copy(p0_ref, 0, t, s).start()
            row_copy(p1_ref, 1, t, s).start()
            return c
        lax.fori_loop(0, TM_COMBINE, body, 0, unroll=8)

    @pl.when(i == 0)
    def _():
        issue(p0c_ref, p1c_ref, 0)

    @pl.when(i + 1 < n_steps)
    def _():
        issue(p0n_ref, p1n_ref, 1 - slot)

    def drain(t, c):
        row_copy(p0c_ref, 0, t, slot).wait()
        row_copy(p1c_ref, 1, t, slot).wait()
        return c
    lax.fori_loop(0, TM_COMBINE, drain, 0, unroll=8)

    rw = rw_ref[...]
    x2 = x1_ref[...] + rw[:, 0:1] * buf_ref[slot, 0] + rw[:, 1:2] * buf_ref[slot, 1]
    x2_ref[...] = x2
    h_ref[...] = _rms(x2, gn_ref[...]).astype(h_ref.dtype)


def _combine_call(x1, route_w, pos0, pos1, ys, gn_row):
    t = x1.shape[0]
    n_steps = t // TM_COMBINE

    def nxt(i):
        return (jnp.minimum(i + 1, n_steps - 1),)

    smem = functools.partial(pl.BlockSpec, (TM_COMBINE,), memory_space=pltpu.SMEM)
    return pl.pallas_call(
        functools.partial(_combine_body, n_steps=n_steps),
        grid=(n_steps,),
        in_specs=[smem(lambda i: (i,)), smem(lambda i: (i,)), smem(nxt), smem(nxt),
                  pl.BlockSpec((TM_COMBINE, D_MODEL), lambda i: (i, 0)),
                  pl.BlockSpec((TM_COMBINE, LANES), lambda i: (i, 0)),
                  pl.BlockSpec((1, D_MODEL), lambda i: (0, 0)),
                  pl.BlockSpec(memory_space=pl.ANY)],
        out_specs=[pl.BlockSpec((TM_COMBINE, D_MODEL), lambda i: (i, 0)),
                   pl.BlockSpec((TM_COMBINE, D_MODEL), lambda i: (i, 0))],
        out_shape=[jax.ShapeDtypeStruct((t, D_MODEL), F32),
                   jax.ShapeDtypeStruct((t, D_MODEL), BF16)],
        scratch_shapes=[pltpu.VMEM((2, TOP_K, TM_COMBINE, D_MODEL), F32),
                        pltpu.SemaphoreType.DMA((2,))],
        compiler_params=_params(("arbitrary",)),
        name="combine",
    )(pos0, pos1, pos0, pos1, x1, route_w, gn_row, ys)


def kernel(x, mem, norm_mix, norm_mem, norm_ffn, w_in, v_gain, w_spatial, b_spatial,
           q_gain_b, k_gain_b, sinks, q_gain_c, k_gain_c, w_mem_kv, w_branch, w_out,
           w_router_group, b_router_group, w_router_expert, b_router_expert,
           w_gate_e, w_up_e, w_down_e):
    b, s, d = x.shape
    t = b * s
    assert d == D_MODEL and s % BLOCK == 0 and mem.shape[1] == MEM_LEN
    assert t % TK_DISPATCH == 0 and t % TM_PROJ == 0
    n_tiles = (t * TOP_K) // TM_EXPERT + N_EXPERTS
    n_rows = n_tiles * TM_EXPERT

    xf = x.reshape(t, d)
    mem2d = mem.reshape(b * MEM_LEN, d)
    pad = LANES - N_GROUPS - N_EXPERTS

    h = _norm_call(xf, norm_mix[0][None])
    for layer in range(DEPTH):
        kvc = _memkv_call(mem2d, norm_mem[layer][None], w_mem_kv, k_gain_c[layer][None], layer)
        proj = _inproj_call(h, w_in, layer)
        y = _mixer_call(proj, kvc, v_gain[layer].reshape(1, A_WIDTH), w_spatial,
                        b_spatial[layer].T, q_gain_b[layer][None], k_gain_b[layer][None],
                        sinks[layer], q_gain_c[layer][None], layer, s)
        merged = _merge_call(y, proj, w_branch, layer)
        w_router = jnp.pad(jnp.concatenate([w_router_group[layer], w_router_expert[layer]], axis=1),
                           ((0, 0), (0, pad)))
        b_router = jnp.pad(jnp.concatenate([b_router_group[layer], b_router_expert[layer]]),
                           (0, pad))[None]
        x1, logits = _outproj_call(merged, xf, w_out, norm_ffn[layer][None], w_router, b_router,
                                   layer)
        route_i, route_w, counts = _route_call(logits)
        pos0, pos1, tile_expert, n_used = _plan(route_i, counts, n_tiles)
        xs = _dispatch_call(x1, pos0, pos1, n_rows)
        ys = _expert_call(xs, tile_expert, n_used, norm_ffn[layer][None], w_gate_e, w_up_e,
                          w_down_e, layer)
        gn_next = norm_mix[layer + 1] if layer + 1 < DEPTH else norm_mix[layer]
        xf, h = _combine_call(x1, route_w, pos0, pos1, ys, gn_next[None])
    return xf.reshape(b, s, d)
```

```python
import functools

import jax
import jax.numpy as jnp
from jax import lax
from jax.experimental import pallas as pl
from jax.experimental.pallas import tpu as pltpu

F32 = jnp.float32
BF16 = jnp.bfloat16
I32 = jnp.int32

D_MODEL = 2048
DEPTH = 2
MEM_LEN = 256
EPS = 1e-6
BRANCH_WIDTH = 1024
N_BRANCH = 3

CHUNK = 128
A_GROUPS = 8
A_GROUP_DIM = 128
A_WIDTH = A_GROUPS * A_GROUP_DIM

BLOCK = 128
B_HEADS = 16
B_KV_HEADS = 4
B_REP = B_HEADS // B_KV_HEADS
B_HEAD_DIM = 64
B_Q_WIDTH = B_HEADS * B_HEAD_DIM
B_KV_WIDTH = B_KV_HEADS * B_HEAD_DIM

C_HEADS = 4
C_HEAD_DIM = 256
C_WIDTH = C_HEADS * C_HEAD_DIM

COL_U = 0
COL_V = A_WIDTH
COL_QB = 2 * A_WIDTH
COL_KB = COL_QB + B_Q_WIDTH
COL_VB = COL_KB + B_KV_WIDTH
COL_QC = COL_VB + B_KV_WIDTH
COL_GATE = COL_QC + C_WIDTH
IN_WIDTH = COL_GATE + N_BRANCH * D_MODEL
MIX_WIDTH = COL_GATE

N_GROUPS = 4
EXPERTS_PER_GROUP = 4
N_EXPERTS = N_GROUPS * EXPERTS_PER_GROUP
TOP_K = 2
D_FF_EXPERT = 512

LANES = 128
VMEM_LIMIT = 56 * 1024 * 1024
SQRT_HALF = 0.7071067811865476

TM_NORM = 512
TM_PROJ, TN_PROJ = 1024, 512
TM_MERGE, TN_MERGE = 1024, 512
TM_OUT = 512
TM_ROUTE = 512
TK_INVERT = 2048
TM_EXPERT = 256
TM_COMBINE = 256


def _params(sem):
    return pltpu.CompilerParams(dimension_semantics=sem, vmem_limit_bytes=VMEM_LIMIT)


def _rms(x, gain):
    ms = jnp.mean(x * x, axis=-1, keepdims=True)
    return x * lax.rsqrt(ms + EPS) * gain


def _norm_body(x_ref, g_ref, o_ref):
    o_ref[...] = _rms(x_ref[...], g_ref[...]).astype(o_ref.dtype)


def _norm_call(x, gain_row):
    t = x.shape[0]
    return pl.pallas_call(
        _norm_body,
        grid=(t // TM_NORM,),
        in_specs=[pl.BlockSpec((TM_NORM, D_MODEL), lambda i: (i, 0)),
                  pl.BlockSpec((1, D_MODEL), lambda i: (0, 0))],
        out_specs=pl.BlockSpec((TM_NORM, D_MODEL), lambda i: (i, 0)),
        out_shape=jax.ShapeDtypeStruct((t, D_MODEL), BF16),
        compiler_params=_params(("arbitrary",)),
        name="norm",
    )(x, gain_row)


TN_MEMKV = 512


def _memkv_body(mem_ref, gm_ref, w_ref, gk_ref, o_ref):
    n = pl.program_id(0)
    h = _rms(mem_ref[...], gm_ref[...]).astype(BF16)
    acc = jnp.dot(h, w_ref[...].astype(BF16), preferred_element_type=F32)

    @pl.when(n < C_WIDTH // TN_MEMKV)
    def _():
        for j in range(TN_MEMKV // C_HEAD_DIM):
            sl = slice(j * C_HEAD_DIM, (j + 1) * C_HEAD_DIM)
            o_ref[:, sl] = _rms(acc[:, sl], gk_ref[...]).astype(o_ref.dtype)

    @pl.when(n >= C_WIDTH // TN_MEMKV)
    def _():
        o_ref[...] = acc.astype(o_ref.dtype)


def _memkv_call(mem2d, gm_row, w_mem_kv, gk_row, layer):
    rows = mem2d.shape[0]
    return pl.pallas_call(
        _memkv_body,
        grid=(2 * C_WIDTH // TN_MEMKV,),
        in_specs=[pl.BlockSpec((rows, D_MODEL), lambda n: (0, 0)),
                  pl.BlockSpec((1, D_MODEL), lambda n: (0, 0)),
                  pl.BlockSpec((None, D_MODEL, TN_MEMKV), lambda n: (layer, 0, n)),
                  pl.BlockSpec((1, C_HEAD_DIM), lambda n: (0, 0))],
        out_specs=pl.BlockSpec((rows, TN_MEMKV), lambda n: (0, n)),
        out_shape=jax.ShapeDtypeStruct((rows, 2 * C_WIDTH), BF16),
        compiler_params=_params(("arbitrary",)),
        name="memkv",
    )(mem2d, gm_row, w_mem_kv, gk_row)


def _inproj_body(h_ref, w_ref, o_ref, wb_ref):
    n = pl.program_id(0)
    m = pl.program_id(1)

    @pl.when(m == 0)
    def _():
        wb_ref[...] = w_ref[...].astype(BF16)

    acc = jnp.dot(h_ref[...], wb_ref[...], preferred_element_type=F32)
    n_gelu = COL_QB // TN_PROJ
    n_gate = COL_GATE // TN_PROJ

    @pl.when(n < n_gelu)
    def _():
        o_ref[...] = (0.5 * acc * (1.0 + lax.erf(acc * SQRT_HALF))).astype(o_ref.dtype)

    @pl.when(jnp.logical_and(n >= n_gelu, n < n_gate))
    def _():
        o_ref[...] = acc.astype(o_ref.dtype)

    @pl.when(n >= n_gate)
    def _():
        o_ref[...] = (1.0 / (1.0 + jnp.exp(-acc))).astype(o_ref.dtype)


def _inproj_call(h, w_in, layer):
    t = h.shape[0]
    return pl.pallas_call(
        _inproj_body,
        grid=(IN_WIDTH // TN_PROJ, t // TM_PROJ),
        in_specs=[pl.BlockSpec((TM_PROJ, D_MODEL), lambda n, m: (m, 0)),
                  pl.BlockSpec((None, D_MODEL, TN_PROJ), lambda n, m: (layer, 0, n))],
        out_specs=pl.BlockSpec((TM_PROJ, TN_PROJ), lambda n, m: (m, n)),
        out_shape=jax.ShapeDtypeStruct((t, IN_WIDTH), BF16),
        scratch_shapes=[pltpu.VMEM((D_MODEL, TN_PROJ), BF16)],
        compiler_params=_params(("arbitrary", "arbitrary")),
        name="inproj",
    )(h, w_in)


def _mixer_body(cur_ref, prev_ref, kvc_ref, vgain_ref, ws_ref, bs_ref, qgb_ref, kgb_ref,
                sink_ref, qgc_ref, y_ref, *, blocks_per_seq):
    i = pl.program_id(0)
    n = lax.rem(i, blocks_per_seq)

    row = lax.broadcasted_iota(I32, (CHUNK, CHUNK), 0)
    col = lax.broadcasted_iota(I32, (CHUNK, CHUNK), 1)
    causal = col <= row
    for g in range(A_GROUPS):
        sl = slice(g * A_GROUP_DIM, (g + 1) * A_GROUP_DIM)
        u = cur_ref[:, COL_U + g * A_GROUP_DIM:COL_U + (g + 1) * A_GROUP_DIM].astype(F32)
        v = cur_ref[:, COL_V + g * A_GROUP_DIM:COL_V + (g + 1) * A_GROUP_DIM].astype(F32)
        vn = _rms(v, vgain_ref[:, sl]).astype(BF16)
        w = jnp.where(causal, ws_ref[g], 0.0).astype(BF16)
        mixed = jnp.dot(w, vn, preferred_element_type=F32) + bs_ref[:, g:g + 1]
        y_ref[:, sl] = (u * mixed).astype(y_ref.dtype)

    qi = lax.broadcasted_iota(I32, (B_REP * BLOCK, 2 * BLOCK), 0) & (BLOCK - 1)
    kj = lax.broadcasted_iota(I32, (B_REP * BLOCK, 2 * BLOCK), 1)
    rel = BLOCK + qi - kj
    valid = (rel >= 0) & (rel < BLOCK) & ((n > 0) | (kj >= BLOCK))
    q_all = cur_ref[:, COL_QB:COL_QB + B_Q_WIDTH].astype(F32)
    k_all = jnp.concatenate([prev_ref[:, 0:B_KV_WIDTH], cur_ref[:, COL_KB:COL_KB + B_KV_WIDTH]],
                            axis=0).astype(F32)
    v_all = jnp.concatenate([prev_ref[:, B_KV_WIDTH:2 * B_KV_WIDTH],
                             cur_ref[:, COL_VB:COL_VB + B_KV_WIDTH]], axis=0)
    qgain = qgb_ref[...] * (B_HEAD_DIM ** -0.5)
    for kv in range(B_KV_HEADS):
        ksl = slice(kv * B_HEAD_DIM, (kv + 1) * B_HEAD_DIM)
        kh = _rms(k_all[:, ksl], kgb_ref[...]).astype(BF16)
        vh = v_all[:, ksl]
        qs, sk = [], []
        for r in range(B_REP):
            h = kv * B_REP + r
            qh = q_all[:, h * B_HEAD_DIM:(h + 1) * B_HEAD_DIM]
            qs.append(_rms(qh, qgain).astype(BF16))
            sk.append(jnp.full((BLOCK, 1), sink_ref[h], F32))
        q4 = jnp.concatenate(qs, axis=0)
        sink = jnp.concatenate(sk, axis=0)
        s = lax.dot_general(q4, kh, (((1,), (1,)), ((), ())), preferred_element_type=F32)
        s = jnp.where(valid, s, -jnp.inf)
        mx = jnp.maximum(jnp.max(s, axis=-1, keepdims=True), sink)
        p = jnp.exp(s - mx)
        denom = jnp.sum(p, axis=-1, keepdims=True) + jnp.exp(sink - mx)
        probs = (p / denom).astype(BF16)
        o = jnp.dot(probs, vh, preferred_element_type=F32)
        for r in range(B_REP):
            h = kv * B_REP + r
            y_ref[:, A_WIDTH + h * B_HEAD_DIM:A_WIDTH + (h + 1) * B_HEAD_DIM] = (
                o[r * BLOCK:(r + 1) * BLOCK].astype(y_ref.dtype))

    qgc = qgc_ref[...] * (C_HEAD_DIM ** -0.5)
    for h in range(C_HEADS):
        sl = slice(h * C_HEAD_DIM, (h + 1) * C_HEAD_DIM)
        qh = _rms(cur_ref[:, COL_QC + h * C_HEAD_DIM:COL_QC + (h + 1) * C_HEAD_DIM].astype(F32),
                  qgc).astype(BF16)
        kh = kvc_ref[:, h * C_HEAD_DIM:(h + 1) * C_HEAD_DIM]
        vh = kvc_ref[:, C_WIDTH + h * C_HEAD_DIM:C_WIDTH + (h + 1) * C_HEAD_DIM]
        s = lax.dot_general(qh, kh, (((1,), (1,)), ((), ())), preferred_element_type=F32)
        p = jnp.exp(s - jnp.max(s, axis=-1, keepdims=True))
        probs = (p / jnp.sum(p, axis=-1, keepdims=True)).astype(BF16)
        o = jnp.dot(probs, vh, preferred_element_type=F32)
        y_ref[:, A_WIDTH + B_Q_WIDTH + h * C_HEAD_DIM:
              A_WIDTH + B_Q_WIDTH + (h + 1) * C_HEAD_DIM] = o.astype(y_ref.dtype)


def _mixer_call(proj, kvc, vgain_row, w_spatial, bs_t, qgb_row, kgb_row, sinks, qgc_row,
                layer, seq):
    t = proj.shape[0]
    bps = seq // BLOCK
    kvb = COL_KB // (2 * B_KV_WIDTH)
    return pl.pallas_call(
        functools.partial(_mixer_body, blocks_per_seq=bps),
        grid=(t // BLOCK,),
        in_specs=[
            pl.BlockSpec((BLOCK, MIX_WIDTH), lambda i: (i, 0)),
            pl.BlockSpec((BLOCK, 2 * B_KV_WIDTH), lambda i: (jnp.maximum(i - 1, 0), kvb)),
            pl.BlockSpec((MEM_LEN, 2 * C_WIDTH), lambda i: (i // bps, 0)),
            pl.BlockSpec((1, A_WIDTH), lambda i: (0, 0)),
            pl.BlockSpec((None, A_GROUPS, CHUNK, CHUNK), lambda i: (layer, 0, 0, 0)),
            pl.BlockSpec((CHUNK, A_GROUPS), lambda i: (0, 0)),
            pl.BlockSpec((1, B_HEAD_DIM), lambda i: (0, 0)),
            pl.BlockSpec((1, B_HEAD_DIM), lambda i: (0, 0)),
            pl.BlockSpec(memory_space=pltpu.SMEM),
            pl.BlockSpec((1, C_HEAD_DIM), lambda i: (0, 0)),
        ],
        out_specs=pl.BlockSpec((BLOCK, N_BRANCH * BRANCH_WIDTH), lambda i: (i, 0)),
        out_shape=jax.ShapeDtypeStruct((t, N_BRANCH * BRANCH_WIDTH), BF16),
        compiler_params=_params(("arbitrary",)),
        name="mixer",
    )(proj, proj, kvc, vgain_row, w_spatial, bs_t, qgb_row, kgb_row, sinks, qgc_row)


def _merge_body(y_ref, g0_ref, g1_ref, g2_ref, w_ref, o_ref, wb_ref):
    m = pl.program_id(1)

    @pl.when(m == 0)
    def _():
        wb_ref[...] = w_ref[...].astype(BF16)

    acc = None
    for b, g_ref in enumerate((g0_ref, g1_ref, g2_ref)):
        wide = jnp.dot(y_ref[:, b * BRANCH_WIDTH:(b + 1) * BRANCH_WIDTH], wb_ref[b],
                       preferred_element_type=F32)
        term = g_ref[...].astype(F32) * wide
        acc = term if acc is None else acc + term
    o_ref[...] = acc.astype(o_ref.dtype)


def _merge_call(y, proj, w_branch, layer):
    t = y.shape[0]
    gate0 = COL_GATE // TN_MERGE
    per = D_MODEL // TN_MERGE

    def gate_spec(b):
        return pl.BlockSpec((TM_MERGE, TN_MERGE), lambda n, m: (m, gate0 + b * per + n))

    return pl.pallas_call(
        _merge_body,
        grid=(D_MODEL // TN_MERGE, t // TM_MERGE),
        in_specs=[pl.BlockSpec((TM_MERGE, N_BRANCH * BRANCH_WIDTH), lambda n, m: (m, 0)),
                  gate_spec(0), gate_spec(1), gate_spec(2),
                  pl.BlockSpec((None, N_BRANCH, BRANCH_WIDTH, TN_MERGE),
                               lambda n, m: (layer, 0, 0, n))],
        out_specs=pl.BlockSpec((TM_MERGE, TN_MERGE), lambda n, m: (m, n)),
        out_shape=jax.ShapeDtypeStruct((t, D_MODEL), BF16),
        scratch_shapes=[pltpu.VMEM((N_BRANCH, BRANCH_WIDTH, TN_MERGE), BF16)],
        compiler_params=_params(("arbitrary", "arbitrary")),
        name="merge",
    )(y, proj, proj, proj, w_branch)


def _outproj_body(mg_ref, x_ref, w_ref, gn_ref, wr_ref, br_ref, x1_ref, lg_ref,
                  wb_ref, wrh_ref, wrl_ref):
    i = pl.program_id(0)

    @pl.when(i == 0)
    def _():
        wb_ref[...] = w_ref[...].astype(BF16)
        wr = wr_ref[...]
        hi = wr.astype(BF16)
        wrh_ref[...] = hi
        wrl_ref[...] = (wr - hi.astype(F32)).astype(BF16)

    x1 = x_ref[...] + jnp.dot(mg_ref[...], wb_ref[...], preferred_element_type=F32)
    x1_ref[...] = x1
    h2 = _rms(x1, gn_ref[...])
    hi = h2.astype(BF16)
    lo = (h2 - hi.astype(F32)).astype(BF16)
    lg = (jnp.dot(hi, wrh_ref[...], preferred_element_type=F32)
          + jnp.dot(lo, wrh_ref[...], preferred_element_type=F32)
          + jnp.dot(hi, wrl_ref[...], preferred_element_type=F32))
    lg_ref[...] = lg + br_ref[...]


def _outproj_call(merged, x, w_out, gn_row, w_router, b_router, layer):
    t = x.shape[0]
    return pl.pallas_call(
        _outproj_body,
        grid=(t // TM_OUT,),
        in_specs=[pl.BlockSpec((TM_OUT, D_MODEL), lambda i: (i, 0)),
                  pl.BlockSpec((TM_OUT, D_MODEL), lambda i: (i, 0)),
                  pl.BlockSpec((None, D_MODEL, D_MODEL), lambda i: (layer, 0, 0),
                               pipeline_mode=pl.Buffered(1)),
                  pl.BlockSpec((1, D_MODEL), lambda i: (0, 0)),
                  pl.BlockSpec((D_MODEL, LANES), lambda i: (0, 0)),
                  pl.BlockSpec((1, LANES), lambda i: (0, 0))],
        out_specs=[pl.BlockSpec((TM_OUT, D_MODEL), lambda i: (i, 0)),
                   pl.BlockSpec((TM_OUT, LANES), lambda i: (i, 0))],
        out_shape=[jax.ShapeDtypeStruct((t, D_MODEL), F32),
                   jax.ShapeDtypeStruct((t, LANES), F32)],
        scratch_shapes=[pltpu.VMEM((D_MODEL, D_MODEL), BF16),
                        pltpu.VMEM((D_MODEL, LANES), BF16),
                        pltpu.VMEM((D_MODEL, LANES), BF16)],
        compiler_params=_params(("arbitrary",)),
        name="outproj",
    )(merged, x, w_out, gn_row, w_router, b_router)


def _route_body(lg_ref, ri_ref, rw_ref, cnt_ref, carry_ref):
    i = pl.program_id(0)

    @pl.when(i == 0)
    def _():
        carry_ref[...] = jnp.zeros_like(carry_ref)

    lg = lg_ref[...]
    tm = lg.shape[0]
    lane = lax.broadcasted_iota(I32, lg.shape, 1)
    neg = -jnp.inf
    big = jnp.int32(LANES)

    is_g = lane < N_GROUPS
    gl = jnp.where(is_g, lg, neg)
    gmax = jnp.max(gl, axis=-1, keepdims=True)
    gidx = jnp.min(jnp.where(gl == gmax, lane, big), axis=-1, keepdims=True)
    p_g = 1.0 / jnp.sum(jnp.where(is_g, jnp.exp(lg - gmax), 0.0), axis=-1, keepdims=True)

    e_lane = lane - N_GROUPS
    in_group = (e_lane >= 0) & (e_lane < N_EXPERTS) & ((e_lane >> 2) == gidx)
    el = jnp.where(in_group, lg, neg)
    v0 = jnp.max(el, axis=-1, keepdims=True)
    i0 = jnp.min(jnp.where(el == v0, lane, big), axis=-1, keepdims=True)
    el1 = jnp.where(lane == i0, neg, el)
    v1 = jnp.max(el1, axis=-1, keepdims=True)
    i1 = jnp.min(jnp.where(el1 == v1, lane, big), axis=-1, keepdims=True)
    e0 = i0 - N_GROUPS
    e1 = i1 - N_GROUPS
    ex = jnp.exp(v1 - v0)
    w0 = p_g / (1.0 + ex)
    w1 = p_g * ex / (1.0 + ex)

    oh0 = lane == e0
    oh1 = lane == e1
    oh = jnp.where(oh0 | oh1, 1.0, 0.0)
    r_i = lax.broadcasted_iota(I32, (tm, tm), 0)
    c_i = lax.broadcasted_iota(I32, (tm, tm), 1)
    lower = jnp.where(c_i < r_i, 1.0, 0.0).astype(BF16)
    before = jnp.dot(lower, oh.astype(BF16), preferred_element_type=F32) + carry_ref[...]
    rank0 = jnp.sum(jnp.where(oh0, before, 0.0), axis=-1, keepdims=True).astype(I32)
    rank1 = jnp.sum(jnp.where(oh1, before, 0.0), axis=-1, keepdims=True).astype(I32)
    carry_ref[...] = carry_ref[...] + jnp.sum(oh, axis=0, keepdims=True)

    zero_i = jnp.zeros_like(lane)
    ri_ref[...] = jnp.where(lane == 0, e0, jnp.where(lane == 1, e1,
                            jnp.where(lane == 2, rank0, jnp.where(lane == 3, rank1, zero_i))))
    rw_ref[...] = jnp.where(lane == 0, w0, jnp.where(lane == 1, w1, 0.0))
    cnt_ref[...] = jnp.broadcast_to(carry_ref[...], cnt_ref.shape).astype(I32)


def _route_call(logits):
    t = logits.shape[0]
    return pl.pallas_call(
        _route_body,
        grid=(t // TM_ROUTE,),
        in_specs=[pl.BlockSpec((TM_ROUTE, LANES), lambda i: (i, 0))],
        out_specs=[pl.BlockSpec((TM_ROUTE, LANES), lambda i: (i, 0)),
                   pl.BlockSpec((TM_ROUTE, LANES), lambda i: (i, 0)),
                   pl.BlockSpec((8, LANES), lambda i: (0, 0))],
        out_shape=[jax.ShapeDtypeStruct((t, LANES), I32),
                   jax.ShapeDtypeStruct((t, LANES), F32),
                   jax.ShapeDtypeStruct((8, LANES), I32)],
        scratch_shapes=[pltpu.VMEM((1, LANES), F32)],
        compiler_params=_params(("arbitrary",)),
        name="route",
    )(logits)


def _plan(route_i, counts, n_tiles):
    c = counts[0, :N_EXPERTS]
    tiles = (c + TM_EXPERT - 1) // TM_EXPERT
    ctiles = jnp.cumsum(tiles)
    start = (ctiles - tiles) * TM_EXPERT
    eid = jnp.arange(N_EXPERTS, dtype=I32)

    def slot(e, r):
        return jnp.sum(jnp.where(e[:, None] == eid[None, :], start[None, :], 0), axis=1) + r

    pos0 = slot(route_i[:, 0], route_i[:, 2]).astype(I32)
    pos1 = slot(route_i[:, 1], route_i[:, 3]).astype(I32)
    n_used = ctiles[-1]
    tid = jnp.minimum(jnp.arange(n_tiles, dtype=I32), n_used - 1)
    tile_expert = jnp.sum(tid[:, None] >= ctiles[None, :], axis=1).astype(I32)
    return pos0, pos1, tile_expert, n_used.reshape(1).astype(I32)


def _invert_body(p0_ref, p1_ref, src_ref):
    i = pl.program_id(0)

    @pl.when(i == 0)
    def _():
        def fill(r, c):
            src_ref[r] = 0
            return c
        lax.fori_loop(0, src_ref.shape[0], fill, 0, unroll=8)

    base = i * TK_INVERT

    def scatter(t, c):
        src_ref[p0_ref[t]] = base + t
        src_ref[p1_ref[t]] = base + t
        return c
    lax.fori_loop(0, TK_INVERT, scatter, 0, unroll=8)


def _invert_call(pos0, pos1, n_rows):
    t = pos0.shape[0]
    smem = pl.BlockSpec((TK_INVERT,), lambda i: (i,), memory_space=pltpu.SMEM)
    return pl.pallas_call(
        _invert_body,
        grid=(t // TK_INVERT,),
        in_specs=[smem, smem],
        out_specs=pl.BlockSpec(memory_space=pltpu.SMEM),
        out_shape=jax.ShapeDtypeStruct((n_rows,), I32),
        compiler_params=_params(("arbitrary",)),
        name="invert",
    )(pos0, pos1)


def _expert_body(te_ref, nu_ref, srcc_ref, srcn_ref, x_hbm, gn_ref, wg_ref, wu_ref, wd_ref,
                 y_ref, xbuf_ref, wgb_ref, wub_ref, wdb_ref, sems):
    i = pl.program_id(0)
    n_used = nu_ref[0]
    slot = lax.rem(i, 2)

    def row_copy(src_ref, r, s):
        return pltpu.make_async_copy(x_hbm.at[pl.ds(src_ref[r], 1)],
                                     xbuf_ref.at[s, pl.ds(r, 1)], sems.at[s])

    def issue(src_ref, s):
        def body(r, c):
            row_copy(src_ref, r, s).start()
            return c
        lax.fori_loop(0, TM_EXPERT, body, 0, unroll=8)

    @pl.when(i == 0)
    def _():
        issue(srcc_ref, 0)

    @pl.when(i + 1 < n_used)
    def _():
        issue(srcn_ref, 1 - slot)

    @pl.when(i < n_used)
    def _():
        prev = te_ref[jnp.maximum(i - 1, 0)]

        @pl.when(jnp.logical_or(i == 0, te_ref[i] != prev))
        def _():
            wgb_ref[...] = wg_ref[...].astype(BF16)
            wub_ref[...] = wu_ref[...].astype(BF16)
            wdb_ref[...] = wd_ref[...].astype(BF16)

        def drain(r, c):
            row_copy(srcc_ref, r, slot).wait()
            return c
        lax.fori_loop(0, TM_EXPERT, drain, 0, unroll=8)

        h = _rms(xbuf_ref[slot], gn_ref[...]).astype(BF16)
        g = jnp.dot(h, wgb_ref[...], preferred_element_type=F32)
        u = jnp.dot(h, wub_ref[...], preferred_element_type=F32)
        hid = (g * (1.0 / (1.0 + jnp.exp(-g))) * u).astype(BF16)
        y_ref[...] = jnp.dot(hid, wdb_ref[...], preferred_element_type=F32)

    @pl.when(i >= n_used)
    def _():
        y_ref[...] = jnp.zeros_like(y_ref)


def _expert_call(x1, src, tile_expert, n_used, gn_row, w_gate, w_up, w_down, layer):
    n_rows = src.shape[0]
    n_tiles = n_rows // TM_EXPERT

    def w_map(i, te, nu):
        return (layer, te[i], 0, 0)

    grid_spec = pltpu.PrefetchScalarGridSpec(
        num_scalar_prefetch=2,
        grid=(n_tiles,),
        in_specs=[pl.BlockSpec((TM_EXPERT,), lambda i, te, nu: (i,), memory_space=pltpu.SMEM),
                  pl.BlockSpec((TM_EXPERT,), lambda i, te, nu: (jnp.minimum(i + 1, n_tiles - 1),),
                               memory_space=pltpu.SMEM),
                  pl.BlockSpec(memory_space=pl.ANY),
                  pl.BlockSpec((1, D_MODEL), lambda i, te, nu: (0, 0)),
                  pl.BlockSpec((None, None, D_MODEL, D_FF_EXPERT), w_map),
                  pl.BlockSpec((None, None, D_MODEL, D_FF_EXPERT), w_map),
                  pl.BlockSpec((None, None, D_FF_EXPERT, D_MODEL), w_map)],
        out_specs=pl.BlockSpec((TM_EXPERT, D_MODEL), lambda i, te, nu: (i, 0)),
        scratch_shapes=[pltpu.VMEM((2, TM_EXPERT, D_MODEL), F32),
                        pltpu.VMEM((D_MODEL, D_FF_EXPERT), BF16),
                        pltpu.VMEM((D_MODEL, D_FF_EXPERT), BF16),
                        pltpu.VMEM((D_FF_EXPERT, D_MODEL), BF16),
                        pltpu.SemaphoreType.DMA((2,))],
    )
    return pl.pallas_call(
        _expert_body,
        grid_spec=grid_spec,
        out_shape=jax.ShapeDtypeStruct((n_rows, D_MODEL), F32),
        compiler_params=_params(("arbitrary",)),
        name="experts",
    )(tile_expert, n_used, src, src, x1, gn_row, w_gate, w_up, w_down)


def _combine_body(p0c_ref, p1c_ref, p0n_ref, p1n_ref, x1_ref, rw_ref, gn_ref, ys_hbm,
                  x2_ref, h_ref, buf_ref, sems, *, n_steps):
    i = pl.program_id(0)
    slot = lax.rem(i, 2)

    def row_copy(p_ref, k, t, s):
        return pltpu.make_async_copy(ys_hbm.at[pl.ds(p_ref[t], 1)],
                                     buf_ref.at[s, k, pl.ds(t, 1)], sems.at[s])

    def issue(p0_ref, p1_ref, s):
        def body(t, c):
            row_copy(p0_ref, 0, t, s).start()
            row_copy(p1_ref, 1, t, s).start()
            return c
        lax.fori_loop(0, TM_COMBINE, body, 0, unroll=8)

    @pl.when(i == 0)
    def _():
        issue(p0c_ref, p1c_ref, 0)

    @pl.when(i + 1 < n_steps)
    def _():
        issue(p0n_ref, p1n_ref, 1 - slot)

    def drain(t, c):
        row_copy(p0c_ref, 0, t, slot).wait()
        row_copy(p1c_ref, 1, t, slot).wait()
        return c
    lax.fori_loop(0, TM_COMBINE, drain, 0, unroll=8)

    rw = rw_ref[...]
    x2 = x1_ref[...] + rw[:, 0:1] * buf_ref[slot, 0] + rw[:, 1:2] * buf_ref[slot, 1]
    x2_ref[...] = x2
    h_ref[...] = _rms(x2, gn_ref[...]).astype(h_ref.dtype)


def _combine_call(x1, route_w, pos0, pos1, ys, gn_row):
    t = x1.shape[0]
    n_steps = t // TM_COMBINE

    def nxt(i):
        return (jnp.minimum(i + 1, n_steps - 1),)

    smem = functools.partial(pl.BlockSpec, (TM_COMBINE,), memory_space=pltpu.SMEM)
    return pl.pallas_call(
        functools.partial(_combine_body, n_steps=n_steps),
        grid=(n_steps,),
        in_specs=[smem(lambda i: (i,)), smem(lambda i: (i,)), smem(nxt), smem(nxt),
                  pl.BlockSpec((TM_COMBINE, D_MODEL), lambda i: (i, 0)),
                  pl.BlockSpec((TM_COMBINE, LANES), lambda i: (i, 0)),
                  pl.BlockSpec((1, D_MODEL), lambda i: (0, 0)),
                  pl.BlockSpec(memory_space=pl.ANY)],
        out_specs=[pl.BlockSpec((TM_COMBINE, D_MODEL), lambda i: (i, 0)),
                   pl.BlockSpec((TM_COMBINE, D_MODEL), lambda i: (i, 0))],
        out_shape=[jax.ShapeDtypeStruct((t, D_MODEL), F32),
                   jax.ShapeDtypeStruct((t, D_MODEL), BF16)],
        scratch_shapes=[pltpu.VMEM((2, TOP_K, TM_COMBINE, D_MODEL), F32),
                        pltpu.SemaphoreType.DMA((2,))],
        compiler_params=_params(("arbitrary",)),
        name="combine",
    )(pos0, pos1, pos0, pos1, x1, route_w, gn_row, ys)


def kernel(x, mem, norm_mix, norm_mem, norm_ffn, w_in, v_gain, w_spatial, b_spatial,
           q_gain_b, k_gain_b, sinks, q_gain_c, k_gain_c, w_mem_kv, w_branch, w_out,
           w_router_group, b_router_group, w_router_expert, b_router_expert,
           w_gate_e, w_up_e, w_down_e):
    b, s, d = x.shape
    t = b * s
    assert d == D_MODEL and s % BLOCK == 0 and mem.shape[1] == MEM_LEN
    assert t % TK_INVERT == 0 and t % TM_PROJ == 0
    n_tiles = (t * TOP_K) // TM_EXPERT + N_EXPERTS
    n_rows = n_tiles * TM_EXPERT

    xf = x.reshape(t, d)
    mem2d = mem.reshape(b * MEM_LEN, d)
    pad = LANES - N_GROUPS - N_EXPERTS

    h = _norm_call(xf, norm_mix[0][None])
    for layer in range(DEPTH):
        kvc = _memkv_call(mem2d, norm_mem[layer][None], w_mem_kv, k_gain_c[layer][None], layer)
        proj = _inproj_call(h, w_in, layer)
        y = _mixer_call(proj, kvc, v_gain[layer].reshape(1, A_WIDTH), w_spatial,
                        b_spatial[layer].T, q_gain_b[layer][None], k_gain_b[layer][None],
                        sinks[layer], q_gain_c[layer][None], layer, s)
        merged = _merge_call(y, proj, w_branch, layer)
        w_router = jnp.pad(jnp.concatenate([w_router_group[layer], w_router_expert[layer]], axis=1),
                           ((0, 0), (0, pad)))
        b_router = jnp.pad(jnp.concatenate([b_router_group[layer], b_router_expert[layer]]),
                           (0, pad))[None]
        x1, logits = _outproj_call(merged, xf, w_out, norm_ffn[layer][None], w_router, b_router,
                                   layer)
        route_i, route_w, counts = _route_call(logits)
        pos0, pos1, tile_expert, n_used = _plan(route_i, counts, n_tiles)
        src = _invert_call(pos0, pos1, n_rows)
        ys = _expert_call(x1, src, tile_expert, n_used, norm_ffn[layer][None], w_gate_e, w_up_e,
                          w_down_e, layer)
        gn_next = norm_mix[layer + 1] if layer + 1 < DEPTH else norm_mix[layer]
        xf, h = _combine_call(x1, route_w, pos0, pos1, ys, gn_next[None])
    return xf.reshape(b, s, d)
```

```python
import functools

import jax
import jax.numpy as jnp
from jax import lax
from jax.experimental import pallas as pl
from jax.experimental.pallas import tpu as pltpu

F32 = jnp.float32
BF16 = jnp.bfloat16
I32 = jnp.int32

D_MODEL = 2048
DEPTH = 2
MEM_LEN = 256
EPS = 1e-6
BRANCH_WIDTH = 1024
N_BRANCH = 3

CHUNK = 128
A_GROUPS = 8
A_GROUP_DIM = 128
A_WIDTH = A_GROUPS * A_GROUP_DIM

BLOCK = 128
B_HEADS = 16
B_KV_HEADS = 4
B_REP = B_HEADS // B_KV_HEADS
B_HEAD_DIM = 64
B_Q_WIDTH = B_HEADS * B_HEAD_DIM
B_KV_WIDTH = B_KV_HEADS * B_HEAD_DIM

C_HEADS = 4
C_HEAD_DIM = 256
C_WIDTH = C_HEADS * C_HEAD_DIM

COL_U = 0
COL_V = A_WIDTH
COL_QB = 2 * A_WIDTH
COL_KB = COL_QB + B_Q_WIDTH
COL_VB = COL_KB + B_KV_WIDTH
COL_QC = COL_VB + B_KV_WIDTH
COL_GATE = COL_QC + C_WIDTH
IN_WIDTH = COL_GATE + N_BRANCH * D_MODEL
SEC_A_WIDTH = COL_QB
SEC_B_WIDTH = COL_GATE - COL_QB
SEC_G_WIDTH = N_BRANCH * D_MODEL
PB_Q = 0
PB_K = COL_KB - COL_QB
PB_V = COL_VB - COL_QB
PB_QC = COL_QC - COL_QB

N_GROUPS = 4
EXPERTS_PER_GROUP = 4
N_EXPERTS = N_GROUPS * EXPERTS_PER_GROUP
TOP_K = 2
D_FF_EXPERT = 512

LANES = 128
VMEM_LIMIT = 56 * 1024 * 1024
SQRT_HALF = 0.7071067811865476

TM_NORM = 512
TM_PROJ, TN_PROJ = 1024, 512
TM_MERGE, TN_MERGE = 1024, 512
TM_OUT = 512
TM_ROUTE = 512
TK_INVERT = 2048
TM_EXPERT = 256
TM_COMBINE = 256


def _params(sem):
    return pltpu.CompilerParams(dimension_semantics=sem, vmem_limit_bytes=VMEM_LIMIT)


def _rms(x, gain):
    ms = jnp.mean(x * x, axis=-1, keepdims=True)
    return x * lax.rsqrt(ms + EPS) * gain


def _norm_body(x_ref, g_ref, o_ref):
    o_ref[...] = _rms(x_ref[...], g_ref[...]).astype(o_ref.dtype)


def _norm_call(x, gain_row):
    t = x.shape[0]
    return pl.pallas_call(
        _norm_body,
        grid=(t // TM_NORM,),
        in_specs=[pl.BlockSpec((TM_NORM, D_MODEL), lambda i: (i, 0)),
                  pl.BlockSpec((1, D_MODEL), lambda i: (0, 0))],
        out_specs=pl.BlockSpec((TM_NORM, D_MODEL), lambda i: (i, 0)),
        out_shape=jax.ShapeDtypeStruct((t, D_MODEL), BF16),
        compiler_params=_params(("arbitrary",)),
        name="norm",
    )(x, gain_row)


TN_MEMKV = 512


def _memkv_body(mem_ref, gm_ref, w_ref, gk_ref, o_ref):
    n = pl.program_id(0)
    h = _rms(mem_ref[...], gm_ref[...]).astype(BF16)
    acc = jnp.dot(h, w_ref[...].astype(BF16), preferred_element_type=F32)

    @pl.when(n < C_WIDTH // TN_MEMKV)
    def _():
        for j in range(TN_MEMKV // C_HEAD_DIM):
            sl = slice(j * C_HEAD_DIM, (j + 1) * C_HEAD_DIM)
            o_ref[:, sl] = _rms(acc[:, sl], gk_ref[...]).astype(o_ref.dtype)

    @pl.when(n >= C_WIDTH // TN_MEMKV)
    def _():
        o_ref[...] = acc.astype(o_ref.dtype)


def _memkv_call(mem2d, gm_row, w_mem_kv, gk_row, layer):
    rows = mem2d.shape[0]
    return pl.pallas_call(
        _memkv_body,
        grid=(2 * C_WIDTH // TN_MEMKV,),
        in_specs=[pl.BlockSpec((rows, D_MODEL), lambda n: (0, 0)),
                  pl.BlockSpec((1, D_MODEL), lambda n: (0, 0)),
                  pl.BlockSpec((None, D_MODEL, TN_MEMKV), lambda n: (layer, 0, n)),
                  pl.BlockSpec((1, C_HEAD_DIM), lambda n: (0, 0))],
        out_specs=pl.BlockSpec((rows, TN_MEMKV), lambda n: (0, n)),
        out_shape=jax.ShapeDtypeStruct((rows, 2 * C_WIDTH), BF16),
        compiler_params=_params(("arbitrary",)),
        name="memkv",
    )(mem2d, gm_row, w_mem_kv, gk_row)


def _gelu(x):
    return 0.5 * x * (1.0 + lax.erf(x * SQRT_HALF))


def _sigmoid(x):
    return 1.0 / (1.0 + jnp.exp(-x))


def _identity(x):
    return x


def _inproj_body(h_ref, w_ref, o_ref, wb_ref, *, act):
    @pl.when(pl.program_id(1) == 0)
    def _():
        wb_ref[...] = w_ref[...].astype(BF16)

    acc = jnp.dot(h_ref[...], wb_ref[...], preferred_element_type=F32)
    o_ref[...] = act(acc).astype(o_ref.dtype)


def _inproj_call(h, w_in, layer, col0, width, act, name):
    t = h.shape[0]
    n0 = col0 // TN_PROJ
    return pl.pallas_call(
        functools.partial(_inproj_body, act=act),
        grid=(width // TN_PROJ, t // TM_PROJ),
        in_specs=[pl.BlockSpec((TM_PROJ, D_MODEL), lambda n, m: (m, 0)),
                  pl.BlockSpec((None, D_MODEL, TN_PROJ), lambda n, m: (layer, 0, n0 + n))],
        out_specs=pl.BlockSpec((TM_PROJ, TN_PROJ), lambda n, m: (m, n)),
        out_shape=jax.ShapeDtypeStruct((t, width), BF16),
        scratch_shapes=[pltpu.VMEM((D_MODEL, TN_PROJ), BF16)],
        compiler_params=_params(("arbitrary", "arbitrary")),
        name=name,
    )(h, w_in)


def _group_sums(width, group):
    gid = jnp.arange(width, dtype=I32) // group
    s = (gid[:, None] == jnp.arange(LANES, dtype=I32)[None, :]).astype(BF16)
    return s, s.T


def _group_scale(x, s, e, group):
    ssq = jnp.dot((x * x).astype(BF16), s, preferred_element_type=F32)
    r = lax.rsqrt(ssq * (1.0 / group) + EPS)
    r_hi = r.astype(BF16)
    r_lo = (r - r_hi.astype(F32)).astype(BF16)
    return (jnp.dot(r_hi, e, preferred_element_type=F32)
            + jnp.dot(r_lo, e, preferred_element_type=F32))


def _mixer_body(a_ref, b_ref, prev_ref, kvc_ref, vgain_ref, ws_ref, bs_ref, qgb_ref, kgb_ref,
                sink_ref, qgc_ref, s64_ref, e64_ref, s128_ref, e128_ref, s256_ref, e256_ref,
                y_ref, *, blocks_per_seq):
    i = pl.program_id(0)
    n = lax.rem(i, blocks_per_seq)
    ones_k = jnp.ones((2 * BLOCK, LANES), BF16)

    row = lax.broadcasted_iota(I32, (CHUNK, CHUNK), 0)
    col = lax.broadcasted_iota(I32, (CHUNK, CHUNK), 1)
    causal = col <= row
    v_all = a_ref[:, COL_V:COL_V + A_WIDTH].astype(F32)
    vn_all = (v_all * _group_scale(v_all, s128_ref[...], e128_ref[...], A_GROUP_DIM)
              * vgain_ref[...]).astype(BF16)
    for g in range(A_GROUPS):
        sl = slice(g * A_GROUP_DIM, (g + 1) * A_GROUP_DIM)
        u = a_ref[:, COL_U + g * A_GROUP_DIM:COL_U + (g + 1) * A_GROUP_DIM].astype(F32)
        w = jnp.where(causal, ws_ref[g], 0.0).astype(BF16)
        mixed = jnp.dot(w, vn_all[:, sl], preferred_element_type=F32) + bs_ref[g]
        y_ref[:, sl] = (u * mixed).astype(y_ref.dtype)

    lane_k = lax.broadcasted_iota(I32, (2 * BLOCK, LANES), 1)
    lane_q = lax.broadcasted_iota(I32, (BLOCK, LANES), 1)
    lo_k = lane_k < B_HEAD_DIM
    lo_q = lane_q < B_HEAD_DIM
    qi = lax.broadcasted_iota(I32, (B_REP * BLOCK, 2 * BLOCK), 0) & (BLOCK - 1)
    kj = lax.broadcasted_iota(I32, (B_REP * BLOCK, 2 * BLOCK), 1)
    rel = BLOCK + qi - kj
    valid = (rel >= 0) & (rel < BLOCK) & ((n > 0) | (kj >= BLOCK))

    q_all = b_ref[:, PB_Q:PB_Q + B_Q_WIDTH].astype(F32)
    qn_all = (q_all * _group_scale(q_all, s64_ref[...], e64_ref[...], B_HEAD_DIM)
              * (qgb_ref[...] * (B_HEAD_DIM ** -0.5)))
    k_all = jnp.concatenate([prev_ref[:, 0:B_KV_WIDTH], b_ref[:, PB_K:PB_K + B_KV_WIDTH]],
                            axis=0).astype(F32)
    kn_all = (k_all * _group_scale(k_all, s64_ref[0:B_KV_WIDTH, :], e64_ref[:, 0:B_KV_WIDTH],
                                   B_HEAD_DIM) * kgb_ref[...])
    v_kv = jnp.concatenate([prev_ref[:, B_KV_WIDTH:2 * B_KV_WIDTH],
                            b_ref[:, PB_V:PB_V + B_KV_WIDTH]], axis=0).astype(F32)
    for slab in range(B_KV_WIDTH // LANES):
        ks = kn_all[:, slab * LANES:(slab + 1) * LANES]
        vs = v_kv[:, slab * LANES:(slab + 1) * LANES]
        kr = pltpu.roll(ks, B_HEAD_DIM, axis=1)
        vr = pltpu.roll(vs, B_HEAD_DIM, axis=1)
        for par in range(2):
            kv = 2 * slab + par
            kdup = (jnp.where(lo_k, ks, kr) if par == 0 else jnp.where(lo_k, kr, ks)).astype(BF16)
            vdup = (jnp.where(lo_k, vs, vr) if par == 0 else jnp.where(lo_k, vr, vs)).astype(BF16)
            qs, sk = [], []
            for r in range(B_REP):
                h = kv * B_REP + r
                q_slab = qn_all[:, (h // 2) * LANES:(h // 2 + 1) * LANES]
                keep = lo_q if h % 2 == 0 else jnp.logical_not(lo_q)
                qs.append(jnp.where(keep, q_slab, 0.0).astype(BF16))
                sk.append(jnp.full((BLOCK, 1), sink_ref[h], F32))
            q4 = jnp.concatenate(qs, axis=0)
            sink = jnp.concatenate(sk, axis=0)
            s = lax.dot_general(q4, kdup, (((1,), (1,)), ((), ())), preferred_element_type=F32)
            s = jnp.where(valid, s, -jnp.inf)
            mx = jnp.maximum(jnp.max(s, axis=-1, keepdims=True), sink)
            p = jnp.exp(s - mx).astype(BF16)
            denom = jnp.dot(p, ones_k, preferred_element_type=F32) + jnp.exp(sink - mx)
            o = jnp.dot(p, vdup, preferred_element_type=F32) / denom
            for pr in range(B_REP // 2):
                even = o[(2 * pr) * BLOCK:(2 * pr + 1) * BLOCK]
                odd = o[(2 * pr + 1) * BLOCK:(2 * pr + 2) * BLOCK]
                c0 = A_WIDTH + (kv * B_REP // 2 + pr) * LANES
                y_ref[:, c0:c0 + LANES] = jnp.where(lo_q, even, odd).astype(y_ref.dtype)

    qc_all = b_ref[:, PB_QC:PB_QC + C_WIDTH].astype(F32)
    qcn_all = (qc_all * _group_scale(qc_all, s256_ref[...], e256_ref[...], C_HEAD_DIM)
               * (qgc_ref[...] * (C_HEAD_DIM ** -0.5))).astype(BF16)
    for h in range(C_HEADS):
        sl = slice(h * C_HEAD_DIM, (h + 1) * C_HEAD_DIM)
        kh = kvc_ref[:, h * C_HEAD_DIM:(h + 1) * C_HEAD_DIM]
        vh = kvc_ref[:, C_WIDTH + h * C_HEAD_DIM:C_WIDTH + (h + 1) * C_HEAD_DIM]
        s = lax.dot_general(qcn_all[:, sl], kh, (((1,), (1,)), ((), ())),
                            preferred_element_type=F32)
        p = jnp.exp(s - jnp.max(s, axis=-1, keepdims=True)).astype(BF16)
        denom = jnp.dot(p, ones_k, preferred_element_type=F32)
        o = jnp.dot(p, vh, preferred_element_type=F32)
        o = o / jnp.concatenate([denom] * (C_HEAD_DIM // LANES), axis=1)
        y_ref[:, A_WIDTH + B_Q_WIDTH + h * C_HEAD_DIM:
              A_WIDTH + B_Q_WIDTH + (h + 1) * C_HEAD_DIM] = o.astype(y_ref.dtype)


def _mixer_call(pa, pb, kvc, vgain_row, w_spatial, bs_bcast, qgb_row, kgb_row, sinks, qgc_row,
                layer, seq):
    t = pa.shape[0]
    bps = seq // BLOCK
    kvb = PB_K // (2 * B_KV_WIDTH)
    s64, e64 = _group_sums(B_Q_WIDTH, B_HEAD_DIM)
    s128, e128 = _group_sums(A_WIDTH, A_GROUP_DIM)
    s256, e256 = _group_sums(C_WIDTH, C_HEAD_DIM)

    def const(shape):
        return pl.BlockSpec(shape, lambda i: (0,) * len(shape))

    return pl.pallas_call(
        functools.partial(_mixer_body, blocks_per_seq=bps),
        grid=(t // BLOCK,),
        in_specs=[
            pl.BlockSpec((BLOCK, SEC_A_WIDTH), lambda i: (i, 0)),
            pl.BlockSpec((BLOCK, SEC_B_WIDTH), lambda i: (i, 0)),
            pl.BlockSpec((BLOCK, 2 * B_KV_WIDTH), lambda i: (jnp.maximum(i - 1, 0), kvb)),
            pl.BlockSpec((MEM_LEN, 2 * C_WIDTH), lambda i: (i // bps, 0)),
            const((1, A_WIDTH)),
            pl.BlockSpec((None, A_GROUPS, CHUNK, CHUNK), lambda i: (layer, 0, 0, 0)),
            const((A_GROUPS, CHUNK, LANES)),
            const((1, B_Q_WIDTH)),
            const((1, B_KV_WIDTH)),
            pl.BlockSpec(memory_space=pltpu.SMEM),
            const((1, C_WIDTH)),
            const((B_Q_WIDTH, LANES)), const((LANES, B_Q_WIDTH)),
            const((A_WIDTH, LANES)), const((LANES, A_WIDTH)),
            const((C_WIDTH, LANES)), const((LANES, C_WIDTH)),
        ],
        out_specs=pl.BlockSpec((BLOCK, N_BRANCH * BRANCH_WIDTH), lambda i: (i, 0)),
        out_shape=jax.ShapeDtypeStruct((t, N_BRANCH * BRANCH_WIDTH), BF16),
        compiler_params=_params(("arbitrary",)),
        name="mixer",
    )(pa, pb, pb, kvc, vgain_row, w_spatial, bs_bcast, qgb_row, kgb_row, sinks, qgc_row,
      s64, e64, s128, e128, s256, e256)


def _merge_body(y_ref, g0_ref, g1_ref, g2_ref, w_ref, o_ref, wb_ref):
    m = pl.program_id(1)

    @pl.when(m == 0)
    def _():
        wb_ref[...] = w_ref[...].astype(BF16)

    acc = None
    for b, g_ref in enumerate((g0_ref, g1_ref, g2_ref)):
        wide = jnp.dot(y_ref[:, b * BRANCH_WIDTH:(b + 1) * BRANCH_WIDTH], wb_ref[b],
                       preferred_element_type=F32)
        term = g_ref[...].astype(F32) * wide
        acc = term if acc is None else acc + term
    o_ref[...] = acc.astype(o_ref.dtype)


def _merge_call(y, gates, w_branch, layer):
    t = y.shape[0]
    per = D_MODEL // TN_MERGE

    def gate_spec(b):
        return pl.BlockSpec((TM_MERGE, TN_MERGE), lambda n, m: (m, b * per + n))

    return pl.pallas_call(
        _merge_body,
        grid=(D_MODEL // TN_MERGE, t // TM_MERGE),
        in_specs=[pl.BlockSpec((TM_MERGE, N_BRANCH * BRANCH_WIDTH), lambda n, m: (m, 0)),
                  gate_spec(0), gate_spec(1), gate_spec(2),
                  pl.BlockSpec((None, N_BRANCH, BRANCH_WIDTH, TN_MERGE),
                               lambda n, m: (layer, 0, 0, n))],
        out_specs=pl.BlockSpec((TM_MERGE, TN_MERGE), lambda n, m: (m, n)),
        out_shape=jax.ShapeDtypeStruct((t, D_MODEL), BF16),
        scratch_shapes=[pltpu.VMEM((N_BRANCH, BRANCH_WIDTH, TN_MERGE), BF16)],
        compiler_params=_params(("arbitrary", "arbitrary")),
        name="merge",
    )(y, gates, gates, gates, w_branch)


def _outproj_body(mg_ref, x_ref, w_ref, gn_ref, wr_ref, br_ref, x1_ref, lg_ref,
                  wb_ref, wrh_ref, wrl_ref):
    i = pl.program_id(0)

    @pl.when(i == 0)
    def _():
        wb_ref[...] = w_ref[...].astype(BF16)
        wr = wr_ref[...]
        hi = wr.astype(BF16)
        wrh_ref[...] = hi
        wrl_ref[...] = (wr - hi.astype(F32)).astype(BF16)

    x1 = x_ref[...] + jnp.dot(mg_ref[...], wb_ref[...], preferred_element_type=F32)
    x1_ref[...] = x1
    h2 = _rms(x1, gn_ref[...])
    hi = h2.astype(BF16)
    lo = (h2 - hi.astype(F32)).astype(BF16)
    lg = (jnp.dot(hi, wrh_ref[...], preferred_element_type=F32)
          + jnp.dot(lo, wrh_ref[...], preferred_element_type=F32)
          + jnp.dot(hi, wrl_ref[...], preferred_element_type=F32))
    lg_ref[...] = lg + br_ref[...]


def _outproj_call(merged, x, w_out, gn_row, w_router, b_router, layer):
    t = x.shape[0]
    return pl.pallas_call(
        _outproj_body,
        grid=(t // TM_OUT,),
        in_specs=[pl.BlockSpec((TM_OUT, D_MODEL), lambda i: (i, 0)),
                  pl.BlockSpec((TM_OUT, D_MODEL), lambda i: (i, 0)),
                  pl.BlockSpec((None, D_MODEL, D_MODEL), lambda i: (layer, 0, 0),
                               pipeline_mode=pl.Buffered(1)),
                  pl.BlockSpec((1, D_MODEL), lambda i: (0, 0)),
                  pl.BlockSpec((D_MODEL, LANES), lambda i: (0, 0)),
                  pl.BlockSpec((1, LANES), lambda i: (0, 0))],
        out_specs=[pl.BlockSpec((TM_OUT, D_MODEL), lambda i: (i, 0)),
                   pl.BlockSpec((TM_OUT, LANES), lambda i: (i, 0))],
        out_shape=[jax.ShapeDtypeStruct((t, D_MODEL), F32),
                   jax.ShapeDtypeStruct((t, LANES), F32)],
        scratch_shapes=[pltpu.VMEM((D_MODEL, D_MODEL), BF16),
                        pltpu.VMEM((D_MODEL, LANES), BF16),
                        pltpu.VMEM((D_MODEL, LANES), BF16)],
        compiler_params=_params(("arbitrary",)),
        name="outproj",
    )(merged, x, w_out, gn_row, w_router, b_router)


def _route_body(lg_ref, ri_ref, rw_ref, cnt_ref, carry_ref):
    i = pl.program_id(0)

    @pl.when(i == 0)
    def _():
        carry_ref[...] = jnp.zeros_like(carry_ref)

    lg = lg_ref[...]
    tm = lg.shape[0]
    lane = lax.broadcasted_iota(I32, lg.shape, 1)
    neg = -jnp.inf
    big = jnp.int32(LANES)

    is_g = lane < N_GROUPS
    gl = jnp.where(is_g, lg, neg)
    gmax = jnp.max(gl, axis=-1, keepdims=True)
    gidx = jnp.min(jnp.where(gl == gmax, lane, big), axis=-1, keepdims=True)
    p_g = 1.0 / jnp.sum(jnp.where(is_g, jnp.exp(lg - gmax), 0.0), axis=-1, keepdims=True)

    e_lane = lane - N_GROUPS
    in_group = (e_lane >= 0) & (e_lane < N_EXPERTS) & ((e_lane >> 2) == gidx)
    el = jnp.where(in_group, lg, neg)
    v0 = jnp.max(el, axis=-1, keepdims=True)
    i0 = jnp.min(jnp.where(el == v0, lane, big), axis=-1, keepdims=True)
    el1 = jnp.where(lane == i0, neg, el)
    v1 = jnp.max(el1, axis=-1, keepdims=True)
    i1 = jnp.min(jnp.where(el1 == v1, lane, big), axis=-1, keepdims=True)
    e0 = i0 - N_GROUPS
    e1 = i1 - N_GROUPS
    ex = jnp.exp(v1 - v0)
    w0 = p_g / (1.0 + ex)
    w1 = p_g * ex / (1.0 + ex)

    oh0 = lane == e0
    oh1 = lane == e1
    oh = jnp.where(oh0 | oh1, 1.0, 0.0)
    r_i = lax.broadcasted_iota(I32, (tm, tm), 0)
    c_i = lax.broadcasted_iota(I32, (tm, tm), 1)
    lower = jnp.where(c_i < r_i, 1.0, 0.0).astype(BF16)
    before = jnp.dot(lower, oh.astype(BF16), preferred_element_type=F32) + carry_ref[...]
    rank0 = jnp.sum(jnp.where(oh0, before, 0.0), axis=-1, keepdims=True).astype(I32)
    rank1 = jnp.sum(jnp.where(oh1, before, 0.0), axis=-1, keepdims=True).astype(I32)
    carry_ref[...] = carry_ref[...] + jnp.sum(oh, axis=0, keepdims=True)

    zero_i = jnp.zeros_like(lane)
    ri_ref[...] = jnp.where(lane == 0, e0, jnp.where(lane == 1, e1,
                            jnp.where(lane == 2, rank0, jnp.where(lane == 3, rank1, zero_i))))
    rw_ref[...] = jnp.where(lane == 0, w0, jnp.where(lane == 1, w1, 0.0))
    cnt_ref[...] = jnp.broadcast_to(carry_ref[...], cnt_ref.shape).astype(I32)


def _route_call(logits):
    t = logits.shape[0]
    return pl.pallas_call(
        _route_body,
        grid=(t // TM_ROUTE,),
        in_specs=[pl.BlockSpec((TM_ROUTE, LANES), lambda i: (i, 0))],
        out_specs=[pl.BlockSpec((TM_ROUTE, LANES), lambda i: (i, 0)),
                   pl.BlockSpec((TM_ROUTE, LANES), lambda i: (i, 0)),
                   pl.BlockSpec((8, LANES), lambda i: (0, 0))],
        out_shape=[jax.ShapeDtypeStruct((t, LANES), I32),
                   jax.ShapeDtypeStruct((t, LANES), F32),
                   jax.ShapeDtypeStruct((8, LANES), I32)],
        scratch_shapes=[pltpu.VMEM((1, LANES), F32)],
        compiler_params=_params(("arbitrary",)),
        name="route",
    )(logits)


def _plan(route_i, counts, n_tiles):
    c = counts[0, :N_EXPERTS]
    tiles = (c + TM_EXPERT - 1) // TM_EXPERT
    ctiles = jnp.cumsum(tiles)
    start = (ctiles - tiles) * TM_EXPERT
    eid = jnp.arange(N_EXPERTS, dtype=I32)

    def slot(e, r):
        return jnp.sum(jnp.where(e[:, None] == eid[None, :], start[None, :], 0), axis=1) + r

    pos0 = slot(route_i[:, 0], route_i[:, 2]).astype(I32)
    pos1 = slot(route_i[:, 1], route_i[:, 3]).astype(I32)
    n_used = ctiles[-1]
    tid = jnp.minimum(jnp.arange(n_tiles, dtype=I32), n_used - 1)
    tile_expert = jnp.sum(tid[:, None] >= ctiles[None, :], axis=1).astype(I32)
    return pos0, pos1, tile_expert, n_used.reshape(1).astype(I32)


def _invert_body(p0_ref, p1_ref, src_ref):
    i = pl.program_id(0)

    @pl.when(i == 0)
    def _():
        def fill(r, c):
            src_ref[r] = 0
            return c
        lax.fori_loop(0, src_ref.shape[0], fill, 0, unroll=8)

    base = i * TK_INVERT

    def scatter(t, c):
        src_ref[p0_ref[t]] = base + t
        src_ref[p1_ref[t]] = base + t
        return c
    lax.fori_loop(0, TK_INVERT, scatter, 0, unroll=8)


def _invert_call(pos0, pos1, n_rows):
    t = pos0.shape[0]
    smem = pl.BlockSpec((TK_INVERT,), lambda i: (i,), memory_space=pltpu.SMEM)
    return pl.pallas_call(
        _invert_body,
        grid=(t // TK_INVERT,),
        in_specs=[smem, smem],
        out_specs=pl.BlockSpec(memory_space=pltpu.SMEM),
        out_shape=jax.ShapeDtypeStruct((n_rows,), I32),
        compiler_params=_params(("arbitrary",)),
        name="invert",
    )(pos0, pos1)


def _expert_body(te_ref, nu_ref, srcc_ref, srcn_ref, x_hbm, gn_ref, wg_ref, wu_ref, wd_ref,
                 y_ref, xbuf_ref, wgb_ref, wub_ref, wdb_ref, sems):
    i = pl.program_id(0)
    n_used = nu_ref[0]
    slot = lax.rem(i, 2)

    def row_copy(src_ref, r, s):
        return pltpu.make_async_copy(x_hbm.at[pl.ds(src_ref[r], 1)],
                                     xbuf_ref.at[s, pl.ds(r, 1)], sems.at[s])

    def issue(src_ref, s):
        for r in range(TM_EXPERT):
            row_copy(src_ref, r, s).start()

    @pl.when(i == 0)
    def _():
        issue(srcc_ref, 0)

    for s in range(2):
        @pl.when(jnp.logical_and(i + 1 < n_used, slot == 1 - s))
        def _():
            issue(srcn_ref, s)

    @pl.when(i < n_used)
    def _():
        prev = te_ref[jnp.maximum(i - 1, 0)]

        @pl.when(jnp.logical_or(i == 0, te_ref[i] != prev))
        def _():
            wgb_ref[...] = wg_ref[...].astype(BF16)
            wub_ref[...] = wu_ref[...].astype(BF16)
            wdb_ref[...] = wd_ref[...].astype(BF16)

        def drain(r, c):
            row_copy(srcc_ref, r, slot).wait()
            return c
        lax.fori_loop(0, TM_EXPERT, drain, 0, unroll=8)

        h = _rms(xbuf_ref[slot], gn_ref[...]).astype(BF16)
        g = jnp.dot(h, wgb_ref[...], preferred_element_type=F32)
        u = jnp.dot(h, wub_ref[...], preferred_element_type=F32)
        hid = (g * (1.0 / (1.0 + jnp.exp(-g))) * u).astype(BF16)
        y_ref[...] = jnp.dot(hid, wdb_ref[...], preferred_element_type=F32)

    @pl.when(i >= n_used)
    def _():
        y_ref[...] = jnp.zeros_like(y_ref)


def _expert_call(x1, src, tile_expert, n_used, gn_row, w_gate, w_up, w_down, layer):
    n_rows = src.shape[0]
    n_tiles = n_rows // TM_EXPERT

    def w_map(i, te, nu):
        return (layer, te[i], 0, 0)

    grid_spec = pltpu.PrefetchScalarGridSpec(
        num_scalar_prefetch=2,
        grid=(n_tiles,),
        in_specs=[pl.BlockSpec((TM_EXPERT,), lambda i, te, nu: (i,), memory_space=pltpu.SMEM),
                  pl.BlockSpec((TM_EXPERT,), lambda i, te, nu: (jnp.minimum(i + 1, n_tiles - 1),),
                               memory_space=pltpu.SMEM),
                  pl.BlockSpec(memory_space=pl.ANY),
                  pl.BlockSpec((1, D_MODEL), lambda i, te, nu: (0, 0)),
                  pl.BlockSpec((None, None, D_MODEL, D_FF_EXPERT), w_map),
                  pl.BlockSpec((None, None, D_MODEL, D_FF_EXPERT), w_map),
                  pl.BlockSpec((None, None, D_FF_EXPERT, D_MODEL), w_map)],
        out_specs=pl.BlockSpec((TM_EXPERT, D_MODEL), lambda i, te, nu: (i, 0)),
        scratch_shapes=[pltpu.VMEM((2, TM_EXPERT, D_MODEL), F32),
                        pltpu.VMEM((D_MODEL, D_FF_EXPERT), BF16),
                        pltpu.VMEM((D_MODEL, D_FF_EXPERT), BF16),
                        pltpu.VMEM((D_FF_EXPERT, D_MODEL), BF16),
                        pltpu.SemaphoreType.DMA((2,))],
    )
    return pl.pallas_call(
        _expert_body,
        grid_spec=grid_spec,
        out_shape=jax.ShapeDtypeStruct((n_rows, D_MODEL), F32),
        compiler_params=_params(("arbitrary",)),
        name="experts",
    )(tile_expert, n_used, src, src, x1, gn_row, w_gate, w_up, w_down)


def _combine_body(p0c_ref, p1c_ref, p0n_ref, p1n_ref, x1_ref, rw_ref, gn_ref, ys_hbm,
                  x2_ref, h_ref, buf_ref, sems, *, n_steps):
    i = pl.program_id(0)
    slot = lax.rem(i, 2)

    def row_copy(p_ref, k, t, s):
        return pltpu.make_async_copy(ys_hbm.at[pl.ds(p_ref[t], 1)],
                                     buf_ref.at[s, k, pl.ds(t, 1)], sems.at[s])

    def issue(p0_ref, p1_ref, s):
        for t in range(TM_COMBINE):
            row_copy(p0_ref, 0, t, s).start(priority=0)
            row_copy(p1_ref, 1, t, s).start(priority=1)

    @pl.when(i == 0)
    def _():
        issue(p0c_ref, p1c_ref, 0)

    for s in range(2):
        @pl.when(jnp.logical_and(i + 1 < n_steps, slot == 1 - s))
        def _():
            issue(p0n_ref, p1n_ref, s)

    def drain(t, c):
        row_copy(p0c_ref, 0, t, slot).wait()
        row_copy(p1c_ref, 1, t, slot).wait()
        return c
    lax.fori_loop(0, TM_COMBINE, drain, 0, unroll=8)

    rw = rw_ref[...]
    x2 = x1_ref[...] + rw[:, 0:1] * buf_ref[slot, 0] + rw[:, 1:2] * buf_ref[slot, 1]
    x2_ref[...] = x2
    h_ref[...] = _rms(x2, gn_ref[...]).astype(h_ref.dtype)


def _combine_call(x1, route_w, pos0, pos1, ys, gn_row):
    t = x1.shape[0]
    n_steps = t // TM_COMBINE

    def nxt(i):
        return (jnp.minimum(i + 1, n_steps - 1),)

    smem = functools.partial(pl.BlockSpec, (TM_COMBINE,), memory_space=pltpu.SMEM)
    return pl.pallas_call(
        functools.partial(_combine_body, n_steps=n_steps),
        grid=(n_steps,),
        in_specs=[smem(lambda i: (i,)), smem(lambda i: (i,)), smem(nxt), smem(nxt),
                  pl.BlockSpec((TM_COMBINE, D_MODEL), lambda i: (i, 0)),
                  pl.BlockSpec((TM_COMBINE, LANES), lambda i: (i, 0)),
                  pl.BlockSpec((1, D_MODEL), lambda i: (0, 0)),
                  pl.BlockSpec(memory_space=pl.ANY)],
        out_specs=[pl.BlockSpec((TM_COMBINE, D_MODEL), lambda i: (i, 0)),
                   pl.BlockSpec((TM_COMBINE, D_MODEL), lambda i: (i, 0))],
        out_shape=[jax.ShapeDtypeStruct((t, D_MODEL), F32),
                   jax.ShapeDtypeStruct((t, D_MODEL), BF16)],
        scratch_shapes=[pltpu.VMEM((2, TOP_K, TM_COMBINE, D_MODEL), F32),
                        pltpu.SemaphoreType.DMA((2,))],
        compiler_params=_params(("arbitrary",)),
        name="combine",
    )(pos0, pos1, pos0, pos1, x1, route_w, gn_row, ys)


def kernel(x, mem, norm_mix, norm_mem, norm_ffn, w_in, v_gain, w_spatial, b_spatial,
           q_gain_b, k_gain_b, sinks, q_gain_c, k_gain_c, w_mem_kv, w_branch, w_out,
           w_router_group, b_router_group, w_router_expert, b_router_expert,
           w_gate_e, w_up_e, w_down_e):
    b, s, d = x.shape
    t = b * s
    assert d == D_MODEL and s % BLOCK == 0 and mem.shape[1] == MEM_LEN
    assert t % TK_INVERT == 0 and t % TM_PROJ == 0
    n_tiles = (t * TOP_K) // TM_EXPERT + N_EXPERTS
    n_rows = n_tiles * TM_EXPERT

    xf = x.reshape(t, d)
    mem2d = mem.reshape(b * MEM_LEN, d)
    pad = LANES - N_GROUPS - N_EXPERTS

    h = _norm_call(xf, norm_mix[0][None])
    for layer in range(DEPTH):
        kvc = _memkv_call(mem2d, norm_mem[layer][None], w_mem_kv, k_gain_c[layer][None], layer)
        pa = _inproj_call(h, w_in, layer, 0, SEC_A_WIDTH, _gelu, "inproj_a")
        pb = _inproj_call(h, w_in, layer, COL_QB, SEC_B_WIDTH, _identity, "inproj_b")
        gates = _inproj_call(h, w_in, layer, COL_GATE, SEC_G_WIDTH, _sigmoid, "inproj_g")
        bs_bcast = jnp.broadcast_to(b_spatial[layer][:, :, None], (A_GROUPS, CHUNK, LANES))
        y = _mixer_call(pa, pb, kvc, v_gain[layer].reshape(1, A_WIDTH), w_spatial, bs_bcast,
                        jnp.tile(q_gain_b[layer], B_HEADS)[None],
                        jnp.tile(k_gain_b[layer], B_KV_HEADS)[None],
                        sinks[layer], jnp.tile(q_gain_c[layer], C_HEADS)[None], layer, s)
        merged = _merge_call(y, gates, w_branch, layer)
        w_router = jnp.pad(jnp.concatenate([w_router_group[layer], w_router_expert[layer]], axis=1),
                           ((0, 0), (0, pad)))
        b_router = jnp.pad(jnp.concatenate([b_router_group[layer], b_router_expert[layer]]),
                           (0, pad))[None]
        x1, logits = _outproj_call(merged, xf, w_out, norm_ffn[layer][None], w_router, b_router,
                                   layer)
        route_i, route_w, counts = _route_call(logits)
        pos0, pos1, tile_expert, n_used = _plan(route_i, counts, n_tiles)
        src = _invert_call(pos0, pos1, n_rows)
        ys = _expert_call(x1, src, tile_expert, n_used, norm_ffn[layer][None], w_gate_e, w_up_e,
                          w_down_e, layer)
        gn_next = norm_mix[layer + 1] if layer + 1 < DEPTH else norm_mix[layer]
        xf, h = _combine_call(x1, route_w, pos0, pos1, ys, gn_next[None])
    return xf.reshape(b, s, d)
```

```python
import functools

import jax
import jax.numpy as jnp
from jax import lax
from jax.experimental import pallas as pl
from jax.experimental.pallas import tpu as pltpu

F32 = jnp.float32
BF16 = jnp.bfloat16
I32 = jnp.int32

D_MODEL = 2048
DEPTH = 2
MEM_LEN = 256
EPS = 1e-6
BRANCH_WIDTH = 1024
N_BRANCH = 3

CHUNK = 128
A_GROUPS = 8
A_GROUP_DIM = 128
A_WIDTH = A_GROUPS * A_GROUP_DIM

BLOCK = 128
B_HEADS = 16
B_KV_HEADS = 4
B_REP = B_HEADS // B_KV_HEADS
B_HEAD_DIM = 64
B_Q_WIDTH = B_HEADS * B_HEAD_DIM
B_KV_WIDTH = B_KV_HEADS * B_HEAD_DIM

C_HEADS = 4
C_HEAD_DIM = 256
C_WIDTH = C_HEADS * C_HEAD_DIM

COL_U = 0
COL_V = A_WIDTH
COL_QB = 2 * A_WIDTH
COL_KB = COL_QB + B_Q_WIDTH
COL_VB = COL_KB + B_KV_WIDTH
COL_QC = COL_VB + B_KV_WIDTH
COL_GATE = COL_QC + C_WIDTH
IN_WIDTH = COL_GATE + N_BRANCH * D_MODEL
SEC_A_WIDTH = COL_QB
SEC_B_WIDTH = COL_GATE - COL_QB
SEC_G_WIDTH = N_BRANCH * D_MODEL
PB_Q = 0
PB_K = COL_KB - COL_QB
PB_V = COL_VB - COL_QB
PB_QC = COL_QC - COL_QB

N_GROUPS = 4
EXPERTS_PER_GROUP = 4
N_EXPERTS = N_GROUPS * EXPERTS_PER_GROUP
TOP_K = 2
D_FF_EXPERT = 512

LANES = 128
N_CHUNK = D_MODEL // LANES
ROW_PITCH = N_CHUNK + 1
VMEM_LIMIT = 56 * 1024 * 1024
SQRT_HALF = 0.7071067811865476

TM_NORM = 512
TM_PROJ = 1024
TN_PROJ_A, TN_PROJ_B, TN_PROJ_G = 1024, 512, 1536
TM_MERGE, TN_MERGE = 512, 1024
TM_OUT = 512
TM_ROUTE = 512
TK_INVERT = 2048
TM_EXPERT = 256
TM_COMBINE = 256


def _params(sem):
    return pltpu.CompilerParams(dimension_semantics=sem, vmem_limit_bytes=VMEM_LIMIT)


def _rms(x, gain):
    ms = jnp.mean(x * x, axis=-1, keepdims=True)
    return x * lax.rsqrt(ms + EPS) * gain


def _norm_body(x_ref, g_ref, o_ref):
    o_ref[...] = _rms(x_ref[...], g_ref[...]).astype(o_ref.dtype)


def _norm_call(x, gain_row):
    t = x.shape[0]
    return pl.pallas_call(
        _norm_body,
        grid=(t // TM_NORM,),
        in_specs=[pl.BlockSpec((TM_NORM, D_MODEL), lambda i: (i, 0)),
                  pl.BlockSpec((1, D_MODEL), lambda i: (0, 0))],
        out_specs=pl.BlockSpec((TM_NORM, D_MODEL), lambda i: (i, 0)),
        out_shape=jax.ShapeDtypeStruct((t, D_MODEL), BF16),
        compiler_params=_params(("arbitrary",)),
        name="norm",
    )(x, gain_row)


TN_MEMKV = 512


def _memkv_body(mem_ref, gm_ref, w_ref, gk_ref, o_ref):
    n = pl.program_id(0)
    h = _rms(mem_ref[...], gm_ref[...]).astype(BF16)
    acc = jnp.dot(h, w_ref[...].astype(BF16), preferred_element_type=F32)

    @pl.when(n < C_WIDTH // TN_MEMKV)
    def _():
        for j in range(TN_MEMKV // C_HEAD_DIM):
            sl = slice(j * C_HEAD_DIM, (j + 1) * C_HEAD_DIM)
            o_ref[:, sl] = _rms(acc[:, sl], gk_ref[...]).astype(o_ref.dtype)

    @pl.when(n >= C_WIDTH // TN_MEMKV)
    def _():
        o_ref[...] = acc.astype(o_ref.dtype)


def _memkv_call(mem2d, gm_row, w_mem_kv, gk_row, layer):
    rows = mem2d.shape[0]
    return pl.pallas_call(
        _memkv_body,
        grid=(2 * C_WIDTH // TN_MEMKV,),
        in_specs=[pl.BlockSpec((rows, D_MODEL), lambda n: (0, 0)),
                  pl.BlockSpec((1, D_MODEL), lambda n: (0, 0)),
                  pl.BlockSpec((None, D_MODEL, TN_MEMKV), lambda n: (layer, 0, n)),
                  pl.BlockSpec((1, C_HEAD_DIM), lambda n: (0, 0))],
        out_specs=pl.BlockSpec((rows, TN_MEMKV), lambda n: (0, n)),
        out_shape=jax.ShapeDtypeStruct((rows, 2 * C_WIDTH), BF16),
        compiler_params=_params(("arbitrary",)),
        name="memkv",
    )(mem2d, gm_row, w_mem_kv, gk_row)


def _gelu(x):
    return 0.5 * x * (1.0 + lax.erf(x * SQRT_HALF))


def _sigmoid(x):
    return 1.0 / (1.0 + jnp.exp(-x))


def _identity(x):
    return x


def _inproj_body(h_ref, w_ref, o_ref, wb_ref, *, act):
    @pl.when(pl.program_id(1) == 0)
    def _():
        wb_ref[...] = w_ref[...].astype(BF16)

    acc = jnp.dot(h_ref[...], wb_ref[...], preferred_element_type=F32)
    o_ref[...] = act(acc).astype(o_ref.dtype)


def _inproj_call(h, w_in, layer, col0, width, tn, act, name):
    t = h.shape[0]
    assert col0 % tn == 0 and width % tn == 0
    n0 = col0 // tn
    return pl.pallas_call(
        functools.partial(_inproj_body, act=act),
        grid=(width // tn, t // TM_PROJ),
        in_specs=[pl.BlockSpec((TM_PROJ, D_MODEL), lambda n, m: (m, 0)),
                  pl.BlockSpec((None, D_MODEL, tn), lambda n, m: (layer, 0, n0 + n))],
        out_specs=pl.BlockSpec((TM_PROJ, tn), lambda n, m: (m, n)),
        out_shape=jax.ShapeDtypeStruct((t, width), BF16),
        scratch_shapes=[pltpu.VMEM((D_MODEL, tn), BF16)],
        compiler_params=_params(("arbitrary", "arbitrary")),
        name=name,
    )(h, w_in)


def _group_sums(width, group):
    gid = jnp.arange(width, dtype=I32) // group
    s = (gid[:, None] == jnp.arange(LANES, dtype=I32)[None, :]).astype(BF16)
    return s, s.T


def _group_scale(x, s, e, group):
    ssq = jnp.dot((x * x).astype(BF16), s, preferred_element_type=F32)
    r = lax.rsqrt(ssq * (1.0 / group) + EPS)
    r_hi = r.astype(BF16)
    r_lo = (r - r_hi.astype(F32)).astype(BF16)
    return (jnp.dot(r_hi, e, preferred_element_type=F32)
            + jnp.dot(r_lo, e, preferred_element_type=F32))


def _mixer_body(a_ref, b_ref, prev_ref, kvc_ref, vgain_ref, ws_ref, bs_ref, qgb_ref, kgb_ref,
                sink_ref, qgc_ref, s64_ref, e64_ref, s128_ref, e128_ref, s256_ref, e256_ref,
                y_ref, *, blocks_per_seq):
    i = pl.program_id(0)
    n = lax.rem(i, blocks_per_seq)
    ones_k = jnp.ones((2 * BLOCK, LANES), BF16)

    row = lax.broadcasted_iota(I32, (CHUNK, CHUNK), 0)
    col = lax.broadcasted_iota(I32, (CHUNK, CHUNK), 1)
    causal = col <= row
    v_all = a_ref[:, COL_V:COL_V + A_WIDTH].astype(F32)
    vn_all = (v_all * _group_scale(v_all, s128_ref[...], e128_ref[...], A_GROUP_DIM)
              * vgain_ref[...]).astype(BF16)
    for g in range(A_GROUPS):
        sl = slice(g * A_GROUP_DIM, (g + 1) * A_GROUP_DIM)
        u = a_ref[:, COL_U + g * A_GROUP_DIM:COL_U + (g + 1) * A_GROUP_DIM].astype(F32)
        w = jnp.where(causal, ws_ref[g], 0.0).astype(BF16)
        mixed = jnp.dot(w, vn_all[:, sl], preferred_element_type=F32) + bs_ref[g]
        y_ref[:, sl] = (u * mixed).astype(y_ref.dtype)

    lane_k = lax.broadcasted_iota(I32, (2 * BLOCK, LANES), 1)
    lane_q = lax.broadcasted_iota(I32, (BLOCK, LANES), 1)
    lo_k = lane_k < B_HEAD_DIM
    lo_q = lane_q < B_HEAD_DIM
    qi = lax.broadcasted_iota(I32, (B_REP * BLOCK, 2 * BLOCK), 0) & (BLOCK - 1)
    kj = lax.broadcasted_iota(I32, (B_REP * BLOCK, 2 * BLOCK), 1)
    rel = BLOCK + qi - kj
    valid = (rel >= 0) & (rel < BLOCK) & ((n > 0) | (kj >= BLOCK))

    q_all = b_ref[:, PB_Q:PB_Q + B_Q_WIDTH].astype(F32)
    qn_all = (q_all * _group_scale(q_all, s64_ref[...], e64_ref[...], B_HEAD_DIM)
              * (qgb_ref[...] * (B_HEAD_DIM ** -0.5)))
    k_all = jnp.concatenate([prev_ref[:, 0:B_KV_WIDTH], b_ref[:, PB_K:PB_K + B_KV_WIDTH]],
                            axis=0).astype(F32)
    kn_all = (k_all * _group_scale(k_all, s64_ref[0:B_KV_WIDTH, :], e64_ref[:, 0:B_KV_WIDTH],
                                   B_HEAD_DIM) * kgb_ref[...])
    v_kv = jnp.concatenate([prev_ref[:, B_KV_WIDTH:2 * B_KV_WIDTH],
                            b_ref[:, PB_V:PB_V + B_KV_WIDTH]], axis=0).astype(F32)
    for slab in range(B_KV_WIDTH // LANES):
        ks = kn_all[:, slab * LANES:(slab + 1) * LANES]
        vs = v_kv[:, slab * LANES:(slab + 1) * LANES]
        kr = pltpu.roll(ks, B_HEAD_DIM, axis=1)
        vr = pltpu.roll(vs, B_HEAD_DIM, axis=1)
        for par in range(2):
            kv = 2 * slab + par
            kdup = (jnp.where(lo_k, ks, kr) if par == 0 else jnp.where(lo_k, kr, ks)).astype(BF16)
            vdup = (jnp.where(lo_k, vs, vr) if par == 0 else jnp.where(lo_k, vr, vs)).astype(BF16)
            qs, sk = [], []
            for r in range(B_REP):
                h = kv * B_REP + r
                q_slab = qn_all[:, (h // 2) * LANES:(h // 2 + 1) * LANES]
                keep = lo_q if h % 2 == 0 else jnp.logical_not(lo_q)
                qs.append(jnp.where(keep, q_slab, 0.0).astype(BF16))
                sk.append(jnp.full((BLOCK, 1), sink_ref[h], F32))
            q4 = jnp.concatenate(qs, axis=0)
            sink = jnp.concatenate(sk, axis=0)
            s = lax.dot_general(q4, kdup, (((1,), (1,)), ((), ())), preferred_element_type=F32)
            s = jnp.where(valid, s, -jnp.inf)
            mx = jnp.maximum(jnp.max(s, axis=-1, keepdims=True), sink)
            p = jnp.exp(s - mx).astype(BF16)
            denom = jnp.dot(p, ones_k, preferred_element_type=F32) + jnp.exp(sink - mx)
            o = jnp.dot(p, vdup, preferred_element_type=F32) / denom
            for pr in range(B_REP // 2):
                even = o[(2 * pr) * BLOCK:(2 * pr + 1) * BLOCK]
                odd = o[(2 * pr + 1) * BLOCK:(2 * pr + 2) * BLOCK]
                c0 = A_WIDTH + (kv * B_REP // 2 + pr) * LANES
                y_ref[:, c0:c0 + LANES] = jnp.where(lo_q, even, odd).astype(y_ref.dtype)

    qc_all = b_ref[:, PB_QC:PB_QC + C_WIDTH].astype(F32)
    qcn_all = (qc_all * _group_scale(qc_all, s256_ref[...], e256_ref[...], C_HEAD_DIM)
               * (qgc_ref[...] * (C_HEAD_DIM ** -0.5))).astype(BF16)
    for h in range(C_HEADS):
        sl = slice(h * C_HEAD_DIM, (h + 1) * C_HEAD_DIM)
        kh = kvc_ref[:, h * C_HEAD_DIM:(h + 1) * C_HEAD_DIM]
        vh = kvc_ref[:, C_WIDTH + h * C_HEAD_DIM:C_WIDTH + (h + 1) * C_HEAD_DIM]
        s = lax.dot_general(qcn_all[:, sl], kh, (((1,), (1,)), ((), ())),
                            preferred_element_type=F32)
        p = jnp.exp(s - jnp.max(s, axis=-1, keepdims=True)).astype(BF16)
        denom = jnp.dot(p, ones_k, preferred_element_type=F32)
        o = jnp.dot(p, vh, preferred_element_type=F32)
        o = o / jnp.concatenate([denom] * (C_HEAD_DIM // LANES), axis=1)
        y_ref[:, A_WIDTH + B_Q_WIDTH + h * C_HEAD_DIM:
              A_WIDTH + B_Q_WIDTH + (h + 1) * C_HEAD_DIM] = o.astype(y_ref.dtype)


def _mixer_call(pa, pb, kvc, vgain_row, w_spatial, bs_bcast, qgb_row, kgb_row, sinks, qgc_row,
                layer, seq):
    t = pa.shape[0]
    bps = seq // BLOCK
    kvb = PB_K // (2 * B_KV_WIDTH)
    s64, e64 = _group_sums(B_Q_WIDTH, B_HEAD_DIM)
    s128, e128 = _group_sums(A_WIDTH, A_GROUP_DIM)
    s256, e256 = _group_sums(C_WIDTH, C_HEAD_DIM)

    def const(shape):
        return pl.BlockSpec(shape, lambda i: (0,) * len(shape))

    return pl.pallas_call(
        functools.partial(_mixer_body, blocks_per_seq=bps),
        grid=(t // BLOCK,),
        in_specs=[
            pl.BlockSpec((BLOCK, SEC_A_WIDTH), lambda i: (i, 0)),
            pl.BlockSpec((BLOCK, SEC_B_WIDTH), lambda i: (i, 0)),
            pl.BlockSpec((BLOCK, 2 * B_KV_WIDTH), lambda i: (jnp.maximum(i - 1, 0), kvb)),
            pl.BlockSpec((MEM_LEN, 2 * C_WIDTH), lambda i: (i // bps, 0)),
            const((1, A_WIDTH)),
            pl.BlockSpec((None, A_GROUPS, CHUNK, CHUNK), lambda i: (layer, 0, 0, 0)),
            const((A_GROUPS, CHUNK, LANES)),
            const((1, B_Q_WIDTH)),
            const((1, B_KV_WIDTH)),
            pl.BlockSpec(memory_space=pltpu.SMEM),
            const((1, C_WIDTH)),
            const((B_Q_WIDTH, LANES)), const((LANES, B_Q_WIDTH)),
            const((A_WIDTH, LANES)), const((LANES, A_WIDTH)),
            const((C_WIDTH, LANES)), const((LANES, C_WIDTH)),
        ],
        out_specs=pl.BlockSpec((BLOCK, N_BRANCH * BRANCH_WIDTH), lambda i: (i, 0)),
        out_shape=jax.ShapeDtypeStruct((t, N_BRANCH * BRANCH_WIDTH), BF16),
        compiler_params=_params(("arbitrary",)),
        name="mixer",
    )(pa, pb, pb, kvc, vgain_row, w_spatial, bs_bcast, qgb_row, kgb_row, sinks, qgc_row,
      s64, e64, s128, e128, s256, e256)


def _merge_body(y_ref, g0_ref, g1_ref, g2_ref, w_ref, o_ref, wb_ref):
    m = pl.program_id(1)

    @pl.when(m == 0)
    def _():
        wb_ref[...] = w_ref[...].astype(BF16)

    acc = None
    for b, g_ref in enumerate((g0_ref, g1_ref, g2_ref)):
        wide = jnp.dot(y_ref[:, b * BRANCH_WIDTH:(b + 1) * BRANCH_WIDTH], wb_ref[b],
                       preferred_element_type=F32)
        term = g_ref[...].astype(F32) * wide
        acc = term if acc is None else acc + term
    o_ref[...] = acc.astype(o_ref.dtype)


def _merge_call(y, gates, w_branch, layer):
    t = y.shape[0]
    per = D_MODEL // TN_MERGE

    def gate_spec(b):
        return pl.BlockSpec((TM_MERGE, TN_MERGE), lambda n, m: (m, b * per + n))

    return pl.pallas_call(
        _merge_body,
        grid=(D_MODEL // TN_MERGE, t // TM_MERGE),
        in_specs=[pl.BlockSpec((TM_MERGE, N_BRANCH * BRANCH_WIDTH), lambda n, m: (m, 0)),
                  gate_spec(0), gate_spec(1), gate_spec(2),
                  pl.BlockSpec((None, N_BRANCH, BRANCH_WIDTH, TN_MERGE),
                               lambda n, m: (layer, 0, 0, n))],
        out_specs=pl.BlockSpec((TM_MERGE, TN_MERGE), lambda n, m: (m, n)),
        out_shape=jax.ShapeDtypeStruct((t, D_MODEL), BF16),
        scratch_shapes=[pltpu.VMEM((N_BRANCH, BRANCH_WIDTH, TN_MERGE), BF16)],
        compiler_params=_params(("arbitrary", "arbitrary")),
        name="merge",
    )(y, gates, gates, gates, w_branch)


def _outproj_body(mg_ref, x_ref, w_ref, gn_ref, wr_ref, br_ref, x1_ref, h2_ref, lg_ref,
                  wrh_ref, wrl_ref):
    i = pl.program_id(0)

    @pl.when(i == 0)
    def _():
        wr = wr_ref[...]
        hi = wr.astype(BF16)
        wrh_ref[...] = hi
        wrl_ref[...] = (wr - hi.astype(F32)).astype(BF16)

    x1 = x_ref[...] + jnp.dot(mg_ref[...], w_ref[...], preferred_element_type=F32)
    x1_ref[...] = x1
    h2 = _rms(x1, gn_ref[...])
    for c in range(N_CHUNK):
        h2_ref[:, c, :] = h2[:, c * LANES:(c + 1) * LANES]
    hi = h2.astype(BF16)
    lo = (h2 - hi.astype(F32)).astype(BF16)
    lg = (jnp.dot(hi, wrh_ref[...], preferred_element_type=F32)
          + jnp.dot(lo, wrh_ref[...], preferred_element_type=F32)
          + jnp.dot(hi, wrl_ref[...], preferred_element_type=F32))
    lg_ref[...] = lg + br_ref[...]


def _outproj_call(merged, x, w_out_bf16, gn_row, w_router, b_router):
    t = x.shape[0]
    return pl.pallas_call(
        _outproj_body,
        grid=(t // TM_OUT,),
        in_specs=[pl.BlockSpec((TM_OUT, D_MODEL), lambda i: (i, 0)),
                  pl.BlockSpec((TM_OUT, D_MODEL), lambda i: (i, 0)),
                  pl.BlockSpec((D_MODEL, D_MODEL), lambda i: (0, 0),
                               pipeline_mode=pl.Buffered(1)),
                  pl.BlockSpec((1, D_MODEL), lambda i: (0, 0)),
                  pl.BlockSpec((D_MODEL, LANES), lambda i: (0, 0)),
                  pl.BlockSpec((1, LANES), lambda i: (0, 0))],
        out_specs=[pl.BlockSpec((TM_OUT, D_MODEL), lambda i: (i, 0)),
                   pl.BlockSpec((TM_OUT, N_CHUNK, LANES), lambda i: (i, 0, 0)),
                   pl.BlockSpec((TM_OUT, LANES), lambda i: (i, 0))],
        out_shape=[jax.ShapeDtypeStruct((t, D_MODEL), F32),
                   jax.ShapeDtypeStruct((t, N_CHUNK, LANES), F32),
                   jax.ShapeDtypeStruct((t, LANES), F32)],
        scratch_shapes=[pltpu.VMEM((D_MODEL, LANES), BF16),
                        pltpu.VMEM((D_MODEL, LANES), BF16)],
        compiler_params=_params(("arbitrary",)),
        name="outproj",
    )(merged, x, w_out_bf16, gn_row, w_router, b_router)


def _route_body(lg_ref, ri_ref, rw_ref, cnt_ref, carry_ref):
    i = pl.program_id(0)

    @pl.when(i == 0)
    def _():
        carry_ref[...] = jnp.zeros_like(carry_ref)

    lg = lg_ref[...]
    tm = lg.shape[0]
    lane = lax.broadcasted_iota(I32, lg.shape, 1)
    neg = -jnp.inf
    big = jnp.int32(LANES)

    is_g = lane < N_GROUPS
    gl = jnp.where(is_g, lg, neg)
    gmax = jnp.max(gl, axis=-1, keepdims=True)
    gidx = jnp.min(jnp.where(gl == gmax, lane, big), axis=-1, keepdims=True)
    p_g = 1.0 / jnp.sum(jnp.where(is_g, jnp.exp(lg - gmax), 0.0), axis=-1, keepdims=True)

    e_lane = lane - N_GROUPS
    in_group = (e_lane >= 0) & (e_lane < N_EXPERTS) & ((e_lane >> 2) == gidx)
    el = jnp.where(in_group, lg, neg)
    v0 = jnp.max(el, axis=-1, keepdims=True)
    i0 = jnp.min(jnp.where(el == v0, lane, big), axis=-1, keepdims=True)
    el1 = jnp.where(lane == i0, neg, el)
    v1 = jnp.max(el1, axis=-1, keepdims=True)
    i1 = jnp.min(jnp.where(el1 == v1, lane, big), axis=-1, keepdims=True)
    e0 = i0 - N_GROUPS
    e1 = i1 - N_GROUPS
    ex = jnp.exp(v1 - v0)
    w0 = p_g / (1.0 + ex)
    w1 = p_g * ex / (1.0 + ex)

    oh0 = lane == e0
    oh1 = lane == e1
    oh = jnp.where(oh0 | oh1, 1.0, 0.0)
    r_i = lax.broadcasted_iota(I32, (tm, tm), 0)
    c_i = lax.broadcasted_iota(I32, (tm, tm), 1)
    lower = jnp.where(c_i < r_i, 1.0, 0.0).astype(BF16)
    before = jnp.dot(lower, oh.astype(BF16), preferred_element_type=F32) + carry_ref[...]
    rank0 = jnp.sum(jnp.where(oh0, before, 0.0), axis=-1, keepdims=True).astype(I32)
    rank1 = jnp.sum(jnp.where(oh1, before, 0.0), axis=-1, keepdims=True).astype(I32)
    carry_ref[...] = carry_ref[...] + jnp.sum(oh, axis=0, keepdims=True)

    zero_i = jnp.zeros_like(lane)
    ri_ref[...] = jnp.where(lane == 0, e0, jnp.where(lane == 1, e1,
                            jnp.where(lane == 2, rank0, jnp.where(lane == 3, rank1, zero_i))))
    rw_ref[...] = jnp.where(lane == 0, w0, jnp.where(lane == 1, w1, 0.0))
    cnt_ref[...] = jnp.broadcast_to(carry_ref[...], cnt_ref.shape).astype(I32)


def _route_call(logits):
    t = logits.shape[0]
    return pl.pallas_call(
        _route_body,
        grid=(t // TM_ROUTE,),
        in_specs=[pl.BlockSpec((TM_ROUTE, LANES), lambda i: (i, 0))],
        out_specs=[pl.BlockSpec((TM_ROUTE, LANES), lambda i: (i, 0)),
                   pl.BlockSpec((TM_ROUTE, LANES), lambda i: (i, 0)),
                   pl.BlockSpec((8, LANES), lambda i: (0, 0))],
        out_shape=[jax.ShapeDtypeStruct((t, LANES), I32),
                   jax.ShapeDtypeStruct((t, LANES), F32),
                   jax.ShapeDtypeStruct((8, LANES), I32)],
        scratch_shapes=[pltpu.VMEM((1, LANES), F32)],
        compiler_params=_params(("arbitrary",)),
        name="route",
    )(logits)


def _plan(route_i, counts, n_tiles):
    c = counts[0, :N_EXPERTS]
    tiles = (c + TM_EXPERT - 1) // TM_EXPERT
    ctiles = jnp.cumsum(tiles)
    start = (ctiles - tiles) * TM_EXPERT
    eid = jnp.arange(N_EXPERTS, dtype=I32)

    def slot(e, r):
        return jnp.sum(jnp.where(e[:, None] == eid[None, :], start[None, :], 0), axis=1) + r

    pos0 = slot(route_i[:, 0], route_i[:, 2]).astype(I32)
    pos1 = slot(route_i[:, 1], route_i[:, 3]).astype(I32)
    n_used = ctiles[-1]
    tid = jnp.minimum(jnp.arange(n_tiles, dtype=I32), n_used - 1)
    tile_expert = jnp.sum(tid[:, None] >= ctiles[None, :], axis=1).astype(I32)
    return pos0, pos1, tile_expert, n_used.reshape(1).astype(I32)


def _invert_body(p0_ref, p1_ref, src_ref):
    i = pl.program_id(0)

    @pl.when(i == 0)
    def _():
        def fill(r, c):
            src_ref[r] = 0
            return c
        lax.fori_loop(0, src_ref.shape[0], fill, 0, unroll=8)

    base = i * TK_INVERT

    def scatter(t, c):
        src_ref[p0_ref[t]] = base + t
        src_ref[p1_ref[t]] = base + t
        return c
    lax.fori_loop(0, TK_INVERT, scatter, 0, unroll=8)


def _invert_call(pos0, pos1, n_rows):
    t = pos0.shape[0]
    smem = pl.BlockSpec((TK_INVERT,), lambda i: (i,), memory_space=pltpu.SMEM)
    return pl.pallas_call(
        _invert_body,
        grid=(t // TK_INVERT,),
        in_specs=[smem, smem],
        out_specs=pl.BlockSpec(memory_space=pltpu.SMEM),
        out_shape=jax.ShapeDtypeStruct((n_rows,), I32),
        compiler_params=_params(("arbitrary",)),
        name="invert",
    )(pos0, pos1)


def _expert_body(te_ref, nu_ref, srcc_ref, srcn_ref, h_hbm, wg_ref, wu_ref, wd_ref,
                 y_ref, xbuf_ref, wgb_ref, wub_ref, wdb_ref, sems):
    i = pl.program_id(0)
    n_used = nu_ref[0]
    slot = lax.rem(i, 2)

    def row_copy(src_ref, r, s):
        return pltpu.make_async_copy(h_hbm.at[src_ref[r]],
                                     xbuf_ref.at[s, pl.ds(ROW_PITCH * r, N_CHUNK), :],
                                     sems.at[s])

    def issue(src_ref, s):
        for r in range(TM_EXPERT):
            row_copy(src_ref, r, s).start()

    @pl.when(i == 0)
    def _():
        issue(srcc_ref, 0)

    for s in range(2):
        @pl.when(jnp.logical_and(i + 1 < n_used, slot == 1 - s))
        def _():
            issue(srcn_ref, s)

    @pl.when(i < n_used)
    def _():
        prev = te_ref[jnp.maximum(i - 1, 0)]

        @pl.when(jnp.logical_or(i == 0, te_ref[i] != prev))
        def _():
            wgb_ref[...] = wg_ref[...].astype(BF16)
            wub_ref[...] = wu_ref[...].astype(BF16)
            wdb_ref[...] = wd_ref[...].astype(BF16)

        def drain(r, c):
            row_copy(srcc_ref, r, slot).wait()
            return c
        lax.fori_loop(0, TM_EXPERT, drain, 0, unroll=8)

        h = jnp.concatenate(
            [xbuf_ref.at[slot][pl.ds(c, TM_EXPERT, stride=ROW_PITCH), :].astype(BF16)
             for c in range(N_CHUNK)], axis=1)
        g = jnp.dot(h, wgb_ref[...], preferred_element_type=F32)
        u = jnp.dot(h, wub_ref[...], preferred_element_type=F32)
        hid = (g * _sigmoid(g) * u).astype(BF16)
        y = jnp.dot(hid, wdb_ref[...], preferred_element_type=F32)
        for c in range(N_CHUNK):
            y_ref[:, c, :] = y[:, c * LANES:(c + 1) * LANES]

    @pl.when(i >= n_used)
    def _():
        y_ref[...] = jnp.zeros_like(y_ref)


def _expert_call(h2, src, tile_expert, n_used, w_gate, w_up, w_down, layer):
    n_rows = src.shape[0]
    n_tiles = n_rows // TM_EXPERT

    def w_map(i, te, nu):
        return (layer, te[i], 0, 0)

    grid_spec = pltpu.PrefetchScalarGridSpec(
        num_scalar_prefetch=2,
        grid=(n_tiles,),
        in_specs=[pl.BlockSpec((TM_EXPERT,), lambda i, te, nu: (i,), memory_space=pltpu.SMEM),
                  pl.BlockSpec((TM_EXPERT,), lambda i, te, nu: (jnp.minimum(i + 1, n_tiles - 1),),
                               memory_space=pltpu.SMEM),
                  pl.BlockSpec(memory_space=pl.ANY),
                  pl.BlockSpec((None, None, D_MODEL, D_FF_EXPERT), w_map),
                  pl.BlockSpec((None, None, D_MODEL, D_FF_EXPERT), w_map),
                  pl.BlockSpec((None, None, D_FF_EXPERT, D_MODEL), w_map)],
        out_specs=pl.BlockSpec((TM_EXPERT, N_CHUNK, LANES), lambda i, te, nu: (i, 0, 0)),
        scratch_shapes=[pltpu.VMEM((2, TM_EXPERT * ROW_PITCH, LANES), F32),
                        pltpu.VMEM((D_MODEL, D_FF_EXPERT), BF16),
                        pltpu.VMEM((D_MODEL, D_FF_EXPERT), BF16),
                        pltpu.VMEM((D_FF_EXPERT, D_MODEL), BF16),
                        pltpu.SemaphoreType.DMA((2,))],
    )
    return pl.pallas_call(
        _expert_body,
        grid_spec=grid_spec,
        out_shape=jax.ShapeDtypeStruct((n_rows, N_CHUNK, LANES), F32),
        compiler_params=_params(("arbitrary",)),
        name="experts",
    )(tile_expert, n_used, src, src, h2, w_gate, w_up, w_down)


def _combine_body(p0c_ref, p1c_ref, p0n_ref, p1n_ref, x1_ref, rw_ref, gn_ref, ys_hbm,
                  x2_ref, h_ref, buf_ref, sems, *, n_steps):
    i = pl.program_id(0)
    slot = lax.rem(i, 2)

    def row_copy(p_ref, k, t, s):
        return pltpu.make_async_copy(ys_hbm.at[p_ref[t]],
                                     buf_ref.at[s, k, pl.ds(ROW_PITCH * t, N_CHUNK), :],
                                     sems.at[s])

    def issue(p0_ref, p1_ref, s):
        for t in range(TM_COMBINE):
            row_copy(p0_ref, 0, t, s).start(priority=0)
            row_copy(p1_ref, 1, t, s).start(priority=1)

    @pl.when(i == 0)
    def _():
        issue(p0c_ref, p1c_ref, 0)

    for s in range(2):
        @pl.when(jnp.logical_and(i + 1 < n_steps, slot == 1 - s))
        def _():
            issue(p0n_ref, p1n_ref, s)

    def drain(t, c):
        row_copy(p0c_ref, 0, t, slot).wait()
        row_copy(p1c_ref, 1, t, slot).wait()
        return c
    lax.fori_loop(0, TM_COMBINE, drain, 0, unroll=8)

    def rows(k):
        return jnp.concatenate(
            [buf_ref.at[slot, k][pl.ds(c, TM_COMBINE, stride=ROW_PITCH), :] for c in range(N_CHUNK)],
            axis=1)

    rw = rw_ref[...]
    x2 = x1_ref[...] + rw[:, 0:1] * rows(0) + rw[:, 1:2] * rows(1)
    x2_ref[...] = x2
    h_ref[...] = _rms(x2, gn_ref[...]).astype(h_ref.dtype)


def _combine_call(x1, route_w, pos0, pos1, ys, gn_row):
    t = x1.shape[0]
    n_steps = t // TM_COMBINE

    def nxt(i):
        return (jnp.minimum(i + 1, n_steps - 1),)

    smem = functools.partial(pl.BlockSpec, (TM_COMBINE,), memory_space=pltpu.SMEM)
    return pl.pallas_call(
        functools.partial(_combine_body, n_steps=n_steps),
        grid=(n_steps,),
        in_specs=[smem(lambda i: (i,)), smem(lambda i: (i,)), smem(nxt), smem(nxt),
                  pl.BlockSpec((TM_COMBINE, D_MODEL), lambda i: (i, 0)),
                  pl.BlockSpec((TM_COMBINE, LANES), lambda i: (i, 0)),
                  pl.BlockSpec((1, D_MODEL), lambda i: (0, 0)),
                  pl.BlockSpec(memory_space=pl.ANY)],
        out_specs=[pl.BlockSpec((TM_COMBINE, D_MODEL), lambda i: (i, 0)),
                   pl.BlockSpec((TM_COMBINE, D_MODEL), lambda i: (i, 0))],
        out_shape=[jax.ShapeDtypeStruct((t, D_MODEL), F32),
                   jax.ShapeDtypeStruct((t, D_MODEL), BF16)],
        scratch_shapes=[pltpu.VMEM((2, TOP_K, TM_COMBINE * ROW_PITCH, LANES), F32),
                        pltpu.SemaphoreType.DMA((2,))],
        compiler_params=_params(("arbitrary",)),
        name="combine",
    )(pos0, pos1, pos0, pos1, x1, route_w, gn_row, ys)


def kernel(x, mem, norm_mix, norm_mem, norm_ffn, w_in, v_gain, w_spatial, b_spatial,
           q_gain_b, k_gain_b, sinks, q_gain_c, k_gain_c, w_mem_kv, w_branch, w_out,
           w_router_group, b_router_group, w_router_expert, b_router_expert,
           w_gate_e, w_up_e, w_down_e):
    b, s, d = x.shape
    t = b * s
    assert d == D_MODEL and s % BLOCK == 0 and mem.shape[1] == MEM_LEN
    assert t % TK_INVERT == 0 and t % TM_PROJ == 0
    n_tiles = (t * TOP_K) // TM_EXPERT + N_EXPERTS
    n_rows = n_tiles * TM_EXPERT

    xf = x.reshape(t, d)
    mem2d = mem.reshape(b * MEM_LEN, d)
    pad = LANES - N_GROUPS - N_EXPERTS

    h = _norm_call(xf, norm_mix[0][None])
    for layer in range(DEPTH):
        kvc = _memkv_call(mem2d, norm_mem[layer][None], w_mem_kv, k_gain_c[layer][None], layer)
        pa = _inproj_call(h, w_in, layer, 0, SEC_A_WIDTH, TN_PROJ_A, _gelu, "inproj_a")
        pb = _inproj_call(h, w_in, layer, COL_QB, SEC_B_WIDTH, TN_PROJ_B, _identity, "inproj_b")
        gates = _inproj_call(h, w_in, layer, COL_GATE, SEC_G_WIDTH, TN_PROJ_G, _sigmoid, "inproj_g")
        bs_bcast = jnp.broadcast_to(b_spatial[layer][:, :, None], (A_GROUPS, CHUNK, LANES))
        y = _mixer_call(pa, pb, kvc, v_gain[layer].reshape(1, A_WIDTH), w_spatial, bs_bcast,
                        jnp.tile(q_gain_b[layer], B_HEADS)[None],
                        jnp.tile(k_gain_b[layer], B_KV_HEADS)[None],
                        sinks[layer], jnp.tile(q_gain_c[layer], C_HEADS)[None], layer, s)
        merged = _merge_call(y, gates, w_branch, layer)
        w_router = jnp.pad(jnp.concatenate([w_router_group[layer], w_router_expert[layer]], axis=1),
                           ((0, 0), (0, pad)))
        b_router = jnp.pad(jnp.concatenate([b_router_group[layer], b_router_expert[layer]]),
                           (0, pad))[None]
        x1, h2, logits = _outproj_call(merged, xf, w_out[layer].astype(BF16),
                                       norm_ffn[layer][None], w_router, b_router)
        route_i, route_w, counts = _route_call(logits)
        pos0, pos1, tile_expert, n_used = _plan(route_i, counts, n_tiles)
        src = _invert_call(pos0, pos1, n_rows)
        ys = _expert_call(h2, src, tile_expert, n_used, w_gate_e, w_up_e, w_down_e, layer)
        gn_next = norm_mix[layer + 1] if layer + 1 < DEPTH else norm_mix[layer]
        xf, h = _combine_call(x1, route_w, pos0, pos1, ys, gn_next[None])
    return xf.reshape(b, s, d)
```

```python
import functools

import jax
import jax.numpy as jnp
from jax import lax
from jax.experimental import pallas as pl
from jax.experimental.pallas import tpu as pltpu

F32 = jnp.float32
BF16 = jnp.bfloat16
I32 = jnp.int32

D_MODEL = 2048
DEPTH = 2
MEM_LEN = 256
EPS = 1e-6
BRANCH_WIDTH = 1024
N_BRANCH = 3

CHUNK = 128
A_GROUPS = 8
A_GROUP_DIM = 128
A_WIDTH = A_GROUPS * A_GROUP_DIM

BLOCK = 128
B_HEADS = 16
B_KV_HEADS = 4
B_REP = B_HEADS // B_KV_HEADS
B_HEAD_DIM = 64
B_Q_WIDTH = B_HEADS * B_HEAD_DIM
B_KV_WIDTH = B_KV_HEADS * B_HEAD_DIM

C_HEADS = 4
C_HEAD_DIM = 256
C_WIDTH = C_HEADS * C_HEAD_DIM

COL_U = 0
COL_V = A_WIDTH
COL_QB = 2 * A_WIDTH
COL_KB = COL_QB + B_Q_WIDTH
COL_VB = COL_KB + B_KV_WIDTH
COL_QC = COL_VB + B_KV_WIDTH
COL_GATE = COL_QC + C_WIDTH
IN_WIDTH = COL_GATE + N_BRANCH * D_MODEL
SEC_A_WIDTH = COL_QB
SEC_B_WIDTH = COL_GATE - COL_QB
SEC_G_WIDTH = N_BRANCH * D_MODEL
PB_Q = 0
PB_K = COL_KB - COL_QB
PB_V = COL_VB - COL_QB
PB_QC = COL_QC - COL_QB

N_GROUPS = 4
EXPERTS_PER_GROUP = 4
N_EXPERTS = N_GROUPS * EXPERTS_PER_GROUP
TOP_K = 2
D_FF_EXPERT = 512

LANES = 128
VMEM_LIMIT = 56 * 1024 * 1024
SQRT_HALF = 0.7071067811865476

TM_NORM = 512
TM_PROJ = 1024
TN_PROJ_A, TN_PROJ_B, TN_PROJ_G = 1024, 512, 1536
TM_MERGE, TN_MERGE = 512, 1024
TM_OUT = 512
TM_ROUTE = 512
TK_INVERT = 2048
TM_EXPERT = 512
TM_COMBINE = 512


def _params(sem):
    return pltpu.CompilerParams(dimension_semantics=sem, vmem_limit_bytes=VMEM_LIMIT)


def _rms(x, gain):
    ms = jnp.mean(x * x, axis=-1, keepdims=True)
    return x * lax.rsqrt(ms + EPS) * gain


def _norm_body(x_ref, g_ref, o_ref):
    o_ref[...] = _rms(x_ref[...], g_ref[...]).astype(o_ref.dtype)


def _norm_call(x, gain_row):
    t = x.shape[0]
    return pl.pallas_call(
        _norm_body,
        grid=(t // TM_NORM,),
        in_specs=[pl.BlockSpec((TM_NORM, D_MODEL), lambda i: (i, 0)),
                  pl.BlockSpec((1, D_MODEL), lambda i: (0, 0))],
        out_specs=pl.BlockSpec((TM_NORM, D_MODEL), lambda i: (i, 0)),
        out_shape=jax.ShapeDtypeStruct((t, D_MODEL), BF16),
        compiler_params=_params(("arbitrary",)),
        name="norm",
    )(x, gain_row)


TN_MEMKV = 512


def _memkv_body(mem_ref, gm_ref, w_ref, gk_ref, o_ref):
    n = pl.program_id(0)
    h = _rms(mem_ref[...], gm_ref[...]).astype(BF16)
    acc = jnp.dot(h, w_ref[...].astype(BF16), preferred_element_type=F32)

    @pl.when(n < C_WIDTH // TN_MEMKV)
    def _():
        for j in range(TN_MEMKV // C_HEAD_DIM):
            sl = slice(j * C_HEAD_DIM, (j + 1) * C_HEAD_DIM)
            o_ref[:, sl] = _rms(acc[:, sl], gk_ref[...]).astype(o_ref.dtype)

    @pl.when(n >= C_WIDTH // TN_MEMKV)
    def _():
        o_ref[...] = acc.astype(o_ref.dtype)


def _memkv_call(mem2d, gm_row, w_mem_kv, gk_row, layer):
    rows = mem2d.shape[0]
    return pl.pallas_call(
        _memkv_body,
        grid=(2 * C_WIDTH // TN_MEMKV,),
        in_specs=[pl.BlockSpec((rows, D_MODEL), lambda n: (0, 0)),
                  pl.BlockSpec((1, D_MODEL), lambda n: (0, 0)),
                  pl.BlockSpec((None, D_MODEL, TN_MEMKV), lambda n: (layer, 0, n)),
                  pl.BlockSpec((1, C_HEAD_DIM), lambda n: (0, 0))],
        out_specs=pl.BlockSpec((rows, TN_MEMKV), lambda n: (0, n)),
        out_shape=jax.ShapeDtypeStruct((rows, 2 * C_WIDTH), BF16),
        compiler_params=_params(("arbitrary",)),
        name="memkv",
    )(mem2d, gm_row, w_mem_kv, gk_row)


def _gelu(x):
    return 0.5 * x * (1.0 + lax.erf(x * SQRT_HALF))


def _sigmoid(x):
    return 1.0 / (1.0 + jnp.exp(-x))


def _identity(x):
    return x


def _inproj_body(h_ref, w_ref, o_ref, wb_ref, *, act):
    @pl.when(pl.program_id(1) == 0)
    def _():
        wb_ref[...] = w_ref[...].astype(BF16)

    acc = jnp.dot(h_ref[...], wb_ref[...], preferred_element_type=F32)
    o_ref[...] = act(acc).astype(o_ref.dtype)


def _inproj_call(h, w_in, layer, col0, width, tn, act, name):
    t = h.shape[0]
    assert col0 % tn == 0 and width % tn == 0
    n0 = col0 // tn
    return pl.pallas_call(
        functools.partial(_inproj_body, act=act),
        grid=(width // tn, t // TM_PROJ),
        in_specs=[pl.BlockSpec((TM_PROJ, D_MODEL), lambda n, m: (m, 0)),
                  pl.BlockSpec((None, D_MODEL, tn), lambda n, m: (layer, 0, n0 + n))],
        out_specs=pl.BlockSpec((TM_PROJ, tn), lambda n, m: (m, n)),
        out_shape=jax.ShapeDtypeStruct((t, width), BF16),
        scratch_shapes=[pltpu.VMEM((D_MODEL, tn), BF16)],
        compiler_params=_params(("arbitrary", "arbitrary")),
        name=name,
    )(h, w_in)


def _group_sums(width, group):
    gid = jnp.arange(width, dtype=I32) // group
    s = (gid[:, None] == jnp.arange(LANES, dtype=I32)[None, :]).astype(BF16)
    return s, s.T


def _group_ssq(x, s):
    return jnp.dot((x * x).astype(BF16), s, preferred_element_type=F32)


def _group_bcast(ssq, e, group):
    r = lax.rsqrt(ssq * (1.0 / group) + EPS)
    r_hi = r.astype(BF16)
    r_lo = (r - r_hi.astype(F32)).astype(BF16)
    return (jnp.dot(r_hi, e, preferred_element_type=F32)
            + jnp.dot(r_lo, e, preferred_element_type=F32))


def _mixer_body(a_ref, b_ref, prev_ref, kvc_ref, vgain_ref, ws_ref, bs_ref, qgb_ref, kgb_ref,
                sink_ref, qgc_ref, s64_ref, e64_ref, s128_ref, e128_ref, s256_ref, e256_ref,
                y_ref, *, blocks_per_seq):
    i = pl.program_id(0)
    n = lax.rem(i, blocks_per_seq)
    ones_k = jnp.ones((2 * BLOCK, LANES), BF16)
    row = lax.broadcasted_iota(I32, (CHUNK, CHUNK), 0)
    col = lax.broadcasted_iota(I32, (CHUNK, CHUNK), 1)
    causal = col <= row
    lane_k = lax.broadcasted_iota(I32, (2 * BLOCK, LANES), 1)
    lane_q = lax.broadcasted_iota(I32, (BLOCK, LANES), 1)
    lo_k = lane_k < B_HEAD_DIM
    lo_q = lane_q < B_HEAD_DIM
    qi = lax.broadcasted_iota(I32, (B_REP * BLOCK, 2 * BLOCK), 0) & (BLOCK - 1)
    kj = lax.broadcasted_iota(I32, (B_REP * BLOCK, 2 * BLOCK), 1)
    rel = BLOCK + qi - kj
    valid = (rel >= 0) & (rel < BLOCK) & ((n > 0) | (kj >= BLOCK))

    v_all = a_ref[:, COL_V:COL_V + A_WIDTH].astype(F32)
    q_all = b_ref[:, PB_Q:PB_Q + B_Q_WIDTH].astype(F32)
    k_all = jnp.concatenate([prev_ref[:, 0:B_KV_WIDTH], b_ref[:, PB_K:PB_K + B_KV_WIDTH]],
                            axis=0).astype(F32)
    qc_all = b_ref[:, PB_QC:PB_QC + C_WIDTH].astype(F32)
    ssq_q = _group_ssq(q_all, s64_ref[...])
    ssq_k = _group_ssq(k_all, s64_ref[0:B_KV_WIDTH, :])
    ssq_c = _group_ssq(qc_all, s256_ref[...])
    ssq_v = _group_ssq(v_all, s128_ref[...])
    qn_all = (q_all * _group_bcast(ssq_q, e64_ref[...], B_HEAD_DIM)
              * (qgb_ref[...] * (B_HEAD_DIM ** -0.5)))
    kn_all = k_all * _group_bcast(ssq_k, e64_ref[:, 0:B_KV_WIDTH], B_HEAD_DIM) * kgb_ref[...]
    qcn_all = (qc_all * _group_bcast(ssq_c, e256_ref[...], C_HEAD_DIM)
               * (qgc_ref[...] * (C_HEAD_DIM ** -0.5))).astype(BF16)
    vn_all = (v_all * _group_bcast(ssq_v, e128_ref[...], A_GROUP_DIM)
              * vgain_ref[...]).astype(BF16)
    v_kv = jnp.concatenate([prev_ref[:, B_KV_WIDTH:2 * B_KV_WIDTH],
                            b_ref[:, PB_V:PB_V + B_KV_WIDTH]], axis=0).astype(F32)

    scores_c = [
        lax.dot_general(qcn_all[:, h * C_HEAD_DIM:(h + 1) * C_HEAD_DIM],
                        kvc_ref[:, h * C_HEAD_DIM:(h + 1) * C_HEAD_DIM],
                        (((1,), (1,)), ((), ())), preferred_element_type=F32)
        for h in range(C_HEADS)]
    vdups, sinks, scores_b = [], [], []
    for slab in range(B_KV_WIDTH // LANES):
        ks = kn_all[:, slab * LANES:(slab + 1) * LANES]
        vs = v_kv[:, slab * LANES:(slab + 1) * LANES]
        kr = pltpu.roll(ks, B_HEAD_DIM, axis=1)
        vr = pltpu.roll(vs, B_HEAD_DIM, axis=1)
        for par in range(2):
            kv = 2 * slab + par
            kdup = (jnp.where(lo_k, ks, kr) if par == 0 else jnp.where(lo_k, kr, ks)).astype(BF16)
            vdups.append((jnp.where(lo_k, vs, vr) if par == 0
                          else jnp.where(lo_k, vr, vs)).astype(BF16))
            qs, sk = [], []
            for r in range(B_REP):
                h = kv * B_REP + r
                q_slab = qn_all[:, (h // 2) * LANES:(h // 2 + 1) * LANES]
                keep = lo_q if h % 2 == 0 else jnp.logical_not(lo_q)
                qs.append(jnp.where(keep, q_slab, 0.0).astype(BF16))
                sk.append(jnp.full((BLOCK, 1), sink_ref[h], F32))
            sinks.append(jnp.concatenate(sk, axis=0))
            scores_b.append(lax.dot_general(jnp.concatenate(qs, axis=0), kdup,
                                            (((1,), (1,)), ((), ())),
                                            preferred_element_type=F32))

    probs_c = [jnp.exp(s - jnp.max(s, axis=-1, keepdims=True)).astype(BF16) for s in scores_c]
    mixed_a = [
        jnp.dot(jnp.where(causal, ws_ref[g], 0.0).astype(BF16),
                vn_all[:, g * A_GROUP_DIM:(g + 1) * A_GROUP_DIM],
                preferred_element_type=F32) + bs_ref[g]
        for g in range(A_GROUPS)]
    probs_b, sink_terms = [], []
    for s, sink in zip(scores_b, sinks):
        s = jnp.where(valid, s, -jnp.inf)
        mx = jnp.maximum(jnp.max(s, axis=-1, keepdims=True), sink)
        probs_b.append(jnp.exp(s - mx).astype(BF16))
        sink_terms.append(jnp.exp(sink - mx))

    outs_c = [(jnp.dot(p, kvc_ref[:, C_WIDTH + h * C_HEAD_DIM:C_WIDTH + (h + 1) * C_HEAD_DIM],
                       preferred_element_type=F32),
               jnp.dot(p, ones_k, preferred_element_type=F32))
              for h, p in enumerate(probs_c)]
    for g, mixed in enumerate(mixed_a):
        sl = slice(g * A_GROUP_DIM, (g + 1) * A_GROUP_DIM)
        u = a_ref[:, COL_U + g * A_GROUP_DIM:COL_U + (g + 1) * A_GROUP_DIM].astype(F32)
        y_ref[:, sl] = (u * mixed).astype(y_ref.dtype)
    outs_b = [(jnp.dot(p, vdup, preferred_element_type=F32),
               jnp.dot(p, ones_k, preferred_element_type=F32) + st)
              for p, vdup, st in zip(probs_b, vdups, sink_terms)]

    for h, (o, denom) in enumerate(outs_c):
        o = o / jnp.concatenate([denom] * (C_HEAD_DIM // LANES), axis=1)
        y_ref[:, A_WIDTH + B_Q_WIDTH + h * C_HEAD_DIM:
              A_WIDTH + B_Q_WIDTH + (h + 1) * C_HEAD_DIM] = o.astype(y_ref.dtype)
    for kv, (o, denom) in enumerate(outs_b):
        o = o / denom
        for pr in range(B_REP // 2):
            even = o[(2 * pr) * BLOCK:(2 * pr + 1) * BLOCK]
            odd = o[(2 * pr + 1) * BLOCK:(2 * pr + 2) * BLOCK]
            c0 = A_WIDTH + (kv * B_REP // 2 + pr) * LANES
            y_ref[:, c0:c0 + LANES] = jnp.where(lo_q, even, odd).astype(y_ref.dtype)


def _mixer_call(pa, pb, kvc, vgain_row, w_spatial, bs_bcast, qgb_row, kgb_row, sinks, qgc_row,
                layer, seq):
    t = pa.shape[0]
    bps = seq // BLOCK
    kvb = PB_K // (2 * B_KV_WIDTH)
    s64, e64 = _group_sums(B_Q_WIDTH, B_HEAD_DIM)
    s128, e128 = _group_sums(A_WIDTH, A_GROUP_DIM)
    s256, e256 = _group_sums(C_WIDTH, C_HEAD_DIM)

    def const(shape):
        return pl.BlockSpec(shape, lambda i: (0,) * len(shape))

    return pl.pallas_call(
        functools.partial(_mixer_body, blocks_per_seq=bps),
        grid=(t // BLOCK,),
        in_specs=[
            pl.BlockSpec((BLOCK, SEC_A_WIDTH), lambda i: (i, 0)),
            pl.BlockSpec((BLOCK, SEC_B_WIDTH), lambda i: (i, 0)),
            pl.BlockSpec((BLOCK, 2 * B_KV_WIDTH), lambda i: (jnp.maximum(i - 1, 0), kvb)),
            pl.BlockSpec((MEM_LEN, 2 * C_WIDTH), lambda i: (i // bps, 0)),
            const((1, A_WIDTH)),
            pl.BlockSpec((None, A_GROUPS, CHUNK, CHUNK), lambda i: (layer, 0, 0, 0)),
            const((A_GROUPS, CHUNK, LANES)),
            const((1, B_Q_WIDTH)),
            const((1, B_KV_WIDTH)),
            pl.BlockSpec(memory_space=pltpu.SMEM),
            const((1, C_WIDTH)),
            const((B_Q_WIDTH, LANES)), const((LANES, B_Q_WIDTH)),
            const((A_WIDTH, LANES)), const((LANES, A_WIDTH)),
            const((C_WIDTH, LANES)), const((LANES, C_WIDTH)),
        ],
        out_specs=pl.BlockSpec((BLOCK, N_BRANCH * BRANCH_WIDTH), lambda i: (i, 0)),
        out_shape=jax.ShapeDtypeStruct((t, N_BRANCH * BRANCH_WIDTH), BF16),
        compiler_params=_params(("arbitrary",)),
        name="mixer",
    )(pa, pb, pb, kvc, vgain_row, w_spatial, bs_bcast, qgb_row, kgb_row, sinks, qgc_row,
      s64, e64, s128, e128, s256, e256)


def _merge_body(y_ref, g0_ref, g1_ref, g2_ref, w_ref, o_ref, wb_ref):
    m = pl.program_id(1)

    @pl.when(m == 0)
    def _():
        wb_ref[...] = w_ref[...].astype(BF16)

    acc = None
    for b, g_ref in enumerate((g0_ref, g1_ref, g2_ref)):
        wide = jnp.dot(y_ref[:, b * BRANCH_WIDTH:(b + 1) * BRANCH_WIDTH], wb_ref[b],
                       preferred_element_type=F32)
        term = g_ref[...].astype(F32) * wide
        acc = term if acc is None else acc + term
    o_ref[...] = acc.astype(o_ref.dtype)


def _merge_call(y, gates, w_branch, layer):
    t = y.shape[0]
    per = D_MODEL // TN_MERGE

    def gate_spec(b):
        return pl.BlockSpec((TM_MERGE, TN_MERGE), lambda n, m: (m, b * per + n))

    return pl.pallas_call(
        _merge_body,
        grid=(D_MODEL // TN_MERGE, t // TM_MERGE),
        in_specs=[pl.BlockSpec((TM_MERGE, N_BRANCH * BRANCH_WIDTH), lambda n, m: (m, 0)),
                  gate_spec(0), gate_spec(1), gate_spec(2),
                  pl.BlockSpec((None, N_BRANCH, BRANCH_WIDTH, TN_MERGE),
                               lambda n, m: (layer, 0, 0, n))],
        out_specs=pl.BlockSpec((TM_MERGE, TN_MERGE), lambda n, m: (m, n)),
        out_shape=jax.ShapeDtypeStruct((t, D_MODEL), BF16),
        scratch_shapes=[pltpu.VMEM((N_BRANCH, BRANCH_WIDTH, TN_MERGE), BF16)],
        compiler_params=_params(("arbitrary", "arbitrary")),
        name="merge",
    )(y, gates, gates, gates, w_branch)


def _outproj_body(mg_ref, x_ref, w_ref, gn_ref, wr_ref, br_ref, x1_ref, lg_ref,
                  wrh_ref, wrl_ref):
    i = pl.program_id(0)

    @pl.when(i == 0)
    def _():
        wr = wr_ref[...]
        hi = wr.astype(BF16)
        wrh_ref[...] = hi
        wrl_ref[...] = (wr - hi.astype(F32)).astype(BF16)

    x1 = x_ref[...] + jnp.dot(mg_ref[...], w_ref[...], preferred_element_type=F32)
    x1_ref[...] = x1
    h2 = _rms(x1, gn_ref[...])
    hi = h2.astype(BF16)
    lo = (h2 - hi.astype(F32)).astype(BF16)
    lg = (jnp.dot(hi, wrh_ref[...], preferred_element_type=F32)
          + jnp.dot(lo, wrh_ref[...], preferred_element_type=F32)
          + jnp.dot(hi, wrl_ref[...], preferred_element_type=F32))
    lg_ref[...] = lg + br_ref[...]


def _outproj_call(merged, x, w_out_bf16, gn_row, w_router, b_router):
    t = x.shape[0]
    return pl.pallas_call(
        _outproj_body,
        grid=(t // TM_OUT,),
        in_specs=[pl.BlockSpec((TM_OUT, D_MODEL), lambda i: (i, 0)),
                  pl.BlockSpec((TM_OUT, D_MODEL), lambda i: (i, 0)),
                  pl.BlockSpec((D_MODEL, D_MODEL), lambda i: (0, 0),
                               pipeline_mode=pl.Buffered(1)),
                  pl.BlockSpec((1, D_MODEL), lambda i: (0, 0)),
                  pl.BlockSpec((D_MODEL, LANES), lambda i: (0, 0)),
                  pl.BlockSpec((1, LANES), lambda i: (0, 0))],
        out_specs=[pl.BlockSpec((TM_OUT, D_MODEL), lambda i: (i, 0)),
                   pl.BlockSpec((TM_OUT, LANES), lambda i: (i, 0))],
        out_shape=[jax.ShapeDtypeStruct((t, D_MODEL), F32),
                   jax.ShapeDtypeStruct((t, LANES), F32)],
        scratch_shapes=[pltpu.VMEM((D_MODEL, LANES), BF16),
                        pltpu.VMEM((D_MODEL, LANES), BF16)],
        compiler_params=_params(("arbitrary",)),
        name="outproj",
    )(merged, x, w_out_bf16, gn_row, w_router, b_router)


def _route_body(lg_ref, ri_ref, rw_ref, cnt_ref, carry_ref):
    i = pl.program_id(0)

    @pl.when(i == 0)
    def _():
        carry_ref[...] = jnp.zeros_like(carry_ref)

    lg = lg_ref[...]
    tm = lg.shape[0]
    lane = lax.broadcasted_iota(I32, lg.shape, 1)
    neg = -jnp.inf
    big = jnp.int32(LANES)

    is_g = lane < N_GROUPS
    gl = jnp.where(is_g, lg, neg)
    gmax = jnp.max(gl, axis=-1, keepdims=True)
    gidx = jnp.min(jnp.where(gl == gmax, lane, big), axis=-1, keepdims=True)
    p_g = 1.0 / jnp.sum(jnp.where(is_g, jnp.exp(lg - gmax), 0.0), axis=-1, keepdims=True)

    e_lane = lane - N_GROUPS
    in_group = (e_lane >= 0) & (e_lane < N_EXPERTS) & ((e_lane >> 2) == gidx)
    el = jnp.where(in_group, lg, neg)
    v0 = jnp.max(el, axis=-1, keepdims=True)
    i0 = jnp.min(jnp.where(el == v0, lane, big), axis=-1, keepdims=True)
    el1 = jnp.where(lane == i0, neg, el)
    v1 = jnp.max(el1, axis=-1, keepdims=True)
    i1 = jnp.min(jnp.where(el1 == v1, lane, big), axis=-1, keepdims=True)
    e0 = i0 - N_GROUPS
    e1 = i1 - N_GROUPS
    ex = jnp.exp(v1 - v0)
    w0 = p_g / (1.0 + ex)
    w1 = p_g * ex / (1.0 + ex)

    oh0 = lane == e0
    oh1 = lane == e1
    oh = jnp.where(oh0 | oh1, 1.0, 0.0)
    r_i = lax.broadcasted_iota(I32, (tm, tm), 0)
    c_i = lax.broadcasted_iota(I32, (tm, tm), 1)
    lower = jnp.where(c_i < r_i, 1.0, 0.0).astype(BF16)
    before = jnp.dot(lower, oh.astype(BF16), preferred_element_type=F32) + carry_ref[...]
    rank0 = jnp.sum(jnp.where(oh0, before, 0.0), axis=-1, keepdims=True).astype(I32)
    rank1 = jnp.sum(jnp.where(oh1, before, 0.0), axis=-1, keepdims=True).astype(I32)
    carry_ref[...] = carry_ref[...] + jnp.sum(oh, axis=0, keepdims=True)

    zero_i = jnp.zeros_like(lane)
    ri_ref[...] = jnp.where(lane == 0, e0, jnp.where(lane == 1, e1,
                            jnp.where(lane == 2, rank0, jnp.where(lane == 3, rank1, zero_i))))
    rw_ref[...] = jnp.where(lane == 0, w0, jnp.where(lane == 1, w1, 0.0))
    cnt_ref[...] = jnp.broadcast_to(carry_ref[...], cnt_ref.shape).astype(I32)


def _route_call(logits):
    t = logits.shape[0]
    return pl.pallas_call(
        _route_body,
        grid=(t // TM_ROUTE,),
        in_specs=[pl.BlockSpec((TM_ROUTE, LANES), lambda i: (i, 0))],
        out_specs=[pl.BlockSpec((TM_ROUTE, LANES), lambda i: (i, 0)),
                   pl.BlockSpec((TM_ROUTE, LANES), lambda i: (i, 0)),
                   pl.BlockSpec((8, LANES), lambda i: (0, 0))],
        out_shape=[jax.ShapeDtypeStruct((t, LANES), I32),
                   jax.ShapeDtypeStruct((t, LANES), F32),
                   jax.ShapeDtypeStruct((8, LANES), I32)],
        scratch_shapes=[pltpu.VMEM((1, LANES), F32)],
        compiler_params=_params(("arbitrary",)),
        name="route",
    )(logits)


def _plan(route_i, counts, n_tiles):
    c = counts[0, :N_EXPERTS]
    tiles = (c + TM_EXPERT - 1) // TM_EXPERT
    ctiles = jnp.cumsum(tiles)
    start = (ctiles - tiles) * TM_EXPERT
    eid = jnp.arange(N_EXPERTS, dtype=I32)

    def slot(e, r):
        return jnp.sum(jnp.where(e[:, None] == eid[None, :], start[None, :], 0), axis=1) + r

    pos0 = slot(route_i[:, 0], route_i[:, 2]).astype(I32)
    pos1 = slot(route_i[:, 1], route_i[:, 3]).astype(I32)
    n_used = ctiles[-1]
    tid = jnp.minimum(jnp.arange(n_tiles, dtype=I32), n_used - 1)
    tile_expert = jnp.sum(tid[:, None] >= ctiles[None, :], axis=1).astype(I32)
    return pos0, pos1, tile_expert, n_used.reshape(1).astype(I32)


def _invert_body(p0_ref, p1_ref, src_ref):
    i = pl.program_id(0)

    @pl.when(i == 0)
    def _():
        def fill(r, c):
            src_ref[r] = 0
            return c
        lax.fori_loop(0, src_ref.shape[0], fill, 0, unroll=8)

    base = i * TK_INVERT

    def scatter(t, c):
        src_ref[p0_ref[t]] = base + t
        src_ref[p1_ref[t]] = base + t
        return c
    lax.fori_loop(0, TK_INVERT, scatter, 0, unroll=8)


def _invert_call(pos0, pos1, n_rows):
    t = pos0.shape[0]
    smem = pl.BlockSpec((TK_INVERT,), lambda i: (i,), memory_space=pltpu.SMEM)
    return pl.pallas_call(
        _invert_body,
        grid=(t // TK_INVERT,),
        in_specs=[smem, smem],
        out_specs=pl.BlockSpec(memory_space=pltpu.SMEM),
        out_shape=jax.ShapeDtypeStruct((n_rows,), I32),
        compiler_params=_params(("arbitrary",)),
        name="invert",
    )(pos0, pos1)


def _expert_body(te_ref, nu_ref, srcc_ref, srcn_ref, x_hbm, gn_ref, wg_ref, wu_ref, wd_ref,
                 y_ref, xbuf_ref, wgb_ref, wub_ref, wdb_ref, sems):
    i = pl.program_id(0)
    n_used = nu_ref[0]
    slot = lax.rem(i, 2)

    def row_copy(src_ref, r, s):
        return pltpu.make_async_copy(x_hbm.at[pl.ds(src_ref[r], 1)],
                                     xbuf_ref.at[s, pl.ds(r, 1)], sems.at[s])

    def issue(src_ref, s):
        for r in range(TM_EXPERT):
            row_copy(src_ref, r, s).start()

    @pl.when(i == 0)
    def _():
        issue(srcc_ref, 0)

    for s in range(2):
        @pl.when(jnp.logical_and(i + 1 < n_used, slot == 1 - s))
        def _():
            issue(srcn_ref, s)

    @pl.when(i < n_used)
    def _():
        prev = te_ref[jnp.maximum(i - 1, 0)]

        @pl.when(jnp.logical_or(i == 0, te_ref[i] != prev))
        def _():
            wgb_ref[...] = wg_ref[...].astype(BF16)
            wub_ref[...] = wu_ref[...].astype(BF16)
            wdb_ref[...] = wd_ref[...].astype(BF16)

        def drain(r, c):
            row_copy(srcc_ref, r, slot).wait()
            return c
        lax.fori_loop(0, TM_EXPERT, drain, 0, unroll=8)

        h = _rms(xbuf_ref[slot], gn_ref[...]).astype(BF16)
        g = jnp.dot(h, wgb_ref[...], preferred_element_type=F32)
        u = jnp.dot(h, wub_ref[...], preferred_element_type=F32)
        hid = (g * _sigmoid(g) * u).astype(BF16)
        y_ref[...] = jnp.dot(hid, wdb_ref[...], preferred_element_type=F32)

    @pl.when(i >= n_used)
    def _():
        y_ref[...] = jnp.zeros_like(y_ref)


def _expert_call(x1, src, tile_expert, n_used, gn_row, w_gate, w_up, w_down, layer):
    n_rows = src.shape[0]
    n_tiles = n_rows // TM_EXPERT

    def w_map(i, te, nu):
        return (layer, te[i], 0, 0)

    grid_spec = pltpu.PrefetchScalarGridSpec(
        num_scalar_prefetch=2,
        grid=(n_tiles,),
        in_specs=[pl.BlockSpec((TM_EXPERT,), lambda i, te, nu: (i,), memory_space=pltpu.SMEM),
                  pl.BlockSpec((TM_EXPERT,), lambda i, te, nu: (jnp.minimum(i + 1, n_tiles - 1),),
                               memory_space=pltpu.SMEM),
                  pl.BlockSpec(memory_space=pl.ANY),
                  pl.BlockSpec((1, D_MODEL), lambda i, te, nu: (0, 0)),
                  pl.BlockSpec((None, None, D_MODEL, D_FF_EXPERT), w_map),
                  pl.BlockSpec((None, None, D_MODEL, D_FF_EXPERT), w_map),
                  pl.BlockSpec((None, None, D_FF_EXPERT, D_MODEL), w_map)],
        out_specs=pl.BlockSpec((TM_EXPERT, D_MODEL), lambda i, te, nu: (i, 0)),
        scratch_shapes=[pltpu.VMEM((2, TM_EXPERT, D_MODEL), F32),
                        pltpu.VMEM((D_MODEL, D_FF_EXPERT), BF16),
                        pltpu.VMEM((D_MODEL, D_FF_EXPERT), BF16),
                        pltpu.VMEM((D_FF_EXPERT, D_MODEL), BF16),
                        pltpu.SemaphoreType.DMA((2,))],
    )
    return pl.pallas_call(
        _expert_body,
        grid_spec=grid_spec,
        out_shape=jax.ShapeDtypeStruct((n_rows, D_MODEL), F32),
        compiler_params=_params(("arbitrary",)),
        name="experts",
    )(tile_expert, n_used, src, src, x1, gn_row, w_gate, w_up, w_down)


def _combine_body(p0c_ref, p1c_ref, p0n_ref, p1n_ref, x1_ref, rw_ref, gn_ref, ys_hbm,
                  x2_ref, h_ref, buf_ref, sems, *, n_steps):
    i = pl.program_id(0)
    slot = lax.rem(i, 2)

    def row_copy(p_ref, k, t, s):
        return pltpu.make_async_copy(ys_hbm.at[pl.ds(p_ref[t], 1)],
                                     buf_ref.at[s, k, pl.ds(t, 1)], sems.at[s])

    def issue(p0_ref, p1_ref, s):
        for t in range(TM_COMBINE):
            row_copy(p0_ref, 0, t, s).start(priority=0)
            row_copy(p1_ref, 1, t, s).start(priority=1)

    @pl.when(i == 0)
    def _():
        issue(p0c_ref, p1c_ref, 0)

    for s in range(2):
        @pl.when(jnp.logical_and(i + 1 < n_steps, slot == 1 - s))
        def _():
            issue(p0n_ref, p1n_ref, s)

    def drain(t, c):
        row_copy(p0c_ref, 0, t, slot).wait()
        row_copy(p1c_ref, 1, t, slot).wait()
        return c
    lax.fori_loop(0, TM_COMBINE, drain, 0, unroll=8)

    rw = rw_ref[...]
    x2 = x1_ref[...] + rw[:, 0:1] * buf_ref[slot, 0] + rw[:, 1:2] * buf_ref[slot, 1]
    x2_ref[...] = x2
    h_ref[...] = _rms(x2, gn_ref[...]).astype(h_ref.dtype)


def _combine_call(x1, route_w, pos0, pos1, ys, gn_row):
    t = x1.shape[0]
    n_steps = t // TM_COMBINE

    def nxt(i):
        return (jnp.minimum(i + 1, n_steps - 1),)

    smem = functools.partial(pl.BlockSpec, (TM_COMBINE,), memory_space=pltpu.SMEM)
    return pl.pallas_call(
        functools.partial(_combine_body, n_steps=n_steps),
        grid=(n_steps,),
        in_specs=[smem(lambda i: (i,)), smem(lambda i: (i,)), smem(nxt), smem(nxt),
                  pl.BlockSpec((TM_COMBINE, D_MODEL), lambda i: (i, 0)),
                  pl.BlockSpec((TM_COMBINE, LANES), lambda i: (i, 0)),
                  pl.BlockSpec((1, D_MODEL), lambda i: (0, 0)),
                  pl.BlockSpec(memory_space=pl.ANY)],
        out_specs=[pl.BlockSpec((TM_COMBINE, D_MODEL), lambda i: (i, 0)),
                   pl.BlockSpec((TM_COMBINE, D_MODEL), lambda i: (i, 0))],
        out_shape=[jax.ShapeDtypeStruct((t, D_MODEL), F32),
                   jax.ShapeDtypeStruct((t, D_MODEL), BF16)],
        scratch_shapes=[pltpu.VMEM((2, TOP_K, TM_COMBINE, D_MODEL), F32),
                        pltpu.SemaphoreType.DMA((2,))],
        compiler_params=_params(("arbitrary",)),
        name="combine",
    )(pos0, pos1, pos0, pos1, x1, route_w, gn_row, ys)


def kernel(x, mem, norm_mix, norm_mem, norm_ffn, w_in, v_gain, w_spatial, b_spatial,
           q_gain_b, k_gain_b, sinks, q_gain_c, k_gain_c, w_mem_kv, w_branch, w_out,
           w_router_group, b_router_group, w_router_expert, b_router_expert,
           w_gate_e, w_up_e, w_down_e):
    b, s, d = x.shape
    t = b * s
    assert d == D_MODEL and s % BLOCK == 0 and mem.shape[1] == MEM_LEN
    assert t % TK_INVERT == 0 and t % TM_PROJ == 0
    n_tiles = (t * TOP_K) // TM_EXPERT + N_EXPERTS
    n_rows = n_tiles * TM_EXPERT

    xf = x.reshape(t, d)
    mem2d = mem.reshape(b * MEM_LEN, d)
    pad = LANES - N_GROUPS - N_EXPERTS

    h = _norm_call(xf, norm_mix[0][None])
    for layer in range(DEPTH):
        kvc = _memkv_call(mem2d, norm_mem[layer][None], w_mem_kv, k_gain_c[layer][None], layer)
        pa = _inproj_call(h, w_in, layer, 0, SEC_A_WIDTH, TN_PROJ_A, _gelu, "inproj_a")
        pb = _inproj_call(h, w_in, layer, COL_QB, SEC_B_WIDTH, TN_PROJ_B, _identity, "inproj_b")
        gates = _inproj_call(h, w_in, layer, COL_GATE, SEC_G_WIDTH, TN_PROJ_G, _sigmoid, "inproj_g")
        bs_bcast = jnp.broadcast_to(b_spatial[layer][:, :, None], (A_GROUPS, CHUNK, LANES))
        y = _mixer_call(pa, pb, kvc, v_gain[layer].reshape(1, A_WIDTH), w_spatial, bs_bcast,
                        jnp.tile(q_gain_b[layer], B_HEADS)[None],
                        jnp.tile(k_gain_b[layer], B_KV_HEADS)[None],
                        sinks[layer], jnp.tile(q_gain_c[layer], C_HEADS)[None], layer, s)
        merged = _merge_call(y, gates, w_branch, layer)
        w_router = jnp.pad(jnp.concatenate([w_router_group[layer], w_router_expert[layer]], axis=1),
                           ((0, 0), (0, pad)))
        b_router = jnp.pad(jnp.concatenate([b_router_group[layer], b_router_expert[layer]]),
                           (0, pad))[None]
        x1, logits = _outproj_call(merged, xf, w_out[layer].astype(BF16),
                                   norm_ffn[layer][None], w_router, b_router)
        route_i, route_w, counts = _route_call(logits)
        pos0, pos1, tile_expert, n_used = _plan(route_i, counts, n_tiles)
        src = _invert_call(pos0, pos1, n_rows)
        ys = _expert_call(x1, src, tile_expert, n_used, norm_ffn[layer][None], w_gate_e, w_up_e,
                          w_down_e, layer)
        gn_next = norm_mix[layer + 1] if layer + 1 < DEPTH else norm_mix[layer]
        xf, h = _combine_call(x1, route_w, pos0, pos1, ys, gn_next[None])
    return xf.reshape(b, s, d)
```

```python
import functools

import jax
import jax.numpy as jnp
from jax import lax
from jax.experimental import pallas as pl
from jax.experimental.pallas import tpu as pltpu

F32 = jnp.float32
BF16 = jnp.bfloat16
I32 = jnp.int32

D_MODEL = 2048
DEPTH = 2
MEM_LEN = 256
EPS = 1e-6
BRANCH_WIDTH = 1024
N_BRANCH = 3

CHUNK = 128
A_GROUPS = 8
A_GROUP_DIM = 128
A_WIDTH = A_GROUPS * A_GROUP_DIM

BLOCK = 128
B_HEADS = 16
B_KV_HEADS = 4
B_REP = B_HEADS // B_KV_HEADS
B_HEAD_DIM = 64
B_Q_WIDTH = B_HEADS * B_HEAD_DIM
B_KV_WIDTH = B_KV_HEADS * B_HEAD_DIM

C_HEADS = 4
C_HEAD_DIM = 256
C_WIDTH = C_HEADS * C_HEAD_DIM

COL_U = 0
COL_V = A_WIDTH
COL_QB = 2 * A_WIDTH
COL_KB = COL_QB + B_Q_WIDTH
COL_VB = COL_KB + B_KV_WIDTH
COL_QC = COL_VB + B_KV_WIDTH
COL_GATE = COL_QC + C_WIDTH
IN_WIDTH = COL_GATE + N_BRANCH * D_MODEL
SEC_A_WIDTH = COL_QB
SEC_B_WIDTH = COL_GATE - COL_QB
SEC_G_WIDTH = N_BRANCH * D_MODEL
PB_Q = 0
PB_K = COL_KB - COL_QB
PB_V = COL_VB - COL_QB
PB_QC = COL_QC - COL_QB

N_GROUPS = 4
EXPERTS_PER_GROUP = 4
N_EXPERTS = N_GROUPS * EXPERTS_PER_GROUP
TOP_K = 2
D_FF_EXPERT = 512

LANES = 128
VMEM_LIMIT = 56 * 1024 * 1024
SQRT_HALF = 0.7071067811865476

TM_NORM = 512
TM_PROJ = 1024
TN_PROJ_A, TN_PROJ_B, TN_PROJ_G = 1024, 512, 1536
TM_MERGE, TN_MERGE = 512, 1024
TM_OUT = 512
TM_ROUTE = 512
TK_INVERT = 2048
TM_EXPERT = 256
TM_COMBINE = 256


def _params(sem):
    return pltpu.CompilerParams(dimension_semantics=sem, vmem_limit_bytes=VMEM_LIMIT)


def _rms(x, gain):
    ms = jnp.mean(x * x, axis=-1, keepdims=True)
    return x * lax.rsqrt(ms + EPS) * gain


def _norm_body(x_ref, g_ref, o_ref):
    o_ref[...] = _rms(x_ref[...], g_ref[...]).astype(o_ref.dtype)


def _norm_call(x, gain_row):
    t = x.shape[0]
    return pl.pallas_call(
        _norm_body,
        grid=(t // TM_NORM,),
        in_specs=[pl.BlockSpec((TM_NORM, D_MODEL), lambda i: (i, 0)),
                  pl.BlockSpec((1, D_MODEL), lambda i: (0, 0))],
        out_specs=pl.BlockSpec((TM_NORM, D_MODEL), lambda i: (i, 0)),
        out_shape=jax.ShapeDtypeStruct((t, D_MODEL), BF16),
        compiler_params=_params(("arbitrary",)),
        name="norm",
    )(x, gain_row)


TN_MEMKV = 512


def _memkv_body(mem_ref, gm_ref, w_ref, gk_ref, o_ref):
    n = pl.program_id(0)
    h = _rms(mem_ref[...], gm_ref[...]).astype(BF16)
    acc = jnp.dot(h, w_ref[...].astype(BF16), preferred_element_type=F32)

    @pl.when(n < C_WIDTH // TN_MEMKV)
    def _():
        for j in range(TN_MEMKV // C_HEAD_DIM):
            sl = slice(j * C_HEAD_DIM, (j + 1) * C_HEAD_DIM)
            o_ref[:, sl] = _rms(acc[:, sl], gk_ref[...]).astype(o_ref.dtype)

    @pl.when(n >= C_WIDTH // TN_MEMKV)
    def _():
        o_ref[...] = acc.astype(o_ref.dtype)


def _memkv_call(mem2d, gm_row, w_mem_kv, gk_row, layer):
    rows = mem2d.shape[0]
    return pl.pallas_call(
        _memkv_body,
        grid=(2 * C_WIDTH // TN_MEMKV,),
        in_specs=[pl.BlockSpec((rows, D_MODEL), lambda n: (0, 0)),
                  pl.BlockSpec((1, D_MODEL), lambda n: (0, 0)),
                  pl.BlockSpec((None, D_MODEL, TN_MEMKV), lambda n: (layer, 0, n)),
                  pl.BlockSpec((1, C_HEAD_DIM), lambda n: (0, 0))],
        out_specs=pl.BlockSpec((rows, TN_MEMKV), lambda n: (0, n)),
        out_shape=jax.ShapeDtypeStruct((rows, 2 * C_WIDTH), BF16),
        compiler_params=_params(("arbitrary",)),
        name="memkv",
    )(mem2d, gm_row, w_mem_kv, gk_row)


def _gelu(x):
    return 0.5 * x * (1.0 + lax.erf(x * SQRT_HALF))


def _sigmoid(x):
    return 1.0 / (1.0 + jnp.exp(-x))


def _identity(x):
    return x


def _inproj_body(h_ref, w_ref, o_ref, wb_ref, *, act):
    @pl.when(pl.program_id(1) == 0)
    def _():
        wb_ref[...] = w_ref[...].astype(BF16)

    acc = jnp.dot(h_ref[...], wb_ref[...], preferred_element_type=F32)
    o_ref[...] = act(acc).astype(o_ref.dtype)


def _inproj_call(h, w_in, layer, col0, width, tn, act, name):
    t = h.shape[0]
    assert col0 % tn == 0 and width % tn == 0
    n0 = col0 // tn
    return pl.pallas_call(
        functools.partial(_inproj_body, act=act),
        grid=(width // tn, t // TM_PROJ),
        in_specs=[pl.BlockSpec((TM_PROJ, D_MODEL), lambda n, m: (m, 0)),
                  pl.BlockSpec((None, D_MODEL, tn), lambda n, m: (layer, 0, n0 + n))],
        out_specs=pl.BlockSpec((TM_PROJ, tn), lambda n, m: (m, n)),
        out_shape=jax.ShapeDtypeStruct((t, width), BF16),
        scratch_shapes=[pltpu.VMEM((D_MODEL, tn), BF16)],
        compiler_params=_params(("arbitrary", "arbitrary")),
        name=name,
    )(h, w_in)


def _group_sums(width, group):
    gid = jnp.arange(width, dtype=I32) // group
    s = (gid[:, None] == jnp.arange(LANES, dtype=I32)[None, :]).astype(BF16)
    return s, s.T


def _group_ssq(x, s):
    return jnp.dot((x * x).astype(BF16), s, preferred_element_type=F32)


def _group_bcast(ssq, e, group):
    r = lax.rsqrt(ssq * (1.0 / group) + EPS)
    r_hi = r.astype(BF16)
    r_lo = (r - r_hi.astype(F32)).astype(BF16)
    return (jnp.dot(r_hi, e, preferred_element_type=F32)
            + jnp.dot(r_lo, e, preferred_element_type=F32))


def _mixer_body(a_ref, b_ref, prev_ref, kvc_ref, vgain_ref, ws_ref, bs_ref, qgb_ref, kgb_ref,
                sink_ref, qgc_ref, s64_ref, e64_ref, s128_ref, e128_ref, s256_ref, e256_ref,
                y_ref, *, blocks_per_seq):
    i = pl.program_id(0)
    n = lax.rem(i, blocks_per_seq)
    ones_k = jnp.ones((2 * BLOCK, LANES), BF16)
    row = lax.broadcasted_iota(I32, (CHUNK, CHUNK), 0)
    col = lax.broadcasted_iota(I32, (CHUNK, CHUNK), 1)
    causal = col <= row
    lane_k = lax.broadcasted_iota(I32, (2 * BLOCK, LANES), 1)
    lane_q = lax.broadcasted_iota(I32, (BLOCK, LANES), 1)
    lo_k = lane_k < B_HEAD_DIM
    lo_q = lane_q < B_HEAD_DIM
    qi = lax.broadcasted_iota(I32, (B_REP * BLOCK, 2 * BLOCK), 0) & (BLOCK - 1)
    kj = lax.broadcasted_iota(I32, (B_REP * BLOCK, 2 * BLOCK), 1)
    rel = BLOCK + qi - kj
    valid = (rel >= 0) & (rel < BLOCK) & ((n > 0) | (kj >= BLOCK))

    v_all = a_ref[:, COL_V:COL_V + A_WIDTH].astype(F32)
    q_all = b_ref[:, PB_Q:PB_Q + B_Q_WIDTH].astype(F32)
    k_all = jnp.concatenate([prev_ref[:, 0:B_KV_WIDTH], b_ref[:, PB_K:PB_K + B_KV_WIDTH]],
                            axis=0).astype(F32)
    qc_all = b_ref[:, PB_QC:PB_QC + C_WIDTH].astype(F32)
    ssq_q = _group_ssq(q_all, s64_ref[...])
    ssq_k = _group_ssq(k_all, s64_ref[0:B_KV_WIDTH, :])
    ssq_c = _group_ssq(qc_all, s256_ref[...])
    ssq_v = _group_ssq(v_all, s128_ref[...])
    qn_all = (q_all * _group_bcast(ssq_q, e64_ref[...], B_HEAD_DIM)
              * (qgb_ref[...] * (B_HEAD_DIM ** -0.5)))
    kn_all = k_all * _group_bcast(ssq_k, e64_ref[:, 0:B_KV_WIDTH], B_HEAD_DIM) * kgb_ref[...]
    qcn_all = (qc_all * _group_bcast(ssq_c, e256_ref[...], C_HEAD_DIM)
               * (qgc_ref[...] * (C_HEAD_DIM ** -0.5))).astype(BF16)
    vn_all = (v_all * _group_bcast(ssq_v, e128_ref[...], A_GROUP_DIM)
              * vgain_ref[...]).astype(BF16)
    v_kv = jnp.concatenate([prev_ref[:, B_KV_WIDTH:2 * B_KV_WIDTH],
                            b_ref[:, PB_V:PB_V + B_KV_WIDTH]], axis=0).astype(F32)

    scores_c = [
        lax.dot_general(qcn_all[:, h * C_HEAD_DIM:(h + 1) * C_HEAD_DIM],
                        kvc_ref[:, h * C_HEAD_DIM:(h + 1) * C_HEAD_DIM],
                        (((1,), (1,)), ((), ())), preferred_element_type=F32)
        for h in range(C_HEADS)]
    vdups, sinks, scores_b = [], [], []
    for slab in range(B_KV_WIDTH // LANES):
        ks = kn_all[:, slab * LANES:(slab + 1) * LANES]
        vs = v_kv[:, slab * LANES:(slab + 1) * LANES]
        kr = pltpu.roll(ks, B_HEAD_DIM, axis=1)
        vr = pltpu.roll(vs, B_HEAD_DIM, axis=1)
        for par in range(2):
            kv = 2 * slab + par
            kdup = (jnp.where(lo_k, ks, kr) if par == 0 else jnp.where(lo_k, kr, ks)).astype(BF16)
            vdups.append((jnp.where(lo_k, vs, vr) if par == 0
                          else jnp.where(lo_k, vr, vs)).astype(BF16))
            qs, sk = [], []
            for r in range(B_REP):
                h = kv * B_REP + r
                q_slab = qn_all[:, (h // 2) * LANES:(h // 2 + 1) * LANES]
                keep = lo_q if h % 2 == 0 else jnp.logical_not(lo_q)
                qs.append(jnp.where(keep, q_slab, 0.0).astype(BF16))
                sk.append(jnp.full((BLOCK, 1), sink_ref[h], F32))
            sinks.append(jnp.concatenate(sk, axis=0))
            scores_b.append(lax.dot_general(jnp.concatenate(qs, axis=0), kdup,
                                            (((1,), (1,)), ((), ())),
                                            preferred_element_type=F32))

    probs_c = [jnp.exp(s - jnp.max(s, axis=-1, keepdims=True)).astype(BF16) for s in scores_c]
    mixed_a = [
        jnp.dot(jnp.where(causal, ws_ref[g], 0.0).astype(BF16),
                vn_all[:, g * A_GROUP_DIM:(g + 1) * A_GROUP_DIM],
                preferred_element_type=F32) + bs_ref[g]
        for g in range(A_GROUPS)]
    probs_b, sink_terms = [], []
    for s, sink in zip(scores_b, sinks):
        s = jnp.where(valid, s, -jnp.inf)
        mx = jnp.maximum(jnp.max(s, axis=-1, keepdims=True), sink)
        probs_b.append(jnp.exp(s - mx).astype(BF16))
        sink_terms.append(jnp.exp(sink - mx))

    outs_c = [(jnp.dot(p, kvc_ref[:, C_WIDTH + h * C_HEAD_DIM:C_WIDTH + (h + 1) * C_HEAD_DIM],
                       preferred_element_type=F32),
               jnp.dot(p, ones_k, preferred_element_type=F32))
              for h, p in enumerate(probs_c)]
    for g, mixed in enumerate(mixed_a):
        sl = slice(g * A_GROUP_DIM, (g + 1) * A_GROUP_DIM)
        u = a_ref[:, COL_U + g * A_GROUP_DIM:COL_U + (g + 1) * A_GROUP_DIM].astype(F32)
        y_ref[:, sl] = (u * mixed).astype(y_ref.dtype)
    outs_b = [(jnp.dot(p, vdup, preferred_element_type=F32),
               jnp.dot(p, ones_k, preferred_element_type=F32) + st)
              for p, vdup, st in zip(probs_b, vdups, sink_terms)]

    for h, (o, denom) in enumerate(outs_c):
        o = o / jnp.concatenate([denom] * (C_HEAD_DIM // LANES), axis=1)
        y_ref[:, A_WIDTH + B_Q_WIDTH + h * C_HEAD_DIM:
              A_WIDTH + B_Q_WIDTH + (h + 1) * C_HEAD_DIM] = o.astype(y_ref.dtype)
    for kv, (o, denom) in enumerate(outs_b):
        o = o / denom
        for pr in range(B_REP // 2):
            even = o[(2 * pr) * BLOCK:(2 * pr + 1) * BLOCK]
            odd = o[(2 * pr + 1) * BLOCK:(2 * pr + 2) * BLOCK]
            c0 = A_WIDTH + (kv * B_REP // 2 + pr) * LANES
            y_ref[:, c0:c0 + LANES] = jnp.where(lo_q, even, odd).astype(y_ref.dtype)


def _mixer_call(pa, pb, kvc, vgain_row, w_spatial, bs_bcast, qgb_row, kgb_row, sinks, qgc_row,
                layer, seq):
    t = pa.shape[0]
    bps = seq // BLOCK
    kvb = PB_K // (2 * B_KV_WIDTH)
    s64, e64 = _group_sums(B_Q_WIDTH, B_HEAD_DIM)
    s128, e128 = _group_sums(A_WIDTH, A_GROUP_DIM)
    s256, e256 = _group_sums(C_WIDTH, C_HEAD_DIM)

    def const(shape):
        return pl.BlockSpec(shape, lambda i: (0,) * len(shape))

    return pl.pallas_call(
        functools.partial(_mixer_body, blocks_per_seq=bps),
        grid=(t // BLOCK,),
        in_specs=[
            pl.BlockSpec((BLOCK, SEC_A_WIDTH), lambda i: (i, 0)),
            pl.BlockSpec((BLOCK, SEC_B_WIDTH), lambda i: (i, 0)),
            pl.BlockSpec((BLOCK, 2 * B_KV_WIDTH), lambda i: (jnp.maximum(i - 1, 0), kvb)),
            pl.BlockSpec((MEM_LEN, 2 * C_WIDTH), lambda i: (i // bps, 0)),
            const((1, A_WIDTH)),
            pl.BlockSpec((None, A_GROUPS, CHUNK, CHUNK), lambda i: (layer, 0, 0, 0)),
            const((A_GROUPS, CHUNK, LANES)),
            const((1, B_Q_WIDTH)),
            const((1, B_KV_WIDTH)),
            pl.BlockSpec(memory_space=pltpu.SMEM),
            const((1, C_WIDTH)),
            const((B_Q_WIDTH, LANES)), const((LANES, B_Q_WIDTH)),
            const((A_WIDTH, LANES)), const((LANES, A_WIDTH)),
            const((C_WIDTH, LANES)), const((LANES, C_WIDTH)),
        ],
        out_specs=pl.BlockSpec((BLOCK, N_BRANCH * BRANCH_WIDTH), lambda i: (i, 0)),
        out_shape=jax.ShapeDtypeStruct((t, N_BRANCH * BRANCH_WIDTH), BF16),
        compiler_params=_params(("arbitrary",)),
        name="mixer",
    )(pa, pb, pb, kvc, vgain_row, w_spatial, bs_bcast, qgb_row, kgb_row, sinks, qgc_row,
      s64, e64, s128, e128, s256, e256)


def _merge_body(y_ref, g0_ref, g1_ref, g2_ref, w_ref, o_ref, wb_ref):
    m = pl.program_id(1)

    @pl.when(m == 0)
    def _():
        wb_ref[...] = w_ref[...].astype(BF16)

    acc = None
    for b, g_ref in enumerate((g0_ref, g1_ref, g2_ref)):
        wide = jnp.dot(y_ref[:, b * BRANCH_WIDTH:(b + 1) * BRANCH_WIDTH], wb_ref[b],
                       preferred_element_type=F32)
        term = g_ref[...].astype(F32) * wide
        acc = term if acc is None else acc + term
    o_ref[...] = acc.astype(o_ref.dtype)


def _merge_call(y, gates, w_branch, layer):
    t = y.shape[0]
    per = D_MODEL // TN_MERGE

    def gate_spec(b):
        return pl.BlockSpec((TM_MERGE, TN_MERGE), lambda n, m: (m, b * per + n))

    return pl.pallas_call(
        _merge_body,
        grid=(D_MODEL // TN_MERGE, t // TM_MERGE),
        in_specs=[pl.BlockSpec((TM_MERGE, N_BRANCH * BRANCH_WIDTH), lambda n, m: (m, 0)),
                  gate_spec(0), gate_spec(1), gate_spec(2),
                  pl.BlockSpec((None, N_BRANCH, BRANCH_WIDTH, TN_MERGE),
                               lambda n, m: (layer, 0, 0, n))],
        out_specs=pl.BlockSpec((TM_MERGE, TN_MERGE), lambda n, m: (m, n)),
        out_shape=jax.ShapeDtypeStruct((t, D_MODEL), BF16),
        scratch_shapes=[pltpu.VMEM((N_BRANCH, BRANCH_WIDTH, TN_MERGE), BF16)],
        compiler_params=_params(("arbitrary", "arbitrary")),
        name="merge",
    )(y, gates, gates, gates, w_branch)


def _outproj_body(mg_ref, x_ref, w_ref, gn_ref, wr_ref, br_ref, x1_ref, lg_ref,
                  wrh_ref, wrl_ref):
    i = pl.program_id(0)

    @pl.when(i == 0)
    def _():
        wr = wr_ref[...]
        hi = wr.astype(BF16)
        wrh_ref[...] = hi
        wrl_ref[...] = (wr - hi.astype(F32)).astype(BF16)

    x1 = x_ref[...] + jnp.dot(mg_ref[...], w_ref[...], preferred_element_type=F32)
    x1_ref[...] = x1
    h2 = _rms(x1, gn_ref[...])
    hi = h2.astype(BF16)
    lo = (h2 - hi.astype(F32)).astype(BF16)
    lg = (jnp.dot(hi, wrh_ref[...], preferred_element_type=F32)
          + jnp.dot(lo, wrh_ref[...], preferred_element_type=F32)
          + jnp.dot(hi, wrl_ref[...], preferred_element_type=F32))
    lg_ref[...] = lg + br_ref[...]


def _outproj_call(merged, x, w_out_bf16, gn_row, w_router, b_router):
    t = x.shape[0]
    return pl.pallas_call(
        _outproj_body,
        grid=(t // TM_OUT,),
        in_specs=[pl.BlockSpec((TM_OUT, D_MODEL), lambda i: (i, 0)),
                  pl.BlockSpec((TM_OUT, D_MODEL), lambda i: (i, 0)),
                  pl.BlockSpec((D_MODEL, D_MODEL), lambda i: (0, 0),
                               pipeline_mode=pl.Buffered(1)),
                  pl.BlockSpec((1, D_MODEL), lambda i: (0, 0)),
                  pl.BlockSpec((D_MODEL, LANES), lambda i: (0, 0)),
                  pl.BlockSpec((1, LANES), lambda i: (0, 0))],
        out_specs=[pl.BlockSpec((TM_OUT, D_MODEL), lambda i: (i, 0)),
                   pl.BlockSpec((TM_OUT, LANES), lambda i: (i, 0))],
        out_shape=[jax.ShapeDtypeStruct((t, D_MODEL), F32),
                   jax.ShapeDtypeStruct((t, LANES), F32)],
        scratch_shapes=[pltpu.VMEM((D_MODEL, LANES), BF16),
                        pltpu.VMEM((D_MODEL, LANES), BF16)],
        compiler_params=_params(("arbitrary",)),
        name="outproj",
    )(merged, x, w_out_bf16, gn_row, w_router, b_router)


def _route_body(lg_ref, ri_ref, rw_ref, cnt_ref, carry_ref):
    i = pl.program_id(0)

    @pl.when(i == 0)
    def _():
        carry_ref[...] = jnp.zeros_like(carry_ref)

    lg = lg_ref[...]
    tm = lg.shape[0]
    lane = lax.broadcasted_iota(I32, lg.shape, 1)
    neg = -jnp.inf
    big = jnp.int32(LANES)

    is_g = lane < N_GROUPS
    gl = jnp.where(is_g, lg, neg)
    gmax = jnp.max(gl, axis=-1, keepdims=True)
    gidx = jnp.min(jnp.where(gl == gmax, lane, big), axis=-1, keepdims=True)
    p_g = 1.0 / jnp.sum(jnp.where(is_g, jnp.exp(lg - gmax), 0.0), axis=-1, keepdims=True)

    e_lane = lane - N_GROUPS
    in_group = (e_lane >= 0) & (e_lane < N_EXPERTS) & ((e_lane >> 2) == gidx)
    el = jnp.where(in_group, lg, neg)
    v0 = jnp.max(el, axis=-1, keepdims=True)
    i0 = jnp.min(jnp.where(el == v0, lane, big), axis=-1, keepdims=True)
    el1 = jnp.where(lane == i0, neg, el)
    v1 = jnp.max(el1, axis=-1, keepdims=True)
    i1 = jnp.min(jnp.where(el1 == v1, lane, big), axis=-1, keepdims=True)
    e0 = i0 - N_GROUPS
    e1 = i1 - N_GROUPS
    ex = jnp.exp(v1 - v0)
    w0 = p_g / (1.0 + ex)
    w1 = p_g * ex / (1.0 + ex)

    oh0 = lane == e0
    oh1 = lane == e1
    oh = jnp.where(oh0 | oh1, 1.0, 0.0)
    r_i = lax.broadcasted_iota(I32, (tm, tm), 0)
    c_i = lax.broadcasted_iota(I32, (tm, tm), 1)
    lower = jnp.where(c_i < r_i, 1.0, 0.0).astype(BF16)
    before = jnp.dot(lower, oh.astype(BF16), preferred_element_type=F32) + carry_ref[...]
    rank0 = jnp.sum(jnp.where(oh0, before, 0.0), axis=-1, keepdims=True).astype(I32)
    rank1 = jnp.sum(jnp.where(oh1, before, 0.0), axis=-1, keepdims=True).astype(I32)
    carry_ref[...] = carry_ref[...] + jnp.sum(oh, axis=0, keepdims=True)

    zero_i = jnp.zeros_like(lane)
    ri_ref[...] = jnp.where(lane == 0, e0, jnp.where(lane == 1, e1,
                            jnp.where(lane == 2, rank0, jnp.where(lane == 3, rank1, zero_i))))
    rw_ref[...] = jnp.where(lane == 0, w0, jnp.where(lane == 1, w1, 0.0))
    cnt_ref[...] = jnp.broadcast_to(carry_ref[...], cnt_ref.shape).astype(I32)


def _route_call(logits):
    t = logits.shape[0]
    return pl.pallas_call(
        _route_body,
        grid=(t // TM_ROUTE,),
        in_specs=[pl.BlockSpec((TM_ROUTE, LANES), lambda i: (i, 0))],
        out_specs=[pl.BlockSpec((TM_ROUTE, LANES), lambda i: (i, 0)),
                   pl.BlockSpec((TM_ROUTE, LANES), lambda i: (i, 0)),
                   pl.BlockSpec((8, LANES), lambda i: (0, 0))],
        out_shape=[jax.ShapeDtypeStruct((t, LANES), I32),
                   jax.ShapeDtypeStruct((t, LANES), F32),
                   jax.ShapeDtypeStruct((8, LANES), I32)],
        scratch_shapes=[pltpu.VMEM((1, LANES), F32)],
        compiler_params=_params(("arbitrary",)),
        name="route",
    )(logits)


def _plan(route_i, counts, n_tiles):
    c = counts[0, :N_EXPERTS]
    tiles = (c + TM_EXPERT - 1) // TM_EXPERT
    ctiles = jnp.cumsum(tiles)
    start = (ctiles - tiles) * TM_EXPERT
    eid = jnp.arange(N_EXPERTS, dtype=I32)

    def slot(e, r):
        return jnp.sum(jnp.where(e[:, None] == eid[None, :], start[None, :], 0), axis=1) + r

    pos0 = slot(route_i[:, 0], route_i[:, 2]).astype(I32)
    pos1 = slot(route_i[:, 1], route_i[:, 3]).astype(I32)
    n_used = ctiles[-1]
    tid = jnp.minimum(jnp.arange(n_tiles, dtype=I32), n_used - 1)
    tile_expert = jnp.sum(tid[:, None] >= ctiles[None, :], axis=1).astype(I32)
    return pos0, pos1, tile_expert, n_used.reshape(1).astype(I32)


def _invert_body(p0_ref, p1_ref, src_ref):
    i = pl.program_id(0)

    @pl.when(i == 0)
    def _():
        def fill(r, c):
            src_ref[r] = 0
            return c
        lax.fori_loop(0, src_ref.shape[0], fill, 0, unroll=8)

    base = i * TK_INVERT

    def scatter(t, c):
        src_ref[p0_ref[t]] = base + t
        src_ref[p1_ref[t]] = base + t
        return c
    lax.fori_loop(0, TK_INVERT, scatter, 0, unroll=8)


def _invert_call(pos0, pos1, n_rows):
    t = pos0.shape[0]
    smem = pl.BlockSpec((TK_INVERT,), lambda i: (i,), memory_space=pltpu.SMEM)
    return pl.pallas_call(
        _invert_body,
        grid=(t // TK_INVERT,),
        in_specs=[smem, smem],
        out_specs=pl.BlockSpec(memory_space=pltpu.SMEM),
        out_shape=jax.ShapeDtypeStruct((n_rows,), I32),
        compiler_params=_params(("arbitrary",)),
        name="invert",
    )(pos0, pos1)


def _expert_body(te_ref, nu_ref, srcc_ref, srcn_ref, x_hbm, gn_ref, wg_ref, wu_ref, wd_ref,
                 y_ref, xbuf_ref, wgb_ref, wub_ref, wdb_ref, sems):
    i = pl.program_id(0)
    n_used = nu_ref[0]
    slot = lax.rem(i, 2)

    def row_copy(src_ref, r, s):
        return pltpu.make_async_copy(x_hbm.at[pl.ds(src_ref[r], 1)],
                                     xbuf_ref.at[s, pl.ds(r, 1)], sems.at[s])

    def issue(src_ref, s):
        for r in range(TM_EXPERT):
            row_copy(src_ref, r, s).start(priority=r % 2)

    @pl.when(i == 0)
    def _():
        issue(srcc_ref, 0)

    @pl.when(i < n_used)
    def _():
        prev = te_ref[jnp.maximum(i - 1, 0)]

        @pl.when(jnp.logical_or(i == 0, te_ref[i] != prev))
        def _():
            wgb_ref[...] = wg_ref[...].astype(BF16)
            wub_ref[...] = wu_ref[...].astype(BF16)
            wdb_ref[...] = wd_ref[...].astype(BF16)

        def drain(r, c):
            row_copy(srcc_ref, r, slot).wait()
            return c
        lax.fori_loop(0, TM_EXPERT, drain, 0, unroll=8)

    def compute(s, prefetch):
        h = _rms(xbuf_ref[s], gn_ref[...]).astype(BF16)
        if prefetch:
            issue(srcn_ref, 1 - s)
        g = jnp.dot(h, wgb_ref[...], preferred_element_type=F32)
        u = jnp.dot(h, wub_ref[...], preferred_element_type=F32)
        hid = (g * _sigmoid(g) * u).astype(BF16)
        y_ref[...] = jnp.dot(hid, wdb_ref[...], preferred_element_type=F32)

    for s in range(2):
        @pl.when(jnp.logical_and(i + 1 < n_used, slot == s))
        def _():
            compute(s, True)

        @pl.when(jnp.logical_and(i + 1 == n_used, slot == s))
        def _():
            compute(s, False)

    @pl.when(i >= n_used)
    def _():
        y_ref[...] = jnp.zeros_like(y_ref)


def _expert_call(x1, src, tile_expert, n_used, gn_row, w_gate, w_up, w_down, layer):
    n_rows = src.shape[0]
    n_tiles = n_rows // TM_EXPERT

    def w_map(i, te, nu):
        return (layer, te[i], 0, 0)

    grid_spec = pltpu.PrefetchScalarGridSpec(
        num_scalar_prefetch=2,
        grid=(n_tiles,),
        in_specs=[pl.BlockSpec((TM_EXPERT,), lambda i, te, nu: (i,), memory_space=pltpu.SMEM),
                  pl.BlockSpec((TM_EXPERT,), lambda i, te, nu: (jnp.minimum(i + 1, n_tiles - 1),),
                               memory_space=pltpu.SMEM),
                  pl.BlockSpec(memory_space=pl.ANY),
                  pl.BlockSpec((1, D_MODEL), lambda i, te, nu: (0, 0)),
                  pl.BlockSpec((None, None, D_MODEL, D_FF_EXPERT), w_map),
                  pl.BlockSpec((None, None, D_MODEL, D_FF_EXPERT), w_map),
                  pl.BlockSpec((None, None, D_FF_EXPERT, D_MODEL), w_map)],
        out_specs=pl.BlockSpec((TM_EXPERT, D_MODEL), lambda i, te, nu: (i, 0)),
        scratch_shapes=[pltpu.VMEM((2, TM_EXPERT, D_MODEL), F32),
                        pltpu.VMEM((D_MODEL, D_FF_EXPERT), BF16),
                        pltpu.VMEM((D_MODEL, D_FF_EXPERT), BF16),
                        pltpu.VMEM((D_FF_EXPERT, D_MODEL), BF16),
                        pltpu.SemaphoreType.DMA((2,))],
    )
    return pl.pallas_call(
        _expert_body,
        grid_spec=grid_spec,
        out_shape=jax.ShapeDtypeStruct((n_rows, D_MODEL), F32),
        compiler_params=_params(("arbitrary",)),
        name="experts",
    )(tile_expert, n_used, src, src, x1, gn_row, w_gate, w_up, w_down)


def _combine_body(p0c_ref, p1c_ref, p0n_ref, p1n_ref, x1_ref, rw_ref, gn_ref, ys_hbm,
                  x2_ref, h_ref, buf_ref, sems, *, n_steps):
    i = pl.program_id(0)
    slot = lax.rem(i, 2)

    def row_copy(p_ref, k, t, s):
        return pltpu.make_async_copy(ys_hbm.at[pl.ds(p_ref[t], 1)],
                                     buf_ref.at[s, k, pl.ds(t, 1)], sems.at[s])

    def issue(p0_ref, p1_ref, s):
        for t in range(TM_COMBINE):
            row_copy(p0_ref, 0, t, s).start(priority=0)
            row_copy(p1_ref, 1, t, s).start(priority=1)

    @pl.when(i == 0)
    def _():
        issue(p0c_ref, p1c_ref, 0)

    def drain(t, c):
        row_copy(p0c_ref, 0, t, slot).wait()
        row_copy(p1c_ref, 1, t, slot).wait()
        return c
    lax.fori_loop(0, TM_COMBINE, drain, 0, unroll=8)

    def compute(s, prefetch):
        rw = rw_ref[...]
        x2 = x1_ref[...] + rw[:, 0:1] * buf_ref[s, 0] + rw[:, 1:2] * buf_ref[s, 1]
        if prefetch:
            issue(p0n_ref, p1n_ref, 1 - s)
        x2_ref[...] = x2
        h_ref[...] = _rms(x2, gn_ref[...]).astype(h_ref.dtype)

    for s in range(2):
        @pl.when(jnp.logical_and(i + 1 < n_steps, slot == s))
        def _():
            compute(s, True)

        @pl.when(jnp.logical_and(i + 1 == n_steps, slot == s))
        def _():
            compute(s, False)


def _combine_call(x1, route_w, pos0, pos1, ys, gn_row):
    t = x1.shape[0]
    n_steps = t // TM_COMBINE

    def nxt(i):
        return (jnp.minimum(i + 1, n_steps - 1),)

    smem = functools.partial(pl.BlockSpec, (TM_COMBINE,), memory_space=pltpu.SMEM)
    return pl.pallas_call(
        functools.partial(_combine_body, n_steps=n_steps),
        grid=(n_steps,),
        in_specs=[smem(lambda i: (i,)), smem(lambda i: (i,)), smem(nxt), smem(nxt),
                  pl.BlockSpec((TM_COMBINE, D_MODEL), lambda i: (i, 0)),
                  pl.BlockSpec((TM_COMBINE, LANES), lambda i: (i, 0)),
                  pl.BlockSpec((1, D_MODEL), lambda i: (0, 0)),
                  pl.BlockSpec(memory_space=pl.ANY)],
        out_specs=[pl.BlockSpec((TM_COMBINE, D_MODEL), lambda i: (i, 0)),
                   pl.BlockSpec((TM_COMBINE, D_MODEL), lambda i: (i, 0))],
        out_shape=[jax.ShapeDtypeStruct((t, D_MODEL), F32),
                   jax.ShapeDtypeStruct((t, D_MODEL), BF16)],
        scratch_shapes=[pltpu.VMEM((2, TOP_K, TM_COMBINE, D_MODEL), F32),
                        pltpu.SemaphoreType.DMA((2,))],
        compiler_params=_params(("arbitrary",)),
        name="combine",
    )(pos0, pos1, pos0, pos1, x1, route_w, gn_row, ys)


def kernel(x, mem, norm_mix, norm_mem, norm_ffn, w_in, v_gain, w_spatial, b_spatial,
           q_gain_b, k_gain_b, sinks, q_gain_c, k_gain_c, w_mem_kv, w_branch, w_out,
           w_router_group, b_router_group, w_router_expert, b_router_expert,
           w_gate_e, w_up_e, w_down_e):
    b, s, d = x.shape
    t = b * s
    assert d == D_MODEL and s % BLOCK == 0 and mem.shape[1] == MEM_LEN
    assert t % TK_INVERT == 0 and t % TM_PROJ == 0
    n_tiles = (t * TOP_K) // TM_EXPERT + N_EXPERTS
    n_rows = n_tiles * TM_EXPERT

    xf = x.reshape(t, d)
    mem2d = mem.reshape(b * MEM_LEN, d)
    pad = LANES - N_GROUPS - N_EXPERTS

    h = _norm_call(xf, norm_mix[0][None])
    for layer in range(DEPTH):
        kvc = _memkv_call(mem2d, norm_mem[layer][None], w_mem_kv, k_gain_c[layer][None], layer)
        pa = _inproj_call(h, w_in, layer, 0, SEC_A_WIDTH, TN_PROJ_A, _gelu, "inproj_a")
        pb = _inproj_call(h, w_in, layer, COL_QB, SEC_B_WIDTH, TN_PROJ_B, _identity, "inproj_b")
        gates = _inproj_call(h, w_in, layer, COL_GATE, SEC_G_WIDTH, TN_PROJ_G, _sigmoid, "inproj_g")
        bs_bcast = jnp.broadcast_to(b_spatial[layer][:, :, None], (A_GROUPS, CHUNK, LANES))
        y = _mixer_call(pa, pb, kvc, v_gain[layer].reshape(1, A_WIDTH), w_spatial, bs_bcast,
                        jnp.tile(q_gain_b[layer], B_HEADS)[None],
                        jnp.tile(k_gain_b[layer], B_KV_HEADS)[None],
                        sinks[layer], jnp.tile(q_gain_c[layer], C_HEADS)[None], layer, s)
        merged = _merge_call(y, gates, w_branch, layer)
        w_router = jnp.pad(jnp.concatenate([w_router_group[layer], w_router_expert[layer]], axis=1),
                           ((0, 0), (0, pad)))
        b_router = jnp.pad(jnp.concatenate([b_router_group[layer], b_router_expert[layer]]),
                           (0, pad))[None]
        x1, logits = _outproj_call(merged, xf, w_out[layer].astype(BF16),
                                   norm_ffn[layer][None], w_router, b_router)
        route_i, route_w, counts = _route_call(logits)
        pos0, pos1, tile_expert, n_used = _plan(route_i, counts, n_tiles)
        src = _invert_call(pos0, pos1, n_rows)
        ys = _expert_call(x1, src, tile_expert, n_used, norm_ffn[layer][None], w_gate_e, w_up_e,
                          w_down_e, layer)
        gn_next = norm_mix[layer + 1] if layer + 1 < DEPTH else norm_mix[layer]
        xf, h = _combine_call(x1, route_w, pos0, pos1, ys, gn_next[None])
    return xf.reshape(b, s, d)
```

```python
import functools

import jax
import jax.numpy as jnp
from jax import lax
from jax.experimental import pallas as pl
from jax.experimental.pallas import tpu as pltpu

F32 = jnp.float32
BF16 = jnp.bfloat16
I32 = jnp.int32

D_MODEL = 2048
DEPTH = 2
MEM_LEN = 256
EPS = 1e-6
BRANCH_WIDTH = 1024
N_BRANCH = 3

CHUNK = 128
A_GROUPS = 8
A_GROUP_DIM = 128
A_WIDTH = A_GROUPS * A_GROUP_DIM

BLOCK = 128
B_HEADS = 16
B_KV_HEADS = 4
B_REP = B_HEADS // B_KV_HEADS
B_HEAD_DIM = 64
B_Q_WIDTH = B_HEADS * B_HEAD_DIM
B_KV_WIDTH = B_KV_HEADS * B_HEAD_DIM

C_HEADS = 4
C_HEAD_DIM = 256
C_WIDTH = C_HEADS * C_HEAD_DIM

COL_U = 0
COL_V = A_WIDTH
COL_QB = 2 * A_WIDTH
COL_KB = COL_QB + B_Q_WIDTH
COL_VB = COL_KB + B_KV_WIDTH
COL_QC = COL_VB + B_KV_WIDTH
COL_GATE = COL_QC + C_WIDTH
IN_WIDTH = COL_GATE + N_BRANCH * D_MODEL
SEC_A_WIDTH = COL_QB
SEC_B_WIDTH = COL_GATE - COL_QB
SEC_G_WIDTH = N_BRANCH * D_MODEL
PB_Q = 0
PB_K = COL_KB - COL_QB
PB_V = COL_VB - COL_QB
PB_QC = COL_QC - COL_QB

N_GROUPS = 4
EXPERTS_PER_GROUP = 4
N_EXPERTS = N_GROUPS * EXPERTS_PER_GROUP
TOP_K = 2
D_FF_EXPERT = 512

LANES = 128
VMEM_LIMIT = 56 * 1024 * 1024
SQRT_HALF = 0.7071067811865476

TM_NORM = 512
TM_PROJ = 1024
TN_PROJ_A, TN_PROJ_B, TN_PROJ_G = 1024, 1280, 1536
TM_MERGE, TN_MERGE = 512, 1024
TM_OUT = 512
TM_ROUTE = 512
TK_INVERT = 2048
TM_EXPERT = 256
TM_COMBINE = 256


def _params(sem):
    return pltpu.CompilerParams(dimension_semantics=sem, vmem_limit_bytes=VMEM_LIMIT)


def _rms(x, gain):
    ms = jnp.mean(x * x, axis=-1, keepdims=True)
    return x * lax.rsqrt(ms + EPS) * gain


def _norm_body(x_ref, g_ref, o_ref):
    o_ref[...] = _rms(x_ref[...], g_ref[...]).astype(o_ref.dtype)


def _norm_call(x, gain_row):
    t = x.shape[0]
    return pl.pallas_call(
        _norm_body,
        grid=(t // TM_NORM,),
        in_specs=[pl.BlockSpec((TM_NORM, D_MODEL), lambda i: (i, 0)),
                  pl.BlockSpec((1, D_MODEL), lambda i: (0, 0))],
        out_specs=pl.BlockSpec((TM_NORM, D_MODEL), lambda i: (i, 0)),
        out_shape=jax.ShapeDtypeStruct((t, D_MODEL), BF16),
        compiler_params=_params(("arbitrary",)),
        name="norm",
    )(x, gain_row)


TN_MEMKV = 512


def _memkv_body(mem_ref, gm_ref, w_ref, gk_ref, o_ref):
    n = pl.program_id(0)
    h = _rms(mem_ref[...], gm_ref[...]).astype(BF16)
    acc = jnp.dot(h, w_ref[...].astype(BF16), preferred_element_type=F32)

    @pl.when(n < C_WIDTH // TN_MEMKV)
    def _():
        for j in range(TN_MEMKV // C_HEAD_DIM):
            sl = slice(j * C_HEAD_DIM, (j + 1) * C_HEAD_DIM)
            o_ref[:, sl] = _rms(acc[:, sl], gk_ref[...]).astype(o_ref.dtype)

    @pl.when(n >= C_WIDTH // TN_MEMKV)
    def _():
        o_ref[...] = acc.astype(o_ref.dtype)


def _memkv_call(mem2d, gm_row, w_mem_kv, gk_row, layer):
    rows = mem2d.shape[0]
    return pl.pallas_call(
        _memkv_body,
        grid=(2 * C_WIDTH // TN_MEMKV,),
        in_specs=[pl.BlockSpec((rows, D_MODEL), lambda n: (0, 0)),
                  pl.BlockSpec((1, D_MODEL), lambda n: (0, 0)),
                  pl.BlockSpec((None, D_MODEL, TN_MEMKV), lambda n: (layer, 0, n)),
                  pl.BlockSpec((1, C_HEAD_DIM), lambda n: (0, 0))],
        out_specs=pl.BlockSpec((rows, TN_MEMKV), lambda n: (0, n)),
        out_shape=jax.ShapeDtypeStruct((rows, 2 * C_WIDTH), BF16),
        compiler_params=_params(("arbitrary",)),
        name="memkv",
    )(mem2d, gm_row, w_mem_kv, gk_row)


def _gelu(x):
    return 0.5 * x * (1.0 + lax.erf(x * SQRT_HALF))


def _sigmoid(x):
    return 1.0 / (1.0 + jnp.exp(-x))


def _identity(x):
    return x


def _inproj_body(h_ref, w_ref, o_ref, wb_ref, *, act):
    @pl.when(pl.program_id(1) == 0)
    def _():
        wb_ref[...] = w_ref[...].astype(BF16)

    acc = jnp.dot(h_ref[...], wb_ref[...], preferred_element_type=F32)
    o_ref[...] = act(acc).astype(o_ref.dtype)


def _inproj_call(h, w_in, layer, col0, width, tn, act, name):
    t = h.shape[0]
    assert col0 % tn == 0 and width % tn == 0
    n0 = col0 // tn
    return pl.pallas_call(
        functools.partial(_inproj_body, act=act),
        grid=(width // tn, t // TM_PROJ),
        in_specs=[pl.BlockSpec((TM_PROJ, D_MODEL), lambda n, m: (m, 0)),
                  pl.BlockSpec((None, D_MODEL, tn), lambda n, m: (layer, 0, n0 + n))],
        out_specs=pl.BlockSpec((TM_PROJ, tn), lambda n, m: (m, n)),
        out_shape=jax.ShapeDtypeStruct((t, width), BF16),
        scratch_shapes=[pltpu.VMEM((D_MODEL, tn), BF16)],
        compiler_params=_params(("arbitrary", "arbitrary")),
        name=name,
    )(h, w_in)


def _group_sums(width, group):
    gid = jnp.arange(width, dtype=I32) // group
    s = (gid[:, None] == jnp.arange(LANES, dtype=I32)[None, :]).astype(BF16)
    return s, s.T


def _group_ssq(x, s):
    return jnp.dot((x * x).astype(BF16), s, preferred_element_type=F32)


def _group_bcast(ssq, e, group):
    r = lax.rsqrt(ssq * (1.0 / group) + EPS)
    r_hi = r.astype(BF16)
    r_lo = (r - r_hi.astype(F32)).astype(BF16)
    return (jnp.dot(r_hi, e, preferred_element_type=F32)
            + jnp.dot(r_lo, e, preferred_element_type=F32))


def _mixer_body(a_ref, b_ref, prev_ref, kvc_ref, vgain_ref, ws_ref, bs_ref, qgb_ref, kgb_ref,
                sink_ref, qgc_ref, s64_ref, e64_ref, s128_ref, e128_ref, s256_ref, e256_ref,
                y_ref, sb_ref, *, blocks_per_seq):
    i = pl.program_id(0)
    n = lax.rem(i, blocks_per_seq)
    ones_k = jnp.ones((2 * BLOCK, LANES), BF16)
    row = lax.broadcasted_iota(I32, (CHUNK, CHUNK), 0)
    col = lax.broadcasted_iota(I32, (CHUNK, CHUNK), 1)
    causal = col <= row
    lane_k = lax.broadcasted_iota(I32, (2 * BLOCK, LANES), 1)
    lane_q = lax.broadcasted_iota(I32, (BLOCK, LANES), 1)
    lo_k = lane_k < B_HEAD_DIM
    lo_q = lane_q < B_HEAD_DIM
    qi = lax.broadcasted_iota(I32, (B_REP * BLOCK, 2 * BLOCK), 0) & (BLOCK - 1)
    kj = lax.broadcasted_iota(I32, (B_REP * BLOCK, 2 * BLOCK), 1)
    rel = BLOCK + qi - kj
    valid = (rel >= 0) & (rel < BLOCK) & ((n > 0) | (kj >= BLOCK))

    v_all = a_ref[:, COL_V:COL_V + A_WIDTH].astype(F32)
    q_all = b_ref[:, PB_Q:PB_Q + B_Q_WIDTH].astype(F32)
    k_all = jnp.concatenate([prev_ref[:, 0:B_KV_WIDTH], b_ref[:, PB_K:PB_K + B_KV_WIDTH]],
                            axis=0).astype(F32)
    qc_all = b_ref[:, PB_QC:PB_QC + C_WIDTH].astype(F32)
    ssq_q = _group_ssq(q_all, s64_ref[...])
    ssq_k = _group_ssq(k_all, s64_ref[0:B_KV_WIDTH, :])
    ssq_c = _group_ssq(qc_all, s256_ref[...])
    ssq_v = _group_ssq(v_all, s128_ref[...])
    qn_all = (q_all * _group_bcast(ssq_q, e64_ref[...], B_HEAD_DIM)
              * (qgb_ref[...] * (B_HEAD_DIM ** -0.5)))
    kn_all = k_all * _group_bcast(ssq_k, e64_ref[:, 0:B_KV_WIDTH], B_HEAD_DIM) * kgb_ref[...]
    qcn_all = (qc_all * _group_bcast(ssq_c, e256_ref[...], C_HEAD_DIM)
               * (qgc_ref[...] * (C_HEAD_DIM ** -0.5))).astype(BF16)
    vn_all = (v_all * _group_bcast(ssq_v, e128_ref[...], A_GROUP_DIM)
              * vgain_ref[...]).astype(BF16)
    v_kv = jnp.concatenate([prev_ref[:, B_KV_WIDTH:2 * B_KV_WIDTH],
                            b_ref[:, PB_V:PB_V + B_KV_WIDTH]], axis=0).astype(F32)

    scores_c = [
        lax.dot_general(qcn_all[:, h * C_HEAD_DIM:(h + 1) * C_HEAD_DIM],
                        kvc_ref[:, h * C_HEAD_DIM:(h + 1) * C_HEAD_DIM],
                        (((1,), (1,)), ((), ())), preferred_element_type=F32)
        for h in range(C_HEADS)]
    vdups, sinks = [], []
    for slab in range(B_KV_WIDTH // LANES):
        ks = kn_all[:, slab * LANES:(slab + 1) * LANES]
        vs = v_kv[:, slab * LANES:(slab + 1) * LANES]
        kr = pltpu.roll(ks, B_HEAD_DIM, axis=1)
        vr = pltpu.roll(vs, B_HEAD_DIM, axis=1)
        for par in range(2):
            kv = 2 * slab + par
            kdup = (jnp.where(lo_k, ks, kr) if par == 0 else jnp.where(lo_k, kr, ks)).astype(BF16)
            vdups.append((jnp.where(lo_k, vs, vr) if par == 0
                          else jnp.where(lo_k, vr, vs)).astype(BF16))
            qs, sk = [], []
            for r in range(B_REP):
                h = kv * B_REP + r
                q_slab = qn_all[:, (h // 2) * LANES:(h // 2 + 1) * LANES]
                keep = lo_q if h % 2 == 0 else jnp.logical_not(lo_q)
                qs.append(jnp.where(keep, q_slab, 0.0).astype(BF16))
                sk.append(jnp.full((BLOCK, LANES), sink_ref[h], F32))
            sinks.append(jnp.concatenate(sk, axis=0))
            sc = lax.dot_general(jnp.concatenate(qs, axis=0), kdup, (((1,), (1,)), ((), ())),
                                 preferred_element_type=F32)
            sb_ref[kv] = jnp.where(valid, sc, -jnp.inf)

    probs_c = [jnp.exp(s - jnp.max(s, axis=-1, keepdims=True)).astype(BF16) for s in scores_c]
    mixed_a = [
        jnp.dot(jnp.where(causal, ws_ref[g], 0.0).astype(BF16),
                vn_all[:, g * A_GROUP_DIM:(g + 1) * A_GROUP_DIM],
                preferred_element_type=F32) + bs_ref[g]
        for g in range(A_GROUPS)]
    maxes_b = [jnp.maximum(jnp.max(sb_ref[kv], axis=-1, keepdims=True), sinks[kv])
               for kv in range(B_KV_HEADS)]
    probs_b = [jnp.exp(sb_ref[kv] - jnp.concatenate([maxes_b[kv]] * (2 * BLOCK // LANES), axis=1)
                       ).astype(BF16) for kv in range(B_KV_HEADS)]
    sink_terms = [jnp.exp(sinks[kv] - maxes_b[kv]) for kv in range(B_KV_HEADS)]

    outs_c = [(jnp.dot(p, kvc_ref[:, C_WIDTH + h * C_HEAD_DIM:C_WIDTH + (h + 1) * C_HEAD_DIM],
                       preferred_element_type=F32),
               jnp.dot(p, ones_k, preferred_element_type=F32))
              for h, p in enumerate(probs_c)]
    for g, mixed in enumerate(mixed_a):
        sl = slice(g * A_GROUP_DIM, (g + 1) * A_GROUP_DIM)
        u = a_ref[:, COL_U + g * A_GROUP_DIM:COL_U + (g + 1) * A_GROUP_DIM].astype(F32)
        y_ref[:, sl] = (u * mixed).astype(y_ref.dtype)
    outs_b = [(jnp.dot(p, vdup, preferred_element_type=F32),
               jnp.dot(p, ones_k, preferred_element_type=F32) + st)
              for p, vdup, st in zip(probs_b, vdups, sink_terms)]

    for h, (o, denom) in enumerate(outs_c):
        o = o / jnp.concatenate([denom] * (C_HEAD_DIM // LANES), axis=1)
        y_ref[:, A_WIDTH + B_Q_WIDTH + h * C_HEAD_DIM:
              A_WIDTH + B_Q_WIDTH + (h + 1) * C_HEAD_DIM] = o.astype(y_ref.dtype)
    for kv, (o, denom) in enumerate(outs_b):
        o = o / denom
        for pr in range(B_REP // 2):
            even = o[(2 * pr) * BLOCK:(2 * pr + 1) * BLOCK]
            odd = o[(2 * pr + 1) * BLOCK:(2 * pr + 2) * BLOCK]
            c0 = A_WIDTH + (kv * B_REP // 2 + pr) * LANES
            y_ref[:, c0:c0 + LANES] = jnp.where(lo_q, even, odd).astype(y_ref.dtype)


def _mixer_call(pa, pb, kvc, vgain_row, w_spatial, bs_bcast, qgb_row, kgb_row, sinks, qgc_row,
                layer, seq):
    t = pa.shape[0]
    bps = seq // BLOCK
    kvb = PB_K // (2 * B_KV_WIDTH)
    s64, e64 = _group_sums(B_Q_WIDTH, B_HEAD_DIM)
    s128, e128 = _group_sums(A_WIDTH, A_GROUP_DIM)
    s256, e256 = _group_sums(C_WIDTH, C_HEAD_DIM)

    def const(shape):
        return pl.BlockSpec(shape, lambda i: (0,) * len(shape))

    return pl.pallas_call(
        functools.partial(_mixer_body, blocks_per_seq=bps),
        grid=(t // BLOCK,),
        in_specs=[
            pl.BlockSpec((BLOCK, SEC_A_WIDTH), lambda i: (i, 0)),
            pl.BlockSpec((BLOCK, SEC_B_WIDTH), lambda i: (i, 0)),
            pl.BlockSpec((BLOCK, 2 * B_KV_WIDTH), lambda i: (jnp.maximum(i - 1, 0), kvb)),
            pl.BlockSpec((MEM_LEN, 2 * C_WIDTH), lambda i: (i // bps, 0)),
            const((1, A_WIDTH)),
            pl.BlockSpec((None, A_GROUPS, CHUNK, CHUNK), lambda i: (layer, 0, 0, 0)),
            const((A_GROUPS, CHUNK, LANES)),
            const((1, B_Q_WIDTH)),
            const((1, B_KV_WIDTH)),
            pl.BlockSpec(memory_space=pltpu.SMEM),
            const((1, C_WIDTH)),
            const((B_Q_WIDTH, LANES)), const((LANES, B_Q_WIDTH)),
            const((A_WIDTH, LANES)), const((LANES, A_WIDTH)),
            const((C_WIDTH, LANES)), const((LANES, C_WIDTH)),
        ],
        out_specs=pl.BlockSpec((BLOCK, N_BRANCH * BRANCH_WIDTH), lambda i: (i, 0)),
        out_shape=jax.ShapeDtypeStruct((t, N_BRANCH * BRANCH_WIDTH), BF16),
        scratch_shapes=[pltpu.VMEM((B_KV_HEADS, B_REP * BLOCK, 2 * BLOCK), F32)],
        compiler_params=_params(("arbitrary",)),
        name="mixer",
    )(pa, pb, pb, kvc, vgain_row, w_spatial, bs_bcast, qgb_row, kgb_row, sinks, qgc_row,
      s64, e64, s128, e128, s256, e256)


def _merge_body(y_ref, g0_ref, g1_ref, g2_ref, w_ref, o_ref, wb_ref):
    m = pl.program_id(1)

    @pl.when(m == 0)
    def _():
        wb_ref[...] = w_ref[...].astype(BF16)

    acc = None
    for b, g_ref in enumerate((g0_ref, g1_ref, g2_ref)):
        wide = jnp.dot(y_ref[:, b * BRANCH_WIDTH:(b + 1) * BRANCH_WIDTH], wb_ref[b],
                       preferred_element_type=F32)
        term = g_ref[...].astype(F32) * wide
        acc = term if acc is None else acc + term
    o_ref[...] = acc.astype(o_ref.dtype)


def _merge_call(y, gates, w_branch, layer):
    t = y.shape[0]
    per = D_MODEL // TN_MERGE

    def gate_spec(b):
        return pl.BlockSpec((TM_MERGE, TN_MERGE), lambda n, m: (m, b * per + n))

    return pl.pallas_call(
        _merge_body,
        grid=(D_MODEL // TN_MERGE, t // TM_MERGE),
        in_specs=[pl.BlockSpec((TM_MERGE, N_BRANCH * BRANCH_WIDTH), lambda n, m: (m, 0)),
                  gate_spec(0), gate_spec(1), gate_spec(2),
                  pl.BlockSpec((None, N_BRANCH, BRANCH_WIDTH, TN_MERGE),
                               lambda n, m: (layer, 0, 0, n))],
        out_specs=pl.BlockSpec((TM_MERGE, TN_MERGE), lambda n, m: (m, n)),
        out_shape=jax.ShapeDtypeStruct((t, D_MODEL), BF16),
        scratch_shapes=[pltpu.VMEM((N_BRANCH, BRANCH_WIDTH, TN_MERGE), BF16)],
        compiler_params=_params(("arbitrary", "arbitrary")),
        name="merge",
    )(y, gates, gates, gates, w_branch)


def _outproj_body(mg_ref, x_ref, w_ref, gn_ref, wr_ref, br_ref, x1_ref, lg_ref,
                  wr2_ref):
    i = pl.program_id(0)

    @pl.when(i == 0)
    def _():
        wr = wr_ref[...]
        hi = wr.astype(BF16)
        wr2_ref[:, 0:LANES] = hi
        wr2_ref[:, LANES:2 * LANES] = (wr - hi.astype(F32)).astype(BF16)

    x1 = x_ref[...] + jnp.dot(mg_ref[...], w_ref[...], preferred_element_type=F32)
    x1_ref[...] = x1
    h2 = _rms(x1, gn_ref[...])
    hi = h2.astype(BF16)
    lo = (h2 - hi.astype(F32)).astype(BF16)
    hi_w = jnp.dot(hi, wr2_ref[...], preferred_element_type=F32)
    lo_w = jnp.dot(lo, wr2_ref[...], preferred_element_type=F32)
    lg_ref[...] = hi_w[:, 0:LANES] + lo_w[:, 0:LANES] + hi_w[:, LANES:2 * LANES] + br_ref[...]


def _outproj_call(merged, x, w_out_bf16, gn_row, w_router, b_router):
    t = x.shape[0]
    return pl.pallas_call(
        _outproj_body,
        grid=(t // TM_OUT,),
        in_specs=[pl.BlockSpec((TM_OUT, D_MODEL), lambda i: (i, 0)),
                  pl.BlockSpec((TM_OUT, D_MODEL), lambda i: (i, 0)),
                  pl.BlockSpec((D_MODEL, D_MODEL), lambda i: (0, 0),
                               pipeline_mode=pl.Buffered(1)),
                  pl.BlockSpec((1, D_MODEL), lambda i: (0, 0)),
                  pl.BlockSpec((D_MODEL, LANES), lambda i: (0, 0)),
                  pl.BlockSpec((1, LANES), lambda i: (0, 0))],
        out_specs=[pl.BlockSpec((TM_OUT, D_MODEL), lambda i: (i, 0)),
                   pl.BlockSpec((TM_OUT, LANES), lambda i: (i, 0))],
        out_shape=[jax.ShapeDtypeStruct((t, D_MODEL), F32),
                   jax.ShapeDtypeStruct((t, LANES), F32)],
        scratch_shapes=[pltpu.VMEM((D_MODEL, 2 * LANES), BF16)],
        compiler_params=_params(("arbitrary",)),
        name="outproj",
    )(merged, x, w_out_bf16, gn_row, w_router, b_router)


def _route_body(lg_ref, ri_ref, rw_ref, cnt_ref, carry_ref):
    i = pl.program_id(0)

    @pl.when(i == 0)
    def _():
        carry_ref[...] = jnp.zeros_like(carry_ref)

    lg = lg_ref[...]
    tm = lg.shape[0]
    lane = lax.broadcasted_iota(I32, lg.shape, 1)
    neg = -jnp.inf
    big = jnp.int32(LANES)

    is_g = lane < N_GROUPS
    gl = jnp.where(is_g, lg, neg)
    gmax = jnp.max(gl, axis=-1, keepdims=True)
    gidx = jnp.min(jnp.where(gl == gmax, lane, big), axis=-1, keepdims=True)
    p_g = 1.0 / jnp.sum(jnp.where(is_g, jnp.exp(lg - gmax), 0.0), axis=-1, keepdims=True)

    e_lane = lane - N_GROUPS
    in_group = (e_lane >= 0) & (e_lane < N_EXPERTS) & ((e_lane >> 2) == gidx)
    el = jnp.where(in_group, lg, neg)
    v0 = jnp.max(el, axis=-1, keepdims=True)
    i0 = jnp.min(jnp.where(el == v0, lane, big), axis=-1, keepdims=True)
    el1 = jnp.where(lane == i0, neg, el)
    v1 = jnp.max(el1, axis=-1, keepdims=True)
    i1 = jnp.min(jnp.where(el1 == v1, lane, big), axis=-1, keepdims=True)
    e0 = i0 - N_GROUPS
    e1 = i1 - N_GROUPS
    ex = jnp.exp(v1 - v0)
    w0 = p_g / (1.0 + ex)
    w1 = p_g * ex / (1.0 + ex)

    oh0 = lane == e0
    oh1 = lane == e1
    oh = jnp.where(oh0 | oh1, 1.0, 0.0)
    r_i = lax.broadcasted_iota(I32, (tm, tm), 0)
    c_i = lax.broadcasted_iota(I32, (tm, tm), 1)
    lower = jnp.where(c_i < r_i, 1.0, 0.0).astype(BF16)
    before = jnp.dot(lower, oh.astype(BF16), preferred_element_type=F32) + carry_ref[...]
    rank0 = jnp.sum(jnp.where(oh0, before, 0.0), axis=-1, keepdims=True).astype(I32)
    rank1 = jnp.sum(jnp.where(oh1, before, 0.0), axis=-1, keepdims=True).astype(I32)
    carry_ref[...] = carry_ref[...] + jnp.sum(oh, axis=0, keepdims=True)

    zero_i = jnp.zeros_like(lane)
    ri_ref[...] = jnp.where(lane == 0, e0, jnp.where(lane == 1, e1,
                            jnp.where(lane == 2, rank0, jnp.where(lane == 3, rank1, zero_i))))
    rw_ref[...] = jnp.where(lane == 0, w0, jnp.where(lane == 1, w1, 0.0))
    cnt_ref[...] = jnp.broadcast_to(carry_ref[...], cnt_ref.shape).astype(I32)


def _route_call(logits):
    t = logits.shape[0]
    return pl.pallas_call(
        _route_body,
        grid=(t // TM_ROUTE,),
        in_specs=[pl.BlockSpec((TM_ROUTE, LANES), lambda i: (i, 0))],
        out_specs=[pl.BlockSpec((TM_ROUTE, LANES), lambda i: (i, 0)),
                   pl.BlockSpec((TM_ROUTE, LANES), lambda i: (i, 0)),
                   pl.BlockSpec((8, LANES), lambda i: (0, 0))],
        out_shape=[jax.ShapeDtypeStruct((t, LANES), I32),
                   jax.ShapeDtypeStruct((t, LANES), F32),
                   jax.ShapeDtypeStruct((8, LANES), I32)],
        scratch_shapes=[pltpu.VMEM((1, LANES), F32)],
        compiler_params=_params(("arbitrary",)),
        name="route",
    )(logits)


def _plan(route_i, counts, n_tiles):
    c = counts[0, :N_EXPERTS]
    tiles = (c + TM_EXPERT - 1) // TM_EXPERT
    ctiles = jnp.cumsum(tiles)
    start = (ctiles - tiles) * TM_EXPERT
    eid = jnp.arange(N_EXPERTS, dtype=I32)

    def slot(e, r):
        return jnp.sum(jnp.where(e[:, None] == eid[None, :], start[None, :], 0), axis=1) + r

    pos0 = slot(route_i[:, 0], route_i[:, 2]).astype(I32)
    pos1 = slot(route_i[:, 1], route_i[:, 3]).astype(I32)
    n_used = ctiles[-1]
    tid = jnp.minimum(jnp.arange(n_tiles, dtype=I32), n_used - 1)
    tile_expert = jnp.sum(tid[:, None] >= ctiles[None, :], axis=1).astype(I32)
    return pos0, pos1, tile_expert, n_used.reshape(1).astype(I32)


def _invert_body(p0_ref, p1_ref, src_ref):
    i = pl.program_id(0)

    @pl.when(i == 0)
    def _():
        def fill(r, c):
            src_ref[r] = 0
            return c
        lax.fori_loop(0, src_ref.shape[0], fill, 0, unroll=8)

    base = i * TK_INVERT

    def scatter(t, c):
        src_ref[p0_ref[t]] = base + t
        src_ref[p1_ref[t]] = base + t
        return c
    lax.fori_loop(0, TK_INVERT, scatter, 0, unroll=8)


def _invert_call(pos0, pos1, n_rows):
    t = pos0.shape[0]
    smem = pl.BlockSpec((TK_INVERT,), lambda i: (i,), memory_space=pltpu.SMEM)
    return pl.pallas_call(
        _invert_body,
        grid=(t // TK_INVERT,),
        in_specs=[smem, smem],
        out_specs=pl.BlockSpec(memory_space=pltpu.SMEM),
        out_shape=jax.ShapeDtypeStruct((n_rows,), I32),
        compiler_params=_params(("arbitrary",)),
        name="invert",
    )(pos0, pos1)


def _expert_body(te_ref, nu_ref, srcc_ref, srcn_ref, x_hbm, gn_ref, wg_ref, wu_ref, wd_ref,
                 y_ref, xbuf_ref, wgb_ref, wub_ref, wdb_ref, sems):
    i = pl.program_id(0)
    n_used = nu_ref[0]
    slot = lax.rem(i, 2)

    def row_copy(src_ref, r, s):
        return pltpu.make_async_copy(x_hbm.at[pl.ds(src_ref[r], 1)],
                                     xbuf_ref.at[s, pl.ds(r, 1)], sems.at[s])

    def issue(src_ref, s):
        for r in range(TM_EXPERT):
            row_copy(src_ref, r, s).start(priority=r % 2)

    @pl.when(i == 0)
    def _():
        issue(srcc_ref, 0)

    @pl.when(i < n_used)
    def _():
        prev = te_ref[jnp.maximum(i - 1, 0)]

        @pl.when(jnp.logical_or(i == 0, te_ref[i] != prev))
        def _():
            wgb_ref[...] = wg_ref[...].astype(BF16)
            wub_ref[...] = wu_ref[...].astype(BF16)
            wdb_ref[...] = wd_ref[...].astype(BF16)

        def drain(r, c):
            row_copy(srcc_ref, r, slot).wait()
            return c
        lax.fori_loop(0, TM_EXPERT, drain, 0, unroll=8)

    def compute(s, prefetch):
        h = _rms(xbuf_ref[s], gn_ref[...]).astype(BF16)
        if prefetch:
            issue(srcn_ref, 1 - s)
        g = jnp.dot(h, wgb_ref[...], preferred_element_type=F32)
        u = jnp.dot(h, wub_ref[...], preferred_element_type=F32)
        hid = (g * _sigmoid(g) * u).astype(BF16)
        y_ref[...] = jnp.dot(hid, wdb_ref[...], preferred_element_type=F32)

    for s in range(2):
        @pl.when(jnp.logical_and(i + 1 < n_used, slot == s))
        def _():
            compute(s, True)

        @pl.when(jnp.logical_and(i + 1 == n_used, slot == s))
        def _():
            compute(s, False)

    @pl.when(i >= n_used)
    def _():
        y_ref[...] = jnp.zeros_like(y_ref)


def _expert_call(x1, src, tile_expert, n_used, gn_row, w_gate, w_up, w_down, layer):
    n_rows = src.shape[0]
    n_tiles = n_rows // TM_EXPERT

    def w_map(i, te, nu):
        return (layer, te[i], 0, 0)

    grid_spec = pltpu.PrefetchScalarGridSpec(
        num_scalar_prefetch=2,
        grid=(n_tiles,),
        in_specs=[pl.BlockSpec((TM_EXPERT,), lambda i, te, nu: (i,), memory_space=pltpu.SMEM),
                  pl.BlockSpec((TM_EXPERT,), lambda i, te, nu: (jnp.minimum(i + 1, n_tiles - 1),),
                               memory_space=pltpu.SMEM),
                  pl.BlockSpec(memory_space=pl.ANY),
                  pl.BlockSpec((1, D_MODEL), lambda i, te, nu: (0, 0)),
                  pl.BlockSpec((None, None, D_MODEL, D_FF_EXPERT), w_map),
                  pl.BlockSpec((None, None, D_MODEL, D_FF_EXPERT), w_map),
                  pl.BlockSpec((None, None, D_FF_EXPERT, D_MODEL), w_map)],
        out_specs=pl.BlockSpec((TM_EXPERT, D_MODEL), lambda i, te, nu: (i, 0)),
        scratch_shapes=[pltpu.VMEM((2, TM_EXPERT, D_MODEL), F32),
                        pltpu.VMEM((D_MODEL, D_FF_EXPERT), BF16),
                        pltpu.VMEM((D_MODEL, D_FF_EXPERT), BF16),
                        pltpu.VMEM((D_FF_EXPERT, D_MODEL), BF16),
                        pltpu.SemaphoreType.DMA((2,))],
    )
    return pl.pallas_call(
        _expert_body,
        grid_spec=grid_spec,
        out_shape=jax.ShapeDtypeStruct((n_rows, D_MODEL), F32),
        compiler_params=_params(("arbitrary",)),
        name="experts",
    )(tile_expert, n_used, src, src, x1, gn_row, w_gate, w_up, w_down)


def _combine_body(p0c_ref, p1c_ref, p0n_ref, p1n_ref, x1_ref, rw_ref, gn_ref, ys_hbm,
                  x2_ref, h_ref, buf_ref, sems, *, n_steps):
    i = pl.program_id(0)
    slot = lax.rem(i, 2)

    def row_copy(p_ref, k, t, s):
        return pltpu.make_async_copy(ys_hbm.at[pl.ds(p_ref[t], 1)],
                                     buf_ref.at[s, k, pl.ds(t, 1)], sems.at[s])

    def issue(p0_ref, p1_ref, s):
        for t in range(TM_COMBINE):
            row_copy(p0_ref, 0, t, s).start(priority=0)
            row_copy(p1_ref, 1, t, s).start(priority=1)

    @pl.when(i == 0)
    def _():
        issue(p0c_ref, p1c_ref, 0)

    def drain(t, c):
        row_copy(p0c_ref, 0, t, slot).wait()
        row_copy(p1c_ref, 1, t, slot).wait()
        return c
    lax.fori_loop(0, TM_COMBINE, drain, 0, unroll=8)

    def compute(s, prefetch):
        rw = rw_ref[...]
        x2 = x1_ref[...] + rw[:, 0:1] * buf_ref[s, 0] + rw[:, 1:2] * buf_ref[s, 1]
        if prefetch:
            issue(p0n_ref, p1n_ref, 1 - s)
        x2_ref[...] = x2
        h_ref[...] = _rms(x2, gn_ref[...]).astype(h_ref.dtype)

    for s in range(2):
        @pl.when(jnp.logical_and(i + 1 < n_steps, slot == s))
        def _():
            compute(s, True)

        @pl.when(jnp.logical_and(i + 1 == n_steps, slot == s))
        def _():
            compute(s, False)


def _combine_call(x1, route_w, pos0, pos1, ys, gn_row):
    t = x1.shape[0]
    n_steps = t // TM_COMBINE

    def nxt(i):
        return (jnp.minimum(i + 1, n_steps - 1),)

    smem = functools.partial(pl.BlockSpec, (TM_COMBINE,), memory_space=pltpu.SMEM)
    return pl.pallas_call(
        functools.partial(_combine_body, n_steps=n_steps),
        grid=(n_steps,),
        in_specs=[smem(lambda i: (i,)), smem(lambda i: (i,)), smem(nxt), smem(nxt),
                  pl.BlockSpec((TM_COMBINE, D_MODEL), lambda i: (i, 0)),
                  pl.BlockSpec((TM_COMBINE, LANES), lambda i: (i, 0)),
                  pl.BlockSpec((1, D_MODEL), lambda i: (0, 0)),
                  pl.BlockSpec(memory_space=pl.ANY)],
        out_specs=[pl.BlockSpec((TM_COMBINE, D_MODEL), lambda i: (i, 0)),
                   pl.BlockSpec((TM_COMBINE, D_MODEL), lambda i: (i, 0))],
        out_shape=[jax.ShapeDtypeStruct((t, D_MODEL), F32),
                   jax.ShapeDtypeStruct((t, D_MODEL), BF16)],
        scratch_shapes=[pltpu.VMEM((2, TOP_K, TM_COMBINE, D_MODEL), F32),
                        pltpu.SemaphoreType.DMA((2,))],
        compiler_params=_params(("arbitrary",)),
        name="combine",
    )(pos0, pos1, pos0, pos1, x1, route_w, gn_row, ys)


def kernel(x, mem, norm_mix, norm_mem, norm_ffn, w_in, v_gain, w_spatial, b_spatial,
           q_gain_b, k_gain_b, sinks, q_gain_c, k_gain_c, w_mem_kv, w_branch, w_out,
           w_router_group, b_router_group, w_router_expert, b_router_expert,
           w_gate_e, w_up_e, w_down_e):
    b, s, d = x.shape
    t = b * s
    assert d == D_MODEL and s % BLOCK == 0 and mem.shape[1] == MEM_LEN
    assert t % TK_INVERT == 0 and t % TM_PROJ == 0
    n_tiles = (t * TOP_K) // TM_EXPERT + N_EXPERTS
    n_rows = n_tiles * TM_EXPERT

    xf = x.reshape(t, d)
    mem2d = mem.reshape(b * MEM_LEN, d)
    pad = LANES - N_GROUPS - N_EXPERTS

    h = _norm_call(xf, norm_mix[0][None])
    for layer in range(DEPTH):
        kvc = _memkv_call(mem2d, norm_mem[layer][None], w_mem_kv, k_gain_c[layer][None], layer)
        pa = _inproj_call(h, w_in, layer, 0, SEC_A_WIDTH, TN_PROJ_A, _gelu, "inproj_a")
        w_sec_b = w_in[layer][None, :, COL_QB:COL_GATE]
        pb = _inproj_call(h, w_sec_b, 0, 0, SEC_B_WIDTH, TN_PROJ_B, _identity, "inproj_b")
        gates = _inproj_call(h, w_in, layer, COL_GATE, SEC_G_WIDTH, TN_PROJ_G, _sigmoid, "inproj_g")
        bs_bcast = jnp.broadcast_to(b_spatial[layer][:, :, None], (A_GROUPS, CHUNK, LANES))
        y = _mixer_call(pa, pb, kvc, v_gain[layer].reshape(1, A_WIDTH), w_spatial, bs_bcast,
                        jnp.tile(q_gain_b[layer], B_HEADS)[None],
                        jnp.tile(k_gain_b[layer], B_KV_HEADS)[None],
                        sinks[layer], jnp.tile(q_gain_c[layer], C_HEADS)[None], layer, s)
        merged = _merge_call(y, gates, w_branch, layer)
        w_router = jnp.pad(jnp.concatenate([w_router_group[layer], w_router_expert[layer]], axis=1),
                           ((0, 0), (0, pad)))
        b_router = jnp.pad(jnp.concatenate([b_router_group[layer], b_router_expert[layer]]),
                           (0, pad))[None]
        x1, logits = _outproj_call(merged, xf, w_out[layer].astype(BF16),
                                   norm_ffn[layer][None], w_router, b_router)
        route_i, route_w, counts = _route_call(logits)
        pos0, pos1, tile_expert, n_used = _plan(route_i, counts, n_tiles)
        src = _invert_call(pos0, pos1, n_rows)
        ys = _expert_call(x1, src, tile_expert, n_used, norm_ffn[layer][None], w_gate_e, w_up_e,
                          w_down_e, layer)
        gn_next = norm_mix[layer + 1] if layer + 1 < DEPTH else norm_mix[layer]
        xf, h = _combine_call(x1, route_w, pos0, pos1, ys, gn_next[None])
    return xf.reshape(b, s, d)
```

```python
import functools

import jax
import jax.numpy as jnp
from jax import lax
from jax.experimental import pallas as pl
from jax.experimental.pallas import tpu as pltpu

F32 = jnp.float32
BF16 = jnp.bfloat16
I32 = jnp.int32

D_MODEL = 2048
DEPTH = 2
MEM_LEN = 256
EPS = 1e-6
BRANCH_WIDTH = 1024
N_BRANCH = 3

CHUNK = 128
A_GROUPS = 8
A_GROUP_DIM = 128
A_WIDTH = A_GROUPS * A_GROUP_DIM

BLOCK = 128
B_HEADS = 16
B_KV_HEADS = 4
B_REP = B_HEADS // B_KV_HEADS
B_HEAD_DIM = 64
B_Q_WIDTH = B_HEADS * B_HEAD_DIM
B_KV_WIDTH = B_KV_HEADS * B_HEAD_DIM

C_HEADS = 4
C_HEAD_DIM = 256
C_WIDTH = C_HEADS * C_HEAD_DIM

COL_U = 0
COL_V = A_WIDTH
COL_QB = 2 * A_WIDTH
COL_KB = COL_QB + B_Q_WIDTH
COL_VB = COL_KB + B_KV_WIDTH
COL_QC = COL_VB + B_KV_WIDTH
COL_GATE = COL_QC + C_WIDTH
IN_WIDTH = COL_GATE + N_BRANCH * D_MODEL
SEC_A_WIDTH = COL_QB
SEC_B_WIDTH = COL_GATE - COL_QB
SEC_G_WIDTH = N_BRANCH * D_MODEL
PB_Q = 0
PB_K = COL_KB - COL_QB
PB_V = COL_VB - COL_QB
PB_QC = COL_QC - COL_QB

N_GROUPS = 4
EXPERTS_PER_GROUP = 4
N_EXPERTS = N_GROUPS * EXPERTS_PER_GROUP
TOP_K = 2
D_FF_EXPERT = 512

LANES = 128
VMEM_LIMIT = 56 * 1024 * 1024
SQRT_HALF = 0.7071067811865476

TM_NORM = 512
TM_PROJ = 1024
TN_PROJ_A, TN_PROJ_B, TN_PROJ_G = 1024, 512, 1536
TM_MERGE, TN_MERGE = 512, 1024
TM_OUT = 512
TM_ROUTE = 512
TK_INVERT = 2048
TM_EXPERT = 256
TM_COMBINE = 256


def _params(sem):
    return pltpu.CompilerParams(dimension_semantics=sem, vmem_limit_bytes=VMEM_LIMIT)


def _rms(x, gain):
    ms = jnp.mean(x * x, axis=-1, keepdims=True)
    return x * lax.rsqrt(ms + EPS) * gain


def _norm_body(x_ref, g_ref, o_ref):
    o_ref[...] = _rms(x_ref[...], g_ref[...]).astype(o_ref.dtype)


def _norm_call(x, gain_row):
    t = x.shape[0]
    return pl.pallas_call(
        _norm_body,
        grid=(t // TM_NORM,),
        in_specs=[pl.BlockSpec((TM_NORM, D_MODEL), lambda i: (i, 0)),
                  pl.BlockSpec((1, D_MODEL), lambda i: (0, 0))],
        out_specs=pl.BlockSpec((TM_NORM, D_MODEL), lambda i: (i, 0)),
        out_shape=jax.ShapeDtypeStruct((t, D_MODEL), BF16),
        compiler_params=_params(("arbitrary",)),
        name="norm",
    )(x, gain_row)


TN_MEMKV = 512


def _memkv_body(mem_ref, gm_ref, w_ref, gk_ref, o_ref):
    n = pl.program_id(0)
    h = _rms(mem_ref[...], gm_ref[...]).astype(BF16)
    acc = jnp.dot(h, w_ref[...].astype(BF16), preferred_element_type=F32)

    @pl.when(n < C_WIDTH // TN_MEMKV)
    def _():
        for j in range(TN_MEMKV // C_HEAD_DIM):
            sl = slice(j * C_HEAD_DIM, (j + 1) * C_HEAD_DIM)
            o_ref[:, sl] = _rms(acc[:, sl], gk_ref[...]).astype(o_ref.dtype)

    @pl.when(n >= C_WIDTH // TN_MEMKV)
    def _():
        o_ref[...] = acc.astype(o_ref.dtype)


def _memkv_call(mem2d, gm_row, w_mem_kv, gk_row, layer):
    rows = mem2d.shape[0]
    return pl.pallas_call(
        _memkv_body,
        grid=(2 * C_WIDTH // TN_MEMKV,),
        in_specs=[pl.BlockSpec((rows, D_MODEL), lambda n: (0, 0)),
                  pl.BlockSpec((1, D_MODEL), lambda n: (0, 0)),
                  pl.BlockSpec((None, D_MODEL, TN_MEMKV), lambda n: (layer, 0, n)),
                  pl.BlockSpec((1, C_HEAD_DIM), lambda n: (0, 0))],
        out_specs=pl.BlockSpec((rows, TN_MEMKV), lambda n: (0, n)),
        out_shape=jax.ShapeDtypeStruct((rows, 2 * C_WIDTH), BF16),
        compiler_params=_params(("arbitrary",)),
        name="memkv",
    )(mem2d, gm_row, w_mem_kv, gk_row)


def _gelu(x):
    return 0.5 * x * (1.0 + lax.erf(x * SQRT_HALF))


def _sigmoid(x):
    return 1.0 / (1.0 + jnp.exp(-x))


def _identity(x):
    return x


def _inproj_body(h_ref, w_ref, o_ref, wb_ref, *, act):
    @pl.when(pl.program_id(1) == 0)
    def _():
        wb_ref[...] = w_ref[...].astype(BF16)

    acc = jnp.dot(h_ref[...], wb_ref[...], preferred_element_type=F32)
    o_ref[...] = act(acc).astype(o_ref.dtype)


def _inproj_call(h, w_in, layer, col0, width, tn, act, name):
    t = h.shape[0]
    assert col0 % tn == 0 and width % tn == 0
    n0 = col0 // tn
    return pl.pallas_call(
        functools.partial(_inproj_body, act=act),
        grid=(width // tn, t // TM_PROJ),
        in_specs=[pl.BlockSpec((TM_PROJ, D_MODEL), lambda n, m: (m, 0)),
                  pl.BlockSpec((None, D_MODEL, tn), lambda n, m: (layer, 0, n0 + n))],
        out_specs=pl.BlockSpec((TM_PROJ, tn), lambda n, m: (m, n)),
        out_shape=jax.ShapeDtypeStruct((t, width), BF16),
        scratch_shapes=[pltpu.VMEM((D_MODEL, tn), BF16)],
        compiler_params=_params(("arbitrary", "arbitrary")),
        name=name,
    )(h, w_in)


def _group_sums(width, group):
    gid = jnp.arange(width, dtype=I32) // group
    s = (gid[:, None] == jnp.arange(LANES, dtype=I32)[None, :]).astype(BF16)
    return s, s.T


def _group_ssq(x, s):
    return jnp.dot((x * x).astype(BF16), s, preferred_element_type=F32)


def _group_bcast(ssq, e, group):
    r = lax.rsqrt(ssq * (1.0 / group) + EPS)
    r_hi = r.astype(BF16)
    r_lo = (r - r_hi.astype(F32)).astype(BF16)
    return (jnp.dot(r_hi, e, preferred_element_type=F32)
            + jnp.dot(r_lo, e, preferred_element_type=F32))


def _mixer_body(a_ref, b_ref, prev_ref, kvc_ref, vgain_ref, ws_ref, bs_ref, qgb_ref, kgb_ref,
                sink_ref, qgc_ref, s64_ref, e64_ref, s128_ref, e128_ref, s256_ref, e256_ref,
                y_ref, sb_ref, *, blocks_per_seq):
    i = pl.program_id(0)
    n = lax.rem(i, blocks_per_seq)
    ones_k = jnp.ones((2 * BLOCK, LANES), BF16)
    row = lax.broadcasted_iota(I32, (CHUNK, CHUNK), 0)
    col = lax.broadcasted_iota(I32, (CHUNK, CHUNK), 1)
    causal = col <= row
    lane_k = lax.broadcasted_iota(I32, (2 * BLOCK, LANES), 1)
    lane_q = lax.broadcasted_iota(I32, (BLOCK, LANES), 1)
    lo_k = lane_k < B_HEAD_DIM
    lo_q = lane_q < B_HEAD_DIM
    qi = lax.broadcasted_iota(I32, (B_REP * BLOCK, 2 * BLOCK), 0) & (BLOCK - 1)
    kj = lax.broadcasted_iota(I32, (B_REP * BLOCK, 2 * BLOCK), 1)
    rel = BLOCK + qi - kj
    valid = (rel >= 0) & (rel < BLOCK) & ((n > 0) | (kj >= BLOCK))

    v_all = a_ref[:, COL_V:COL_V + A_WIDTH].astype(F32)
    q_all = b_ref[:, PB_Q:PB_Q + B_Q_WIDTH].astype(F32)
    k_all = jnp.concatenate([prev_ref[:, 0:B_KV_WIDTH], b_ref[:, PB_K:PB_K + B_KV_WIDTH]],
                            axis=0).astype(F32)
    qc_all = b_ref[:, PB_QC:PB_QC + C_WIDTH].astype(F32)
    ssq_q = _group_ssq(q_all, s64_ref[...])
    ssq_k = _group_ssq(k_all, s64_ref[0:B_KV_WIDTH, :])
    ssq_c = _group_ssq(qc_all, s256_ref[...])
    ssq_v = _group_ssq(v_all, s128_ref[...])
    qn_all = (q_all * _group_bcast(ssq_q, e64_ref[...], B_HEAD_DIM)
              * (qgb_ref[...] * (B_HEAD_DIM ** -0.5)))
    kn_all = k_all * _group_bcast(ssq_k, e64_ref[:, 0:B_KV_WIDTH], B_HEAD_DIM) * kgb_ref[...]
    qcn_all = (qc_all * _group_bcast(ssq_c, e256_ref[...], C_HEAD_DIM)
               * (qgc_ref[...] * (C_HEAD_DIM ** -0.5))).astype(BF16)
    vn_all = (v_all * _group_bcast(ssq_v, e128_ref[...], A_GROUP_DIM)
              * vgain_ref[...]).astype(BF16)
    v_kv = jnp.concatenate([prev_ref[:, B_KV_WIDTH:2 * B_KV_WIDTH],
                            b_ref[:, PB_V:PB_V + B_KV_WIDTH]], axis=0).astype(F32)

    scores_c = [
        lax.dot_general(qcn_all[:, h * C_HEAD_DIM:(h + 1) * C_HEAD_DIM],
                        kvc_ref[:, h * C_HEAD_DIM:(h + 1) * C_HEAD_DIM],
                        (((1,), (1,)), ((), ())), preferred_element_type=F32)
        for h in range(C_HEADS)]
    vdups, sinks = [], []
    for slab in range(B_KV_WIDTH // LANES):
        ks = kn_all[:, slab * LANES:(slab + 1) * LANES]
        vs = v_kv[:, slab * LANES:(slab + 1) * LANES]
        kr = pltpu.roll(ks, B_HEAD_DIM, axis=1)
        vr = pltpu.roll(vs, B_HEAD_DIM, axis=1)
        for par in range(2):
            kv = 2 * slab + par
            kdup = (jnp.where(lo_k, ks, kr) if par == 0 else jnp.where(lo_k, kr, ks)).astype(BF16)
            vdups.append((jnp.where(lo_k, vs, vr) if par == 0
                          else jnp.where(lo_k, vr, vs)).astype(BF16))
            qs, sk = [], []
            for r in range(B_REP):
                h = kv * B_REP + r
                q_slab = qn_all[:, (h // 2) * LANES:(h // 2 + 1) * LANES]
                keep = lo_q if h % 2 == 0 else jnp.logical_not(lo_q)
                qs.append(jnp.where(keep, q_slab, 0.0).astype(BF16))
                sk.append(jnp.full((BLOCK, LANES), sink_ref[h], F32))
            sinks.append(jnp.concatenate(sk, axis=0))
            sc = lax.dot_general(jnp.concatenate(qs, axis=0), kdup, (((1,), (1,)), ((), ())),
                                 preferred_element_type=F32)
            sb_ref[kv] = jnp.where(valid, sc, -jnp.inf)

    probs_c = [jnp.exp(s - jnp.max(s, axis=-1, keepdims=True)).astype(BF16) for s in scores_c]
    mixed_a = [
        jnp.dot(jnp.where(causal, ws_ref[g], 0.0).astype(BF16),
                vn_all[:, g * A_GROUP_DIM:(g + 1) * A_GROUP_DIM],
                preferred_element_type=F32) + bs_ref[g]
        for g in range(A_GROUPS)]
    maxes_b = [jnp.maximum(jnp.max(sb_ref[kv], axis=-1, keepdims=True), sinks[kv])
               for kv in range(B_KV_HEADS)]
    probs_b = [jnp.exp(sb_ref[kv] - jnp.concatenate([maxes_b[kv]] * (2 * BLOCK // LANES), axis=1)
                       ).astype(BF16) for kv in range(B_KV_HEADS)]
    sink_terms = [jnp.exp(sinks[kv] - maxes_b[kv]) for kv in range(B_KV_HEADS)]

    outs_c = [(jnp.dot(p, kvc_ref[:, C_WIDTH + h * C_HEAD_DIM:C_WIDTH + (h + 1) * C_HEAD_DIM],
                       preferred_element_type=F32),
               jnp.dot(p, ones_k, preferred_element_type=F32))
              for h, p in enumerate(probs_c)]
    for g, mixed in enumerate(mixed_a):
        sl = slice(g * A_GROUP_DIM, (g + 1) * A_GROUP_DIM)
        u = a_ref[:, COL_U + g * A_GROUP_DIM:COL_U + (g + 1) * A_GROUP_DIM].astype(F32)
        y_ref[:, sl] = (u * mixed).astype(y_ref.dtype)
    outs_b = [(jnp.dot(p, vdup, preferred_element_type=F32),
               jnp.dot(p, ones_k, preferred_element_type=F32) + st)
              for p, vdup, st in zip(probs_b, vdups, sink_terms)]

    for h, (o, denom) in enumerate(outs_c):
        o = o / jnp.concatenate([denom] * (C_HEAD_DIM // LANES), axis=1)
        y_ref[:, A_WIDTH + B_Q_WIDTH + h * C_HEAD_DIM:
              A_WIDTH + B_Q_WIDTH + (h + 1) * C_HEAD_DIM] = o.astype(y_ref.dtype)
    for kv, (o, denom) in enumerate(outs_b):
        o = o / denom
        for pr in range(B_REP // 2):
            even = o[(2 * pr) * BLOCK:(2 * pr + 1) * BLOCK]
            odd = o[(2 * pr + 1) * BLOCK:(2 * pr + 2) * BLOCK]
            c0 = A_WIDTH + (kv * B_REP // 2 + pr) * LANES
            y_ref[:, c0:c0 + LANES] = jnp.where(lo_q, even, odd).astype(y_ref.dtype)


def _mixer_call(pa, pb, kvc, vgain_row, w_spatial, bs_bcast, qgb_row, kgb_row, sinks, qgc_row,
                layer, seq):
    t = pa.shape[0]
    bps = seq // BLOCK
    kvb = PB_K // (2 * B_KV_WIDTH)
    s64, e64 = _group_sums(B_Q_WIDTH, B_HEAD_DIM)
    s128, e128 = _group_sums(A_WIDTH, A_GROUP_DIM)
    s256, e256 = _group_sums(C_WIDTH, C_HEAD_DIM)

    def const(shape):
        return pl.BlockSpec(shape, lambda i: (0,) * len(shape))

    return pl.pallas_call(
        functools.partial(_mixer_body, blocks_per_seq=bps),
        grid=(t // BLOCK,),
        in_specs=[
            pl.BlockSpec((BLOCK, SEC_A_WIDTH), lambda i: (i, 0)),
            pl.BlockSpec((BLOCK, SEC_B_WIDTH), lambda i: (i, 0)),
            pl.BlockSpec((BLOCK, 2 * B_KV_WIDTH), lambda i: (jnp.maximum(i - 1, 0), kvb)),
            pl.BlockSpec((MEM_LEN, 2 * C_WIDTH), lambda i: (i // bps, 0)),
            const((1, A_WIDTH)),
            pl.BlockSpec((None, A_GROUPS, CHUNK, CHUNK), lambda i: (layer, 0, 0, 0)),
            const((A_GROUPS, CHUNK, LANES)),
            const((1, B_Q_WIDTH)),
            const((1, B_KV_WIDTH)),
            pl.BlockSpec(memory_space=pltpu.SMEM),
            const((1, C_WIDTH)),
            const((B_Q_WIDTH, LANES)), const((LANES, B_Q_WIDTH)),
            const((A_WIDTH, LANES)), const((LANES, A_WIDTH)),
            const((C_WIDTH, LANES)), const((LANES, C_WIDTH)),
        ],
        out_specs=pl.BlockSpec((BLOCK, N_BRANCH * BRANCH_WIDTH), lambda i: (i, 0)),
        out_shape=jax.ShapeDtypeStruct((t, N_BRANCH * BRANCH_WIDTH), BF16),
        scratch_shapes=[pltpu.VMEM((B_KV_HEADS, B_REP * BLOCK, 2 * BLOCK), F32)],
        compiler_params=_params(("arbitrary",)),
        name="mixer",
    )(pa, pb, pb, kvc, vgain_row, w_spatial, bs_bcast, qgb_row, kgb_row, sinks, qgc_row,
      s64, e64, s128, e128, s256, e256)


def _merge_body(y_ref, g0_ref, g1_ref, g2_ref, w_ref, o_ref, wb_ref):
    m = pl.program_id(1)

    @pl.when(m == 0)
    def _():
        wb_ref[...] = w_ref[...].astype(BF16)

    acc = None
    for b, g_ref in enumerate((g0_ref, g1_ref, g2_ref)):
        wide = jnp.dot(y_ref[:, b * BRANCH_WIDTH:(b + 1) * BRANCH_WIDTH], wb_ref[b],
                       preferred_element_type=F32)
        term = g_ref[...].astype(F32) * wide
        acc = term if acc is None else acc + term
    o_ref[...] = acc.astype(o_ref.dtype)


def _merge_call(y, gates, w_branch, layer):
    t = y.shape[0]
    per = D_MODEL // TN_MERGE

    def gate_spec(b):
        return pl.BlockSpec((TM_MERGE, TN_MERGE), lambda n, m: (m, b * per + n))

    return pl.pallas_call(
        _merge_body,
        grid=(D_MODEL // TN_MERGE, t // TM_MERGE),
        in_specs=[pl.BlockSpec((TM_MERGE, N_BRANCH * BRANCH_WIDTH), lambda n, m: (m, 0)),
                  gate_spec(0), gate_spec(1), gate_spec(2),
                  pl.BlockSpec((None, N_BRANCH, BRANCH_WIDTH, TN_MERGE),
                               lambda n, m: (layer, 0, 0, n))],
        out_specs=pl.BlockSpec((TM_MERGE, TN_MERGE), lambda n, m: (m, n)),
        out_shape=jax.ShapeDtypeStruct((t, D_MODEL), BF16),
        scratch_shapes=[pltpu.VMEM((N_BRANCH, BRANCH_WIDTH, TN_MERGE), BF16)],
        compiler_params=_params(("arbitrary", "arbitrary")),
        name="merge",
    )(y, gates, gates, gates, w_branch)


def _outproj_body(mg_ref, x_ref, w_ref, gn_ref, wr_ref, br_ref, x1_ref, lg_ref,
                  wr2_ref):
    i = pl.program_id(0)

    @pl.when(i == 0)
    def _():
        wr = wr_ref[...]
        hi = wr.astype(BF16)
        wr2_ref[:, 0:LANES] = hi
        wr2_ref[:, LANES:2 * LANES] = (wr - hi.astype(F32)).astype(BF16)

    x1 = x_ref[...] + jnp.dot(mg_ref[...], w_ref[...], preferred_element_type=F32)
    x1_ref[...] = x1
    h2 = _rms(x1, gn_ref[...])
    hi = h2.astype(BF16)
    lo = (h2 - hi.astype(F32)).astype(BF16)
    hi_w = jnp.dot(hi, wr2_ref[...], preferred_element_type=F32)
    lo_w = jnp.dot(lo, wr2_ref[...], preferred_element_type=F32)
    lg_ref[...] = hi_w[:, 0:LANES] + lo_w[:, 0:LANES] + hi_w[:, LANES:2 * LANES] + br_ref[...]


def _outproj_call(merged, x, w_out_bf16, gn_row, w_router, b_router):
    t = x.shape[0]
    return pl.pallas_call(
        _outproj_body,
        grid=(t // TM_OUT,),
        in_specs=[pl.BlockSpec((TM_OUT, D_MODEL), lambda i: (i, 0)),
                  pl.BlockSpec((TM_OUT, D_MODEL), lambda i: (i, 0)),
                  pl.BlockSpec((D_MODEL, D_MODEL), lambda i: (0, 0),
                               pipeline_mode=pl.Buffered(1)),
                  pl.BlockSpec((1, D_MODEL), lambda i: (0, 0)),
                  pl.BlockSpec((D_MODEL, LANES), lambda i: (0, 0)),
                  pl.BlockSpec((1, LANES), lambda i: (0, 0))],
        out_specs=[pl.BlockSpec((TM_OUT, D_MODEL), lambda i: (i, 0)),
                   pl.BlockSpec((TM_OUT, LANES), lambda i: (i, 0))],
        out_shape=[jax.ShapeDtypeStruct((t, D_MODEL), F32),
                   jax.ShapeDtypeStruct((t, LANES), F32)],
        scratch_shapes=[pltpu.VMEM((D_MODEL, 2 * LANES), BF16)],
        compiler_params=_params(("arbitrary",)),
        name="outproj",
    )(merged, x, w_out_bf16, gn_row, w_router, b_router)


def _route_body(lg_ref, ri_ref, rw_ref, cnt_ref, carry_ref):
    i = pl.program_id(0)

    @pl.when(i == 0)
    def _():
        carry_ref[...] = jnp.zeros_like(carry_ref)

    lg = lg_ref[...]
    tm = lg.shape[0]
    lane = lax.broadcasted_iota(I32, lg.shape, 1)
    neg = -jnp.inf
    big = jnp.int32(LANES)

    is_g = lane < N_GROUPS
    gl = jnp.where(is_g, lg, neg)
    gmax = jnp.max(gl, axis=-1, keepdims=True)
    gidx = jnp.min(jnp.where(gl == gmax, lane, big), axis=-1, keepdims=True)
    p_g = 1.0 / jnp.sum(jnp.where(is_g, jnp.exp(lg - gmax), 0.0), axis=-1, keepdims=True)

    e_lane = lane - N_GROUPS
    in_group = (e_lane >= 0) & (e_lane < N_EXPERTS) & ((e_lane >> 2) == gidx)
    el = jnp.where(in_group, lg, neg)
    v0 = jnp.max(el, axis=-1, keepdims=True)
    i0 = jnp.min(jnp.where(el == v0, lane, big), axis=-1, keepdims=True)
    el1 = jnp.where(lane == i0, neg, el)
    v1 = jnp.max(el1, axis=-1, keepdims=True)
    i1 = jnp.min(jnp.where(el1 == v1, lane, big), axis=-1, keepdims=True)
    e0 = i0 - N_GROUPS
    e1 = i1 - N_GROUPS
    ex = jnp.exp(v1 - v0)
    w0 = p_g / (1.0 + ex)
    w1 = p_g * ex / (1.0 + ex)

    oh0 = lane == e0
    oh1 = lane == e1
    oh = jnp.where(oh0 | oh1, 1.0, 0.0)
    r_i = lax.broadcasted_iota(I32, (tm, tm), 0)
    c_i = lax.broadcasted_iota(I32, (tm, tm), 1)
    lower = jnp.where(c_i < r_i, 1.0, 0.0).astype(BF16)
    before = jnp.dot(lower, oh.astype(BF16), preferred_element_type=F32) + carry_ref[...]
    rank0 = jnp.sum(jnp.where(oh0, before, 0.0), axis=-1, keepdims=True).astype(I32)
    rank1 = jnp.sum(jnp.where(oh1, before, 0.0), axis=-1, keepdims=True).astype(I32)
    carry_ref[...] = carry_ref[...] + jnp.sum(oh, axis=0, keepdims=True)

    zero_i = jnp.zeros_like(lane)
    ri_ref[...] = jnp.where(lane == 0, e0, jnp.where(lane == 1, e1,
                            jnp.where(lane == 2, rank0, jnp.where(lane == 3, rank1, zero_i))))
    rw_ref[...] = jnp.where(lane == 0, w0, jnp.where(lane == 1, w1, 0.0))
    cnt_ref[...] = jnp.broadcast_to(carry_ref[...], cnt_ref.shape).astype(I32)


def _route_call(logits):
    t = logits.shape[0]
    return pl.pallas_call(
        _route_body,
        grid=(t // TM_ROUTE,),
        in_specs=[pl.BlockSpec((TM_ROUTE, LANES), lambda i: (i, 0))],
        out_specs=[pl.BlockSpec((TM_ROUTE, LANES), lambda i: (i, 0)),
                   pl.BlockSpec((TM_ROUTE, LANES), lambda i: (i, 0)),
                   pl.BlockSpec((8, LANES), lambda i: (0, 0))],
        out_shape=[jax.ShapeDtypeStruct((t, LANES), I32),
                   jax.ShapeDtypeStruct((t, LANES), F32),
                   jax.ShapeDtypeStruct((8, LANES), I32)],
        scratch_shapes=[pltpu.VMEM((1, LANES), F32)],
        compiler_params=_params(("arbitrary",)),
        name="route",
    )(logits)


def _plan(route_i, counts, n_tiles):
    c = counts[0, :N_EXPERTS]
    tiles = (c + TM_EXPERT - 1) // TM_EXPERT
    ctiles = jnp.cumsum(tiles)
    start = (ctiles - tiles) * TM_EXPERT
    eid = jnp.arange(N_EXPERTS, dtype=I32)

    def slot(e, r):
        return jnp.sum(jnp.where(e[:, None] == eid[None, :], start[None, :], 0), axis=1) + r

    pos0 = slot(route_i[:, 0], route_i[:, 2]).astype(I32)
    pos1 = slot(route_i[:, 1], route_i[:, 3]).astype(I32)
    n_used = ctiles[-1]
    tid = jnp.minimum(jnp.arange(n_tiles, dtype=I32), n_used - 1)
    tile_expert = jnp.sum(tid[:, None] >= ctiles[None, :], axis=1).astype(I32)
    return pos0, pos1, tile_expert, n_used.reshape(1).astype(I32)


def _invert_body(p0_ref, p1_ref, src_ref):
    i = pl.program_id(0)

    @pl.when(i == 0)
    def _():
        def fill(r, c):
            src_ref[r] = 0
            return c
        lax.fori_loop(0, src_ref.shape[0], fill, 0, unroll=8)

    base = i * TK_INVERT

    def scatter(t, c):
        src_ref[p0_ref[t]] = base + t
        src_ref[p1_ref[t]] = base + t
        return c
    lax.fori_loop(0, TK_INVERT, scatter, 0, unroll=8)


def _invert_call(pos0, pos1, n_rows):
    t = pos0.shape[0]
    smem = pl.BlockSpec((TK_INVERT,), lambda i: (i,), memory_space=pltpu.SMEM)
    return pl.pallas_call(
        _invert_body,
        grid=(t // TK_INVERT,),
        in_specs=[smem, smem],
        out_specs=pl.BlockSpec(memory_space=pltpu.SMEM),
        out_shape=jax.ShapeDtypeStruct((n_rows,), I32),
        compiler_params=_params(("arbitrary",)),
        name="invert",
    )(pos0, pos1)


def _expert_body(te_ref, nu_ref, srcc_ref, srcn_ref, x_hbm, gn_ref, wg_ref, wu_ref, wd_ref,
                 y_ref, xbuf_ref, wgb_ref, wub_ref, wdb_ref, sems):
    i = pl.program_id(0)
    n_used = nu_ref[0]
    slot = lax.rem(i, 2)

    def row_copy(src_ref, r, s):
        return pltpu.make_async_copy(x_hbm.at[pl.ds(src_ref[r], 1)],
                                     xbuf_ref.at[s, pl.ds(r, 1)], sems.at[s])

    def issue(src_ref, s):
        for r in range(TM_EXPERT):
            row_copy(src_ref, r, s).start(priority=r % 2)

    @pl.when(i == 0)
    def _():
        issue(srcc_ref, 0)

    @pl.when(i < n_used)
    def _():
        prev = te_ref[jnp.maximum(i - 1, 0)]

        @pl.when(jnp.logical_or(i == 0, te_ref[i] != prev))
        def _():
            wgb_ref[...] = wg_ref[...].astype(BF16)
            wub_ref[...] = wu_ref[...].astype(BF16)
            wdb_ref[...] = wd_ref[...].astype(BF16)

        def drain(r, c):
            row_copy(srcc_ref, r, slot).wait()
            return c
        lax.fori_loop(0, TM_EXPERT, drain, 0, unroll=8)

    def compute(s, prefetch):
        h = _rms(xbuf_ref[s], gn_ref[...]).astype(BF16)
        if prefetch:
            issue(srcn_ref, 1 - s)
        g = jnp.dot(h, wgb_ref[...], preferred_element_type=F32)
        u = jnp.dot(h, wub_ref[...], preferred_element_type=F32)
        hid = (g * _sigmoid(g) * u).astype(BF16)
        y_ref[...] = jnp.dot(hid, wdb_ref[...], preferred_element_type=F32)

    for s in range(2):
        @pl.when(jnp.logical_and(i + 1 < n_used, slot == s))
        def _():
            compute(s, True)

        @pl.when(jnp.logical_and(i + 1 == n_used, slot == s))
        def _():
            compute(s, False)

    @pl.when(i >= n_used)
    def _():
        y_ref[...] = jnp.zeros_like(y_ref)


def _expert_call(x1, src, tile_expert, n_used, gn_row, w_gate, w_up, w_down, layer):
    n_rows = src.shape[0]
    n_tiles = n_rows // TM_EXPERT

    def w_map(i, te, nu):
        return (layer, te[i], 0, 0)

    grid_spec = pltpu.PrefetchScalarGridSpec(
        num_scalar_prefetch=2,
        grid=(n_tiles,),
        in_specs=[pl.BlockSpec((TM_EXPERT,), lambda i, te, nu: (i,), memory_space=pltpu.SMEM),
                  pl.BlockSpec((TM_EXPERT,), lambda i, te, nu: (jnp.minimum(i + 1, n_tiles - 1),),
                               memory_space=pltpu.SMEM),
                  pl.BlockSpec(memory_space=pl.ANY),
                  pl.BlockSpec((1, D_MODEL), lambda i, te, nu: (0, 0)),
                  pl.BlockSpec((None, None, D_MODEL, D_FF_EXPERT), w_map),
                  pl.BlockSpec((None, None, D_MODEL, D_FF_EXPERT), w_map),
                  pl.BlockSpec((None, None, D_FF_EXPERT, D_MODEL), w_map)],
        out_specs=pl.BlockSpec((TM_EXPERT, D_MODEL), lambda i, te, nu: (i, 0)),
        scratch_shapes=[pltpu.VMEM((2, TM_EXPERT, D_MODEL), F32),
                        pltpu.VMEM((D_MODEL, D_FF_EXPERT), BF16),
                        pltpu.VMEM((D_MODEL, D_FF_EXPERT), BF16),
                        pltpu.VMEM((D_FF_EXPERT, D_MODEL), BF16),
                        pltpu.SemaphoreType.DMA((2,))],
    )
    return pl.pallas_call(
        _expert_body,
        grid_spec=grid_spec,
        out_shape=jax.ShapeDtypeStruct((n_rows, D_MODEL), F32),
        compiler_params=_params(("arbitrary",)),
        name="experts",
    )(tile_expert, n_used, src, src, x1, gn_row, w_gate, w_up, w_down)


def _combine_body(p0c_ref, p1c_ref, p0n_ref, p1n_ref, x1_ref, rw_ref, gn_ref, ys_hbm,
                  x2_ref, h_ref, buf_ref, sems, *, n_steps):
    i = pl.program_id(0)
    slot = lax.rem(i, 2)

    def row_copy(p_ref, k, t, s):
        return pltpu.make_async_copy(ys_hbm.at[pl.ds(p_ref[t], 1)],
                                     buf_ref.at[s, k, pl.ds(t, 1)], sems.at[s])

    def issue(p0_ref, p1_ref, s):
        for t in range(TM_COMBINE):
            row_copy(p0_ref, 0, t, s).start(priority=0)
            row_copy(p1_ref, 1, t, s).start(priority=1)

    @pl.when(i == 0)
    def _():
        issue(p0c_ref, p1c_ref, 0)

    def drain(t, c):
        row_copy(p0c_ref, 0, t, slot).wait()
        row_copy(p1c_ref, 1, t, slot).wait()
        return c
    lax.fori_loop(0, TM_COMBINE, drain, 0, unroll=8)

    def compute(s, prefetch):
        rw = rw_ref[...]
        x2 = x1_ref[...] + rw[:, 0:1] * buf_ref[s, 0] + rw[:, 1:2] * buf_ref[s, 1]
        if prefetch:
            issue(p0n_ref, p1n_ref, 1 - s)
        x2_ref[...] = x2
        h_ref[...] = _rms(x2, gn_ref[...]).astype(h_ref.dtype)

    for s in range(2):
        @pl.when(jnp.logical_and(i + 1 < n_steps, slot == s))
        def _():
            compute(s, True)

        @pl.when(jnp.logical_and(i + 1 == n_steps, slot == s))
        def _():
            compute(s, False)


def _combine_call(x1, route_w, pos0, pos1, ys, gn_row):
    t = x1.shape[0]
    n_steps = t // TM_COMBINE

    def nxt(i):
        return (jnp.minimum(i + 1, n_steps - 1),)

    smem = functools.partial(pl.BlockSpec, (TM_COMBINE,), memory_space=pltpu.SMEM)
    return pl.pallas_call(
        functools.partial(_combine_body, n_steps=n_steps),
        grid=(n_steps,),
        in_specs=[smem(lambda i: (i,)), smem(lambda i: (i,)), smem(nxt), smem(nxt),
                  pl.BlockSpec((TM_COMBINE, D_MODEL), lambda i: (i, 0)),
                  pl.BlockSpec((TM_COMBINE, LANES), lambda i: (i, 0)),
                  pl.BlockSpec((1, D_MODEL), lambda i: (0, 0)),
                  pl.BlockSpec(memory_space=pl.ANY)],
        out_specs=[pl.BlockSpec((TM_COMBINE, D_MODEL), lambda i: (i, 0)),
                   pl.BlockSpec((TM_COMBINE, D_MODEL), lambda i: (i, 0))],
        out_shape=[jax.ShapeDtypeStruct((t, D_MODEL), F32),
                   jax.ShapeDtypeStruct((t, D_MODEL), BF16)],
        scratch_shapes=[pltpu.VMEM((2, TOP_K, TM_COMBINE, D_MODEL), F32),
                        pltpu.SemaphoreType.DMA((2,))],
        compiler_params=_params(("arbitrary",)),
        name="combine",
    )(pos0, pos1, pos0, pos1, x1, route_w, gn_row, ys)


def kernel(x, mem, norm_mix, norm_mem, norm_ffn, w_in, v_gain, w_spatial, b_spatial,
           q_gain_b, k_gain_b, sinks, q_gain_c, k_gain_c, w_mem_kv, w_branch, w_out,
           w_router_group, b_router_group, w_router_expert, b_router_expert,
           w_gate_e, w_up_e, w_down_e):
    b, s, d = x.shape
    t = b * s
    assert d == D_MODEL and s % BLOCK == 0 and mem.shape[1] == MEM_LEN
    assert t % TK_INVERT == 0 and t % TM_PROJ == 0
    n_tiles = (t * TOP_K) // TM_EXPERT + N_EXPERTS
    n_rows = n_tiles * TM_EXPERT

    xf = x.reshape(t, d)
    mem2d = mem.reshape(b * MEM_LEN, d)
    pad = LANES - N_GROUPS - N_EXPERTS

    h = _norm_call(xf, norm_mix[0][None])
    for layer in range(DEPTH):
        kvc = _memkv_call(mem2d, norm_mem[layer][None], w_mem_kv, k_gain_c[layer][None], layer)
        pa = _inproj_call(h, w_in, layer, 0, SEC_A_WIDTH, TN_PROJ_A, _gelu, "inproj_a")
        pb = _inproj_call(h, w_in, layer, COL_QB, SEC_B_WIDTH, TN_PROJ_B, _identity, "inproj_b")
        gates = _inproj_call(h, w_in, layer, COL_GATE, SEC_G_WIDTH, TN_PROJ_G, _sigmoid, "inproj_g")
        bs_bcast = jnp.broadcast_to(b_spatial[layer][:, :, None], (A_GROUPS, CHUNK, LANES))
        y = _mixer_call(pa, pb, kvc, v_gain[layer].reshape(1, A_WIDTH), w_spatial, bs_bcast,
                        jnp.tile(q_gain_b[layer], B_HEADS)[None],
                        jnp.tile(k_gain_b[layer], B_KV_HEADS)[None],
                        sinks[layer], jnp.tile(q_gain_c[layer], C_HEADS)[None], layer, s)
        merged = _merge_call(y, gates, w_branch, layer)
        w_router = jnp.pad(jnp.concatenate([w_router_group[layer], w_router_expert[layer]], axis=1),
                           ((0, 0), (0, pad)))
        b_router = jnp.pad(jnp.concatenate([b_router_group[layer], b_router_expert[layer]]),
                           (0, pad))[None]
        x1, logits = _outproj_call(merged, xf, w_out[layer].astype(BF16),
                                   norm_ffn[layer][None], w_router, b_router)
        route_i, route_w, counts = _route_call(logits)
        pos0, pos1, tile_expert, n_used = _plan(route_i, counts, n_tiles)
        src = _invert_call(pos0, pos1, n_rows)
        ys = _expert_call(x1, src, tile_expert, n_used, norm_ffn[layer][None], w_gate_e, w_up_e,
                          w_down_e, layer)
        gn_next = norm_mix[layer + 1] if layer + 1 < DEPTH else norm_mix[layer]
        xf, h = _combine_call(x1, route_w, pos0, pos1, ys, gn_next[None])
    return xf.reshape(b, s, d)
```

```python
import functools

import jax
import jax.numpy as jnp
from jax import lax
from jax.experimental import pallas as pl
from jax.experimental.pallas import tpu as pltpu

F32 = jnp.float32
BF16 = jnp.bfloat16
I32 = jnp.int32

D_MODEL = 2048
DEPTH = 2
MEM_LEN = 256
EPS = 1e-6
BRANCH_WIDTH = 1024
N_BRANCH = 3

CHUNK = 128
A_GROUPS = 8
A_GROUP_DIM = 128
A_WIDTH = A_GROUPS * A_GROUP_DIM

BLOCK = 128
B_HEADS = 16
B_KV_HEADS = 4
B_REP = B_HEADS // B_KV_HEADS
B_HEAD_DIM = 64
B_Q_WIDTH = B_HEADS * B_HEAD_DIM
B_KV_WIDTH = B_KV_HEADS * B_HEAD_DIM

C_HEADS = 4
C_HEAD_DIM = 256
C_WIDTH = C_HEADS * C_HEAD_DIM

COL_U = 0
COL_V = A_WIDTH
COL_QB = 2 * A_WIDTH
COL_KB = COL_QB + B_Q_WIDTH
COL_VB = COL_KB + B_KV_WIDTH
COL_QC = COL_VB + B_KV_WIDTH
COL_GATE = COL_QC + C_WIDTH
IN_WIDTH = COL_GATE + N_BRANCH * D_MODEL
SEC_A_WIDTH = COL_QB
SEC_B_WIDTH = COL_GATE - COL_QB
SEC_G_WIDTH = N_BRANCH * D_MODEL
PB_Q = 0
PB_K = COL_KB - COL_QB
PB_V = COL_VB - COL_QB
PB_QC = COL_QC - COL_QB

N_GROUPS = 4
EXPERTS_PER_GROUP = 4
N_EXPERTS = N_GROUPS * EXPERTS_PER_GROUP
TOP_K = 2
D_FF_EXPERT = 512

LANES = 128
VMEM_LIMIT = 56 * 1024 * 1024
SQRT_HALF = 0.7071067811865476

TM_NORM = 512
TM_PROJ = 1024
TN_PROJ_A, TN_PROJ_B, TN_PROJ_G = 1024, 512, 1536
TM_MERGE, TN_MERGE = 512, 1024
TM_OUT = 512
TM_ROUTE = 512
TK_INVERT = 2048
TM_EXPERT = 256
TM_COMBINE = 256


def _params(sem):
    return pltpu.CompilerParams(dimension_semantics=sem, vmem_limit_bytes=VMEM_LIMIT)


def _rms(x, gain):
    ms = jnp.mean(x * x, axis=-1, keepdims=True)
    return x * lax.rsqrt(ms + EPS) * gain


def _norm_body(x_ref, g_ref, o_ref):
    o_ref[...] = _rms(x_ref[...], g_ref[...]).astype(o_ref.dtype)


def _norm_call(x, gain_row):
    t = x.shape[0]
    return pl.pallas_call(
        _norm_body,
        grid=(t // TM_NORM,),
        in_specs=[pl.BlockSpec((TM_NORM, D_MODEL), lambda i: (i, 0)),
                  pl.BlockSpec((1, D_MODEL), lambda i: (0, 0))],
        out_specs=pl.BlockSpec((TM_NORM, D_MODEL), lambda i: (i, 0)),
        out_shape=jax.ShapeDtypeStruct((t, D_MODEL), BF16),
        compiler_params=_params(("arbitrary",)),
        name="norm",
    )(x, gain_row)


TN_MEMKV = 512


def _memkv_body(mem_ref, gm_ref, w_ref, gk_ref, o_ref):
    n = pl.program_id(0)
    h = _rms(mem_ref[...], gm_ref[...]).astype(BF16)
    acc = jnp.dot(h, w_ref[...].astype(BF16), preferred_element_type=F32)

    @pl.when(n < C_WIDTH // TN_MEMKV)
    def _():
        for j in range(TN_MEMKV // C_HEAD_DIM):
            sl = slice(j * C_HEAD_DIM, (j + 1) * C_HEAD_DIM)
            o_ref[:, sl] = _rms(acc[:, sl], gk_ref[...]).astype(o_ref.dtype)

    @pl.when(n >= C_WIDTH // TN_MEMKV)
    def _():
        o_ref[...] = acc.astype(o_ref.dtype)


def _memkv_call(mem2d, gm_row, w_mem_kv, gk_row, layer):
    rows = mem2d.shape[0]
    return pl.pallas_call(
        _memkv_body,
        grid=(2 * C_WIDTH // TN_MEMKV,),
        in_specs=[pl.BlockSpec((rows, D_MODEL), lambda n: (0, 0)),
                  pl.BlockSpec((1, D_MODEL), lambda n: (0, 0)),
                  pl.BlockSpec((None, D_MODEL, TN_MEMKV), lambda n: (layer, 0, n)),
                  pl.BlockSpec((1, C_HEAD_DIM), lambda n: (0, 0))],
        out_specs=pl.BlockSpec((rows, TN_MEMKV), lambda n: (0, n)),
        out_shape=jax.ShapeDtypeStruct((rows, 2 * C_WIDTH), BF16),
        compiler_params=_params(("arbitrary",)),
        name="memkv",
    )(mem2d, gm_row, w_mem_kv, gk_row)


def _gelu(x):
    return 0.5 * x * (1.0 + lax.erf(x * SQRT_HALF))


def _sigmoid(x):
    return 1.0 / (1.0 + jnp.exp(-x))


def _identity(x):
    return x


def _inproj_body(h_ref, w_ref, o_ref, wb_ref, *, act):
    @pl.when(pl.program_id(1) == 0)
    def _():
        wb_ref[...] = w_ref[...].astype(BF16)

    acc = jnp.dot(h_ref[...], wb_ref[...], preferred_element_type=F32)
    o_ref[...] = act(acc).astype(o_ref.dtype)


def _inproj_call(h, w_in, layer, col0, width, tn, act, name):
    t = h.shape[0]
    assert col0 % tn == 0 and width % tn == 0
    n0 = col0 // tn
    return pl.pallas_call(
        functools.partial(_inproj_body, act=act),
        grid=(width // tn, t // TM_PROJ),
        in_specs=[pl.BlockSpec((TM_PROJ, D_MODEL), lambda n, m: (m, 0)),
                  pl.BlockSpec((None, D_MODEL, tn), lambda n, m: (layer, 0, n0 + n))],
        out_specs=pl.BlockSpec((TM_PROJ, tn), lambda n, m: (m, n)),
        out_shape=jax.ShapeDtypeStruct((t, width), BF16),
        scratch_shapes=[pltpu.VMEM((D_MODEL, tn), BF16)],
        compiler_params=_params(("arbitrary", "arbitrary")),
        name=name,
    )(h, w_in)


def _group_sums(width, group):
    gid = jnp.arange(width, dtype=I32) // group
    s = (gid[:, None] == jnp.arange(LANES, dtype=I32)[None, :]).astype(BF16)
    return s, s.T


def _group_ssq(x, s):
    return jnp.dot((x * x).astype(BF16), s, preferred_element_type=F32)


def _group_bcast(ssq, e, group):
    r = lax.rsqrt(ssq * (1.0 / group) + EPS)
    r_hi = r.astype(BF16)
    r_lo = (r - r_hi.astype(F32)).astype(BF16)
    return (jnp.dot(r_hi, e, preferred_element_type=F32)
            + jnp.dot(r_lo, e, preferred_element_type=F32))


def _mixer_body(a_ref, b_ref, prev_ref, kvc_ref, vgain_ref, ws_ref, bs_ref, qgb_ref, kgb_ref,
                sink_ref, qgc_ref, s64_ref, e64_ref, s128_ref, e128_ref, s256_ref, e256_ref,
                y_ref, sb_ref, *, blocks_per_seq):
    i = pl.program_id(0)
    n = lax.rem(i, blocks_per_seq)
    ones_k = jnp.ones((2 * BLOCK, LANES), BF16)
    row = lax.broadcasted_iota(I32, (CHUNK, CHUNK), 0)
    col = lax.broadcasted_iota(I32, (CHUNK, CHUNK), 1)
    causal = col <= row
    lane_k = lax.broadcasted_iota(I32, (2 * BLOCK, LANES), 1)
    lane_q = lax.broadcasted_iota(I32, (BLOCK, LANES), 1)
    lo_k = lane_k < B_HEAD_DIM
    lo_q = lane_q < B_HEAD_DIM
    qi = lax.broadcasted_iota(I32, (B_REP * BLOCK, 2 * BLOCK), 0) & (BLOCK - 1)
    kj = lax.broadcasted_iota(I32, (B_REP * BLOCK, 2 * BLOCK), 1)
    rel = BLOCK + qi - kj
    valid = (rel >= 0) & (rel < BLOCK) & ((n > 0) | (kj >= BLOCK))

    v_all = a_ref[:, COL_V:COL_V + A_WIDTH].astype(F32)
    q_all = b_ref[:, PB_Q:PB_Q + B_Q_WIDTH].astype(F32)
    k_all = jnp.concatenate([prev_ref[:, 0:B_KV_WIDTH], b_ref[:, PB_K:PB_K + B_KV_WIDTH]],
                            axis=0).astype(F32)
    qc_all = b_ref[:, PB_QC:PB_QC + C_WIDTH].astype(F32)
    ssq_q = _group_ssq(q_all, s64_ref[...])
    ssq_k = _group_ssq(k_all, s64_ref[0:B_KV_WIDTH, :])
    ssq_c = _group_ssq(qc_all, s256_ref[...])
    ssq_v = _group_ssq(v_all, s128_ref[...])
    qn_all = (q_all * _group_bcast(ssq_q, e64_ref[...], B_HEAD_DIM)
              * (qgb_ref[...] * (B_HEAD_DIM ** -0.5)))
    kn_all = k_all * _group_bcast(ssq_k, e64_ref[:, 0:B_KV_WIDTH], B_HEAD_DIM) * kgb_ref[...]
    qcn_all = (qc_all * _group_bcast(ssq_c, e256_ref[...], C_HEAD_DIM)
               * (qgc_ref[...] * (C_HEAD_DIM ** -0.5))).astype(BF16)
    vn_all = (v_all * _group_bcast(ssq_v, e128_ref[...], A_GROUP_DIM)
              * vgain_ref[...]).astype(BF16)
    v_kv = jnp.concatenate([prev_ref[:, B_KV_WIDTH:2 * B_KV_WIDTH],
                            b_ref[:, PB_V:PB_V + B_KV_WIDTH]], axis=0).astype(F32)

    scores_c = [
        lax.dot_general(qcn_all[:, h * C_HEAD_DIM:(h + 1) * C_HEAD_DIM],
                        kvc_ref[:, h * C_HEAD_DIM:(h + 1) * C_HEAD_DIM],
                        (((1,), (1,)), ((), ())), preferred_element_type=F32)
        for h in range(C_HEADS)]
    vdups, sinks = [], []
    for slab in range(B_KV_WIDTH // LANES):
        ks = kn_all[:, slab * LANES:(slab + 1) * LANES]
        vs = v_kv[:, slab * LANES:(slab + 1) * LANES]
        kr = pltpu.roll(ks, B_HEAD_DIM, axis=1)
        vr = pltpu.roll(vs, B_HEAD_DIM, axis=1)
        for par in range(2):
            kv = 2 * slab + par
            kdup = (jnp.where(lo_k, ks, kr) if par == 0 else jnp.where(lo_k, kr, ks)).astype(BF16)
            vdups.append((jnp.where(lo_k, vs, vr) if par == 0
                          else jnp.where(lo_k, vr, vs)).astype(BF16))
            qs, sk = [], []
            for r in range(B_REP):
                h = kv * B_REP + r
                q_slab = qn_all[:, (h // 2) * LANES:(h // 2 + 1) * LANES]
                keep = lo_q if h % 2 == 0 else jnp.logical_not(lo_q)
                qs.append(jnp.where(keep, q_slab, 0.0).astype(BF16))
                sk.append(jnp.full((BLOCK, LANES), sink_ref[h], F32))
            sinks.append(jnp.concatenate(sk, axis=0))
            sc = lax.dot_general(jnp.concatenate(qs, axis=0), kdup, (((1,), (1,)), ((), ())),
                                 preferred_element_type=F32)
            sb_ref[kv] = jnp.where(valid, sc, -jnp.inf)

    probs_c = [jnp.exp(s - jnp.max(s, axis=-1, keepdims=True)).astype(BF16) for s in scores_c]
    mixed_a = [
        jnp.dot(jnp.where(causal, ws_ref[g], 0.0).astype(BF16),
                vn_all[:, g * A_GROUP_DIM:(g + 1) * A_GROUP_DIM],
                preferred_element_type=F32) + bs_ref[g]
        for g in range(A_GROUPS)]
    maxes_b = [jnp.maximum(jnp.max(sb_ref[kv], axis=-1, keepdims=True), sinks[kv])
               for kv in range(B_KV_HEADS)]
    probs_b = [jnp.exp(sb_ref[kv] - jnp.concatenate([maxes_b[kv]] * (2 * BLOCK // LANES), axis=1)
                       ).astype(BF16) for kv in range(B_KV_HEADS)]
    sink_terms = [jnp.exp(sinks[kv] - maxes_b[kv]) for kv in range(B_KV_HEADS)]

    outs_c = [(jnp.dot(p, kvc_ref[:, C_WIDTH + h * C_HEAD_DIM:C_WIDTH + (h + 1) * C_HEAD_DIM],
                       preferred_element_type=F32),
               jnp.dot(p, ones_k, preferred_element_type=F32))
              for h, p in enumerate(probs_c)]
    for g, mixed in enumerate(mixed_a):
        sl = slice(g * A_GROUP_DIM, (g + 1) * A_GROUP_DIM)
        u = a_ref[:, COL_U + g * A_GROUP_DIM:COL_U + (g + 1) * A_GROUP_DIM].astype(F32)
        y_ref[:, sl] = (u * mixed).astype(y_ref.dtype)
    outs_b = [(jnp.dot(p, vdup, preferred_element_type=F32),
               jnp.dot(p, ones_k, preferred_element_type=F32) + st)
              for p, vdup, st in zip(probs_b, vdups, sink_terms)]

    for h, (o, denom) in enumerate(outs_c):
        o = o / jnp.concatenate([denom] * (C_HEAD_DIM // LANES), axis=1)
        y_ref[:, A_WIDTH + B_Q_WIDTH + h * C_HEAD_DIM:
              A_WIDTH + B_Q_WIDTH + (h + 1) * C_HEAD_DIM] = o.astype(y_ref.dtype)
    for kv, (o, denom) in enumerate(outs_b):
        o = o / denom
        for pr in range(B_REP // 2):
            even = o[(2 * pr) * BLOCK:(2 * pr + 1) * BLOCK]
            odd = o[(2 * pr + 1) * BLOCK:(2 * pr + 2) * BLOCK]
            c0 = A_WIDTH + (kv * B_REP // 2 + pr) * LANES
            y_ref[:, c0:c0 + LANES] = jnp.where(lo_q, even, odd).astype(y_ref.dtype)


def _mixer_call(pa, pb, kvc, vgain_row, w_spatial, bs_bcast, qgb_row, kgb_row, sinks, qgc_row,
                layer, seq):
    t = pa.shape[0]
    bps = seq // BLOCK
    kvb = PB_K // (2 * B_KV_WIDTH)
    s64, e64 = _group_sums(B_Q_WIDTH, B_HEAD_DIM)
    s128, e128 = _group_sums(A_WIDTH, A_GROUP_DIM)
    s256, e256 = _group_sums(C_WIDTH, C_HEAD_DIM)

    def const(shape):
        return pl.BlockSpec(shape, lambda i: (0,) * len(shape))

    return pl.pallas_call(
        functools.partial(_mixer_body, blocks_per_seq=bps),
        grid=(t // BLOCK,),
        in_specs=[
            pl.BlockSpec((BLOCK, SEC_A_WIDTH), lambda i: (i, 0)),
            pl.BlockSpec((BLOCK, SEC_B_WIDTH), lambda i: (i, 0)),
            pl.BlockSpec((BLOCK, 2 * B_KV_WIDTH), lambda i: (jnp.maximum(i - 1, 0), kvb)),
            pl.BlockSpec((MEM_LEN, 2 * C_WIDTH), lambda i: (i // bps, 0)),
            const((1, A_WIDTH)),
            pl.BlockSpec((None, A_GROUPS, CHUNK, CHUNK), lambda i: (layer, 0, 0, 0)),
            const((A_GROUPS, CHUNK, LANES)),
            const((1, B_Q_WIDTH)),
            const((1, B_KV_WIDTH)),
            pl.BlockSpec(memory_space=pltpu.SMEM),
            const((1, C_WIDTH)),
            const((B_Q_WIDTH, LANES)), const((LANES, B_Q_WIDTH)),
            const((A_WIDTH, LANES)), const((LANES, A_WIDTH)),
            const((C_WIDTH, LANES)), const((LANES, C_WIDTH)),
        ],
        out_specs=pl.BlockSpec((BLOCK, N_BRANCH * BRANCH_WIDTH), lambda i: (i, 0)),
        out_shape=jax.ShapeDtypeStruct((t, N_BRANCH * BRANCH_WIDTH), BF16),
        scratch_shapes=[pltpu.VMEM((B_KV_HEADS, B_REP * BLOCK, 2 * BLOCK), F32)],
        compiler_params=_params(("arbitrary",)),
        name="mixer",
    )(pa, pb, pb, kvc, vgain_row, w_spatial, bs_bcast, qgb_row, kgb_row, sinks, qgc_row,
      s64, e64, s128, e128, s256, e256)


def _merge_body(y_ref, g0_ref, g1_ref, g2_ref, w_ref, o_ref, wb_ref):
    m = pl.program_id(1)

    @pl.when(m == 0)
    def _():
        wb_ref[...] = w_ref[...].astype(BF16)

    acc = None
    for b, g_ref in enumerate((g0_ref, g1_ref, g2_ref)):
        wide = jnp.dot(y_ref[:, b * BRANCH_WIDTH:(b + 1) * BRANCH_WIDTH], wb_ref[b],
                       preferred_element_type=F32)
        term = g_ref[...].astype(F32) * wide
        acc = term if acc is None else acc + term
    o_ref[...] = acc.astype(o_ref.dtype)


def _merge_call(y, gates, w_branch, layer):
    t = y.shape[0]
    per = D_MODEL // TN_MERGE

    def gate_spec(b):
        return pl.BlockSpec((TM_MERGE, TN_MERGE), lambda n, m: (m, b * per + n))

    return pl.pallas_call(
        _merge_body,
        grid=(D_MODEL // TN_MERGE, t // TM_MERGE),
        in_specs=[pl.BlockSpec((TM_MERGE, N_BRANCH * BRANCH_WIDTH), lambda n, m: (m, 0)),
                  gate_spec(0), gate_spec(1), gate_spec(2),
                  pl.BlockSpec((None, N_BRANCH, BRANCH_WIDTH, TN_MERGE),
                               lambda n, m: (layer, 0, 0, n))],
        out_specs=pl.BlockSpec((TM_MERGE, TN_MERGE), lambda n, m: (m, n)),
        out_shape=jax.ShapeDtypeStruct((t, D_MODEL), BF16),
        scratch_shapes=[pltpu.VMEM((N_BRANCH, BRANCH_WIDTH, TN_MERGE), BF16)],
        compiler_params=_params(("arbitrary", "arbitrary")),
        name="merge",
    )(y, gates, gates, gates, w_branch)


def _outproj_body(mg_ref, x_ref, w_ref, gn_ref, wr_ref, br_ref, x1_ref, lg_ref,
                  wr2_ref):
    i = pl.program_id(0)

    @pl.when(i == 0)
    def _():
        wr = wr_ref[...]
        hi = wr.astype(BF16)
        wr2_ref[:, 0:LANES] = hi
        wr2_ref[:, LANES:2 * LANES] = (wr - hi.astype(F32)).astype(BF16)

    x1 = x_ref[...] + jnp.dot(mg_ref[...], w_ref[...], preferred_element_type=F32)
    x1_ref[...] = x1
    h2 = _rms(x1, gn_ref[...])
    hi = h2.astype(BF16)
    lo = (h2 - hi.astype(F32)).astype(BF16)
    hi_w = jnp.dot(hi, wr2_ref[...], preferred_element_type=F32)
    lo_w = jnp.dot(lo, wr2_ref[...], preferred_element_type=F32)
    lg_ref[...] = hi_w[:, 0:LANES] + lo_w[:, 0:LANES] + hi_w[:, LANES:2 * LANES] + br_ref[...]


def _outproj_call(merged, x, w_out_bf16, gn_row, w_router, b_router):
    t = x.shape[0]
    return pl.pallas_call(
        _outproj_body,
        grid=(t // TM_OUT,),
        in_specs=[pl.BlockSpec((TM_OUT, D_MODEL), lambda i: (i, 0)),
                  pl.BlockSpec((TM_OUT, D_MODEL), lambda i: (i, 0)),
                  pl.BlockSpec((D_MODEL, D_MODEL), lambda i: (0, 0),
                               pipeline_mode=pl.Buffered(1)),
                  pl.BlockSpec((1, D_MODEL), lambda i: (0, 0)),
                  pl.BlockSpec((D_MODEL, LANES), lambda i: (0, 0)),
                  pl.BlockSpec((1, LANES), lambda i: (0, 0))],
        out_specs=[pl.BlockSpec((TM_OUT, D_MODEL), lambda i: (i, 0)),
                   pl.BlockSpec((TM_OUT, LANES), lambda i: (i, 0))],
        out_shape=[jax.ShapeDtypeStruct((t, D_MODEL), F32),
                   jax.ShapeDtypeStruct((t, LANES), F32)],
        scratch_shapes=[pltpu.VMEM((D_MODEL, 2 * LANES), BF16)],
        compiler_params=_params(("arbitrary",)),
        name="outproj",
    )(merged, x, w_out_bf16, gn_row, w_router, b_router)


def _route_body(lg_ref, ri_ref, rw_ref, cnt_ref, carry_ref):
    i = pl.program_id(0)

    @pl.when(i == 0)
    def _():
        carry_ref[...] = jnp.zeros_like(carry_ref)

    lg = lg_ref[...]
    tm = lg.shape[0]
    lane = lax.broadcasted_iota(I32, lg.shape, 1)
    neg = -jnp.inf
    big = jnp.int32(LANES)

    is_g = lane < N_GROUPS
    gl = jnp.where(is_g, lg, neg)
    gmax = jnp.max(gl, axis=-1, keepdims=True)
    gidx = jnp.min(jnp.where(gl == gmax, lane, big), axis=-1, keepdims=True)
    p_g = 1.0 / jnp.sum(jnp.where(is_g, jnp.exp(lg - gmax), 0.0), axis=-1, keepdims=True)

    e_lane = lane - N_GROUPS
    in_group = (e_lane >= 0) & (e_lane < N_EXPERTS) & ((e_lane >> 2) == gidx)
    el = jnp.where(in_group, lg, neg)
    v0 = jnp.max(el, axis=-1, keepdims=True)
    i0 = jnp.min(jnp.where(el == v0, lane, big), axis=-1, keepdims=True)
    el1 = jnp.where(lane == i0, neg, el)
    v1 = jnp.max(el1, axis=-1, keepdims=True)
    i1 = jnp.min(jnp.where(el1 == v1, lane, big), axis=-1, keepdims=True)
    e0 = i0 - N_GROUPS
    e1 = i1 - N_GROUPS
    ex = jnp.exp(v1 - v0)
    w0 = p_g / (1.0 + ex)
    w1 = p_g * ex / (1.0 + ex)

    oh0 = lane == e0
    oh1 = lane == e1
    oh = jnp.where(oh0 | oh1, 1.0, 0.0)
    r_i = lax.broadcasted_iota(I32, (tm, tm), 0)
    c_i = lax.broadcasted_iota(I32, (tm, tm), 1)
    lower = jnp.where(c_i < r_i, 1.0, 0.0).astype(BF16)
    before = jnp.dot(lower, oh.astype(BF16), preferred_element_type=F32) + carry_ref[...]
    rank0 = jnp.sum(jnp.where(oh0, before, 0.0), axis=-1, keepdims=True).astype(I32)
    rank1 = jnp.sum(jnp.where(oh1, before, 0.0), axis=-1, keepdims=True).astype(I32)
    carry_ref[...] = carry_ref[...] + jnp.sum(oh, axis=0, keepdims=True)

    zero_i = jnp.zeros_like(lane)
    ri_ref[...] = jnp.where(lane == 0, e0, jnp.where(lane == 1, e1,
                            jnp.where(lane == 2, rank0, jnp.where(lane == 3, rank1, zero_i))))
    rw_ref[...] = jnp.where(lane == 0, w0, jnp.where(lane == 1, w1, 0.0))
    cnt_ref[...] = jnp.broadcast_to(carry_ref[...], cnt_ref.shape).astype(I32)


def _route_call(logits):
    t = logits.shape[0]
    return pl.pallas_call(
        _route_body,
        grid=(t // TM_ROUTE,),
        in_specs=[pl.BlockSpec((TM_ROUTE, LANES), lambda i: (i, 0))],
        out_specs=[pl.BlockSpec((TM_ROUTE, LANES), lambda i: (i, 0)),
                   pl.BlockSpec((TM_ROUTE, LANES), lambda i: (i, 0)),
                   pl.BlockSpec((8, LANES), lambda i: (0, 0))],
        out_shape=[jax.ShapeDtypeStruct((t, LANES), I32),
                   jax.ShapeDtypeStruct((t, LANES), F32),
                   jax.ShapeDtypeStruct((8, LANES), I32)],
        scratch_shapes=[pltpu.VMEM((1, LANES), F32)],
        compiler_params=_params(("arbitrary",)),
        name="route",
    )(logits)


def _plan(route_i, counts, n_tiles):
    c = counts[0, :N_EXPERTS]
    tiles = (c + TM_EXPERT - 1) // TM_EXPERT
    ctiles = jnp.cumsum(tiles)
    start = (ctiles - tiles) * TM_EXPERT
    eid = jnp.arange(N_EXPERTS, dtype=I32)

    def slot(e, r):
        return jnp.sum(jnp.where(e[:, None] == eid[None, :], start[None, :], 0), axis=1) + r

    pos0 = slot(route_i[:, 0], route_i[:, 2]).astype(I32)
    pos1 = slot(route_i[:, 1], route_i[:, 3]).astype(I32)
    n_used = ctiles[-1]
    tid = jnp.minimum(jnp.arange(n_tiles, dtype=I32), n_used - 1)
    tile_expert = jnp.sum(tid[:, None] >= ctiles[None, :], axis=1).astype(I32)
    used_rows = n_used * TM_EXPERT
    pad_lo = jnp.concatenate([start + c, used_rows[None]]).astype(I32)
    pad_hi = jnp.concatenate([start + tiles * TM_EXPERT,
                              jnp.full((1,), n_tiles * TM_EXPERT, I32)]).astype(I32)
    return pos0, pos1, tile_expert, n_used.reshape(1).astype(I32), pad_lo, pad_hi


def _invert_body(p0_ref, p1_ref, lo_ref, hi_ref, src_ref):
    i = pl.program_id(0)

    @pl.when(i == 0)
    def _():
        def fill(r, c):
            src_ref[r] = 0
            return c
        for j in range(N_EXPERTS + 1):
            lax.fori_loop(lo_ref[j], hi_ref[j], fill, 0)

    base = i * TK_INVERT

    def scatter(t, c):
        src_ref[p0_ref[t]] = base + t
        src_ref[p1_ref[t]] = base + t
        return c
    lax.fori_loop(0, TK_INVERT, scatter, 0, unroll=8)


def _invert_call(pos0, pos1, pad_lo, pad_hi, n_rows):
    t = pos0.shape[0]
    smem = pl.BlockSpec((TK_INVERT,), lambda i: (i,), memory_space=pltpu.SMEM)
    whole = pl.BlockSpec(memory_space=pltpu.SMEM)
    return pl.pallas_call(
        _invert_body,
        grid=(t // TK_INVERT,),
        in_specs=[smem, smem, whole, whole],
        out_specs=pl.BlockSpec(memory_space=pltpu.SMEM),
        out_shape=jax.ShapeDtypeStruct((n_rows,), I32),
        compiler_params=_params(("arbitrary",)),
        name="invert",
    )(pos0, pos1, pad_lo, pad_hi)


def _expert_body(te_ref, nu_ref, srcc_ref, srcn_ref, x_hbm, gn_ref, wg_ref, wu_ref, wd_ref,
                 y_ref, xbuf_ref, wgb_ref, wub_ref, wdb_ref, sems):
    i = pl.program_id(0)
    n_used = nu_ref[0]
    slot = lax.rem(i, 2)

    def row_copy(src_ref, r, s):
        return pltpu.make_async_copy(x_hbm.at[pl.ds(src_ref[r], 1)],
                                     xbuf_ref.at[s, pl.ds(r, 1)], sems.at[s])

    def issue(src_ref, s):
        for r in range(TM_EXPERT):
            row_copy(src_ref, r, s).start(priority=r % 2)

    @pl.when(i == 0)
    def _():
        issue(srcc_ref, 0)

    @pl.when(i < n_used)
    def _():
        prev = te_ref[jnp.maximum(i - 1, 0)]

        @pl.when(jnp.logical_or(i == 0, te_ref[i] != prev))
        def _():
            wgb_ref[...] = wg_ref[...].astype(BF16)
            wub_ref[...] = wu_ref[...].astype(BF16)
            wdb_ref[...] = wd_ref[...].astype(BF16)

        def drain(r, c):
            row_copy(srcc_ref, r, slot).wait()
            return c
        lax.fori_loop(0, TM_EXPERT, drain, 0, unroll=8)

    def compute(s, prefetch):
        h = _rms(xbuf_ref[s], gn_ref[...]).astype(BF16)
        if prefetch:
            issue(srcn_ref, 1 - s)
        g = jnp.dot(h, wgb_ref[...], preferred_element_type=F32)
        u = jnp.dot(h, wub_ref[...], preferred_element_type=F32)
        hid = (g * _sigmoid(g) * u).astype(BF16)
        y_ref[...] = jnp.dot(hid, wdb_ref[...], preferred_element_type=F32)

    for s in range(2):
        @pl.when(jnp.logical_and(i + 1 < n_used, slot == s))
        def _():
            compute(s, True)

        @pl.when(jnp.logical_and(i + 1 == n_used, slot == s))
        def _():
            compute(s, False)

    @pl.when(i >= n_used)
    def _():
        y_ref[...] = jnp.zeros_like(y_ref)


def _expert_call(x1, src, tile_expert, n_used, gn_row, w_gate, w_up, w_down, layer):
    n_rows = src.shape[0]
    n_tiles = n_rows // TM_EXPERT

    def w_map(i, te, nu):
        return (layer, te[i], 0, 0)

    grid_spec = pltpu.PrefetchScalarGridSpec(
        num_scalar_prefetch=2,
        grid=(n_tiles,),
        in_specs=[pl.BlockSpec((TM_EXPERT,), lambda i, te, nu: (i,), memory_space=pltpu.SMEM),
                  pl.BlockSpec((TM_EXPERT,), lambda i, te, nu: (jnp.minimum(i + 1, n_tiles - 1),),
                               memory_space=pltpu.SMEM),
                  pl.BlockSpec(memory_space=pl.ANY),
                  pl.BlockSpec((1, D_MODEL), lambda i, te, nu: (0, 0)),
                  pl.BlockSpec((None, None, D_MODEL, D_FF_EXPERT), w_map),
                  pl.BlockSpec((None, None, D_MODEL, D_FF_EXPERT), w_map),
                  pl.BlockSpec((None, None, D_FF_EXPERT, D_MODEL), w_map)],
        out_specs=pl.BlockSpec((TM_EXPERT, D_MODEL), lambda i, te, nu: (i, 0)),
        scratch_shapes=[pltpu.VMEM((2, TM_EXPERT, D_MODEL), F32),
                        pltpu.VMEM((D_MODEL, D_FF_EXPERT), BF16),
                        pltpu.VMEM((D_MODEL, D_FF_EXPERT), BF16),
                        pltpu.VMEM((D_FF_EXPERT, D_MODEL), BF16),
                        pltpu.SemaphoreType.DMA((2,))],
    )
    return pl.pallas_call(
        _expert_body,
        grid_spec=grid_spec,
        out_shape=jax.ShapeDtypeStruct((n_rows, D_MODEL), F32),
        compiler_params=_params(("arbitrary",)),
        name="experts",
    )(tile_expert, n_used, src, src, x1, gn_row, w_gate, w_up, w_down)


def _combine_body(p0c_ref, p1c_ref, p0n_ref, p1n_ref, x1_ref, rw_ref, gn_ref, ys_hbm,
                  x2_ref, h_ref, buf_ref, sems, *, n_steps):
    i = pl.program_id(0)
    slot = lax.rem(i, 2)

    def row_copy(p_ref, k, t, s):
        return pltpu.make_async_copy(ys_hbm.at[pl.ds(p_ref[t], 1)],
                                     buf_ref.at[s, k, pl.ds(t, 1)], sems.at[s])

    def issue(p0_ref, p1_ref, s):
        for t in range(TM_COMBINE):
            row_copy(p0_ref, 0, t, s).start(priority=0)
            row_copy(p1_ref, 1, t, s).start(priority=1)

    @pl.when(i == 0)
    def _():
        issue(p0c_ref, p1c_ref, 0)

    def drain(t, c):
        row_copy(p0c_ref, 0, t, slot).wait()
        row_copy(p1c_ref, 1, t, slot).wait()
        return c
    lax.fori_loop(0, TM_COMBINE, drain, 0, unroll=8)

    def compute(s, prefetch):
        rw = rw_ref[...]
        x2 = x1_ref[...] + rw[:, 0:1] * buf_ref[s, 0] + rw[:, 1:2] * buf_ref[s, 1]
        if prefetch:
            issue(p0n_ref, p1n_ref, 1 - s)
        x2_ref[...] = x2
        h_ref[...] = _rms(x2, gn_ref[...]).astype(h_ref.dtype)

    for s in range(2):
        @pl.when(jnp.logical_and(i + 1 < n_steps, slot == s))
        def _():
            compute(s, True)

        @pl.when(jnp.logical_and(i + 1 == n_steps, slot == s))
        def _():
            compute(s, False)


def _combine_call(x1, route_w, pos0, pos1, ys, gn_row):
    t = x1.shape[0]
    n_steps = t // TM_COMBINE

    def nxt(i):
        return (jnp.minimum(i + 1, n_steps - 1),)

    smem = functools.partial(pl.BlockSpec, (TM_COMBINE,), memory_space=pltpu.SMEM)
    return pl.pallas_call(
        functools.partial(_combine_body, n_steps=n_steps),
        grid=(n_steps,),
        in_specs=[smem(lambda i: (i,)), smem(lambda i: (i,)), smem(nxt), smem(nxt),
                  pl.BlockSpec((TM_COMBINE, D_MODEL), lambda i: (i, 0)),
                  pl.BlockSpec((TM_COMBINE, LANES), lambda i: (i, 0)),
                  pl.BlockSpec((1, D_MODEL), lambda i: (0, 0)),
                  pl.BlockSpec(memory_space=pl.ANY)],
        out_specs=[pl.BlockSpec((TM_COMBINE, D_MODEL), lambda i: (i, 0)),
                   pl.BlockSpec((TM_COMBINE, D_MODEL), lambda i: (i, 0))],
        out_shape=[jax.ShapeDtypeStruct((t, D_MODEL), F32),
                   jax.ShapeDtypeStruct((t, D_MODEL), BF16)],
        scratch_shapes=[pltpu.VMEM((2, TOP_K, TM_COMBINE, D_MODEL), F32),
                        pltpu.SemaphoreType.DMA((2,))],
        compiler_params=_params(("arbitrary",)),
        name="combine",
    )(pos0, pos1, pos0, pos1, x1, route_w, gn_row, ys)


def kernel(x, mem, norm_mix, norm_mem, norm_ffn, w_in, v_gain, w_spatial, b_spatial,
           q_gain_b, k_gain_b, sinks, q_gain_c, k_gain_c, w_mem_kv, w_branch, w_out,
           w_router_group, b_router_group, w_router_expert, b_router_expert,
           w_gate_e, w_up_e, w_down_e):
    b, s, d = x.shape
    t = b * s
    assert d == D_MODEL and s % BLOCK == 0 and mem.shape[1] == MEM_LEN
    assert t % TK_INVERT == 0 and t % TM_PROJ == 0
    n_tiles = (t * TOP_K) // TM_EXPERT + N_EXPERTS
    n_rows = n_tiles * TM_EXPERT

    xf = x.reshape(t, d)
    mem2d = mem.reshape(b * MEM_LEN, d)
    pad = LANES - N_GROUPS - N_EXPERTS

    h = _norm_call(xf, norm_mix[0][None])
    for layer in range(DEPTH):
        kvc = _memkv_call(mem2d, norm_mem[layer][None], w_mem_kv, k_gain_c[layer][None], layer)
        pa = _inproj_call(h, w_in, layer, 0, SEC_A_WIDTH, TN_PROJ_A, _gelu, "inproj_a")
        pb = _inproj_call(h, w_in, layer, COL_QB, SEC_B_WIDTH, TN_PROJ_B, _identity, "inproj_b")
        gates = _inproj_call(h, w_in, layer, COL_GATE, SEC_G_WIDTH, TN_PROJ_G, _sigmoid, "inproj_g")
        bs_bcast = jnp.broadcast_to(b_spatial[layer][:, :, None], (A_GROUPS, CHUNK, LANES))
        y = _mixer_call(pa, pb, kvc, v_gain[layer].reshape(1, A_WIDTH), w_spatial, bs_bcast,
                        jnp.tile(q_gain_b[layer], B_HEADS)[None],
                        jnp.tile(k_gain_b[layer], B_KV_HEADS)[None],
                        sinks[layer], jnp.tile(q_gain_c[layer], C_HEADS)[None], layer, s)
        merged = _merge_call(y, gates, w_branch, layer)
        w_router = jnp.pad(jnp.concatenate([w_router_group[layer], w_router_expert[layer]], axis=1),
                           ((0, 0), (0, pad)))
        b_router = jnp.pad(jnp.concatenate([b_router_group[layer], b_router_expert[layer]]),
                           (0, pad))[None]
        x1, logits = _outproj_call(merged, xf, w_out[layer].astype(BF16),
                                   norm_ffn[layer][None], w_router, b_router)
        route_i, route_w, counts = _route_call(logits)
        pos0, pos1, tile_expert, n_used, pad_lo, pad_hi = _plan(route_i, counts, n_tiles)
        src = _invert_call(pos0, pos1, pad_lo, pad_hi, n_rows)
        ys = _expert_call(x1, src, tile_expert, n_used, norm_ffn[layer][None], w_gate_e, w_up_e,
                          w_down_e, layer)
        gn_next = norm_mix[layer + 1] if layer + 1 < DEPTH else norm_mix[layer]
        xf, h = _combine_call(x1, route_w, pos0, pos1, ys, gn_next[None])
    return xf.reshape(b, s, d)
```

```python
import functools

import jax
import jax.numpy as jnp
from jax import lax
from jax.experimental import pallas as pl
from jax.experimental.pallas import tpu as pltpu

F32 = jnp.float32
BF16 = jnp.bfloat16
I32 = jnp.int32

D_MODEL = 2048
DEPTH = 2
MEM_LEN = 256
EPS = 1e-6
BRANCH_WIDTH = 1024
N_BRANCH = 3

CHUNK = 128
A_GROUPS = 8
A_GROUP_DIM = 128
A_WIDTH = A_GROUPS * A_GROUP_DIM

BLOCK = 128
B_HEADS = 16
B_KV_HEADS = 4
B_REP = B_HEADS // B_KV_HEADS
B_HEAD_DIM = 64
B_Q_WIDTH = B_HEADS * B_HEAD_DIM
B_KV_WIDTH = B_KV_HEADS * B_HEAD_DIM

C_HEADS = 4
C_HEAD_DIM = 256
C_WIDTH = C_HEADS * C_HEAD_DIM

COL_U = 0
COL_V = A_WIDTH
COL_QB = 2 * A_WIDTH
COL_KB = COL_QB + B_Q_WIDTH
COL_VB = COL_KB + B_KV_WIDTH
COL_QC = COL_VB + B_KV_WIDTH
COL_GATE = COL_QC + C_WIDTH
IN_WIDTH = COL_GATE + N_BRANCH * D_MODEL
SEC_A_WIDTH = COL_QB
SEC_B_WIDTH = COL_GATE - COL_QB
SEC_G_WIDTH = N_BRANCH * D_MODEL
PB_Q = 0
PB_K = COL_KB - COL_QB
PB_V = COL_VB - COL_QB
PB_QC = COL_QC - COL_QB

N_GROUPS = 4
EXPERTS_PER_GROUP = 4
N_EXPERTS = N_GROUPS * EXPERTS_PER_GROUP
TOP_K = 2
D_FF_EXPERT = 512

LANES = 128
VMEM_LIMIT = 56 * 1024 * 1024
SQRT_HALF = 0.7071067811865476

TM_NORM = 512
TM_PROJ = 1024
TN_PROJ_A, TN_PROJ_B, TN_PROJ_G = 1024, 512, 1536
TM_MERGE, TN_MERGE = 512, 1024
TM_OUT = 512
TM_ROUTE = 512
TK_INVERT = 2048
TM_EXPERT = 256
TM_COMBINE = 256


def _params(sem):
    return pltpu.CompilerParams(dimension_semantics=sem, vmem_limit_bytes=VMEM_LIMIT)


def _rms(x, gain):
    ms = jnp.mean(x * x, axis=-1, keepdims=True)
    return x * lax.rsqrt(ms + EPS) * gain


def _norm_body(x_ref, g_ref, o_ref):
    o_ref[...] = _rms(x_ref[...], g_ref[...]).astype(o_ref.dtype)


def _norm_call(x, gain_row):
    t = x.shape[0]
    return pl.pallas_call(
        _norm_body,
        grid=(t // TM_NORM,),
        in_specs=[pl.BlockSpec((TM_NORM, D_MODEL), lambda i: (i, 0)),
                  pl.BlockSpec((1, D_MODEL), lambda i: (0, 0))],
        out_specs=pl.BlockSpec((TM_NORM, D_MODEL), lambda i: (i, 0)),
        out_shape=jax.ShapeDtypeStruct((t, D_MODEL), BF16),
        compiler_params=_params(("arbitrary",)),
        name="norm",
    )(x, gain_row)


TN_MEMKV = 512


def _memkv_body(mem_ref, gm_ref, w_ref, gk_ref, o_ref):
    n = pl.program_id(0)
    h = _rms(mem_ref[...], gm_ref[...]).astype(BF16)
    acc = jnp.dot(h, w_ref[...].astype(BF16), preferred_element_type=F32)

    @pl.when(n < C_WIDTH // TN_MEMKV)
    def _():
        for j in range(TN_MEMKV // C_HEAD_DIM):
            sl = slice(j * C_HEAD_DIM, (j + 1) * C_HEAD_DIM)
            o_ref[:, sl] = _rms(acc[:, sl], gk_ref[...]).astype(o_ref.dtype)

    @pl.when(n >= C_WIDTH // TN_MEMKV)
    def _():
        o_ref[...] = acc.astype(o_ref.dtype)


def _memkv_call(mem2d, gm_row, w_mem_kv, gk_row, layer):
    rows = mem2d.shape[0]
    return pl.pallas_call(
        _memkv_body,
        grid=(2 * C_WIDTH // TN_MEMKV,),
        in_specs=[pl.BlockSpec((rows, D_MODEL), lambda n: (0, 0)),
                  pl.BlockSpec((1, D_MODEL), lambda n: (0, 0)),
                  pl.BlockSpec((None, D_MODEL, TN_MEMKV), lambda n: (layer, 0, n)),
                  pl.BlockSpec((1, C_HEAD_DIM), lambda n: (0, 0))],
        out_specs=pl.BlockSpec((rows, TN_MEMKV), lambda n: (0, n)),
        out_shape=jax.ShapeDtypeStruct((rows, 2 * C_WIDTH), BF16),
        compiler_params=_params(("arbitrary",)),
        name="memkv",
    )(mem2d, gm_row, w_mem_kv, gk_row)


def _gelu(x):
    return 0.5 * x * (1.0 + lax.erf(x * SQRT_HALF))


def _sigmoid(x):
    return 1.0 / (1.0 + jnp.exp(-x))


def _identity(x):
    return x


def _inproj_body(h_ref, w_ref, o_ref, wb_ref, *, act):
    @pl.when(pl.program_id(1) == 0)
    def _():
        wb_ref[...] = w_ref[...].astype(BF16)

    acc = jnp.dot(h_ref[...], wb_ref[...], preferred_element_type=F32)
    o_ref[...] = act(acc).astype(o_ref.dtype)


def _inproj_call(h, w_in, layer, col0, width, tn, act, name):
    t = h.shape[0]
    assert col0 % tn == 0 and width % tn == 0
    n0 = col0 // tn
    return pl.pallas_call(
        functools.partial(_inproj_body, act=act),
        grid=(width // tn, t // TM_PROJ),
        in_specs=[pl.BlockSpec((TM_PROJ, D_MODEL), lambda n, m: (m, 0)),
                  pl.BlockSpec((None, D_MODEL, tn), lambda n, m: (layer, 0, n0 + n))],
        out_specs=pl.BlockSpec((TM_PROJ, tn), lambda n, m: (m, n)),
        out_shape=jax.ShapeDtypeStruct((t, width), BF16),
        scratch_shapes=[pltpu.VMEM((D_MODEL, tn), BF16)],
        compiler_params=_params(("arbitrary", "arbitrary")),
        name=name,
    )(h, w_in)


def _group_sums(width, group):
    gid = jnp.arange(width, dtype=I32) // group
    s = (gid[:, None] == jnp.arange(LANES, dtype=I32)[None, :]).astype(BF16)
    return s, s.T


def _group_ssq(x, s):
    return jnp.dot((x * x).astype(BF16), s, preferred_element_type=F32)


def _group_bcast(ssq, e, group):
    r = lax.rsqrt(ssq * (1.0 / group) + EPS)
    r_hi = r.astype(BF16)
    r_lo = (r - r_hi.astype(F32)).astype(BF16)
    return (jnp.dot(r_hi, e, preferred_element_type=F32)
            + jnp.dot(r_lo, e, preferred_element_type=F32))


def _mixer_body(a_ref, b_ref, prev_ref, kvc_ref, vgain_ref, ws_ref, bs_ref, qgb_ref, kgb_ref,
                sink_ref, qgc_ref, s64_ref, e64_ref, s128_ref, e128_ref, s256_ref, e256_ref,
                y_ref, sb_ref, *, blocks_per_seq):
    i = pl.program_id(0)
    n = lax.rem(i, blocks_per_seq)
    ones_k = jnp.ones((2 * BLOCK, LANES), BF16)
    row = lax.broadcasted_iota(I32, (CHUNK, CHUNK), 0)
    col = lax.broadcasted_iota(I32, (CHUNK, CHUNK), 1)
    causal = col <= row
    lane_k = lax.broadcasted_iota(I32, (2 * BLOCK, LANES), 1)
    lane_q = lax.broadcasted_iota(I32, (BLOCK, LANES), 1)
    lo_k = lane_k < B_HEAD_DIM
    lo_q = lane_q < B_HEAD_DIM
    qi = lax.broadcasted_iota(I32, (B_REP * BLOCK, 2 * BLOCK), 0) & (BLOCK - 1)
    kj = lax.broadcasted_iota(I32, (B_REP * BLOCK, 2 * BLOCK), 1)
    rel = BLOCK + qi - kj
    valid = (rel >= 0) & (rel < BLOCK) & ((n > 0) | (kj >= BLOCK))

    v_all = a_ref[:, COL_V:COL_V + A_WIDTH].astype(F32)
    q_all = b_ref[:, PB_Q:PB_Q + B_Q_WIDTH].astype(F32)
    k_all = jnp.concatenate([prev_ref[:, 0:B_KV_WIDTH], b_ref[:, PB_K:PB_K + B_KV_WIDTH]],
                            axis=0).astype(F32)
    qc_all = b_ref[:, PB_QC:PB_QC + C_WIDTH].astype(F32)
    ssq_q = _group_ssq(q_all, s64_ref[...])
    ssq_k = _group_ssq(k_all, s64_ref[0:B_KV_WIDTH, :])
    ssq_c = _group_ssq(qc_all, s256_ref[...])
    ssq_v = _group_ssq(v_all, s128_ref[...])
    qn_all = (q_all * _group_bcast(ssq_q, e64_ref[...], B_HEAD_DIM)
              * (qgb_ref[...] * (B_HEAD_DIM ** -0.5)))
    kn_all = k_all * _group_bcast(ssq_k, e64_ref[:, 0:B_KV_WIDTH], B_HEAD_DIM) * kgb_ref[...]
    qcn_all = (qc_all * _group_bcast(ssq_c, e256_ref[...], C_HEAD_DIM)
               * (qgc_ref[...] * (C_HEAD_DIM ** -0.5))).astype(BF16)
    vn_all = (v_all * _group_bcast(ssq_v, e128_ref[...], A_GROUP_DIM)
              * vgain_ref[...]).astype(BF16)
    v_kv = jnp.concatenate([prev_ref[:, B_KV_WIDTH:2 * B_KV_WIDTH],
                            b_ref[:, PB_V:PB_V + B_KV_WIDTH]], axis=0).astype(F32)

    scores_c = [
        lax.dot_general(qcn_all[:, h * C_HEAD_DIM:(h + 1) * C_HEAD_DIM],
                        kvc_ref[:, h * C_HEAD_DIM:(h + 1) * C_HEAD_DIM],
                        (((1,), (1,)), ((), ())), preferred_element_type=F32)
        for h in range(C_HEADS)]
    vdups, sinks = [], []
    for slab in range(B_KV_WIDTH // LANES):
        ks = kn_all[:, slab * LANES:(slab + 1) * LANES]
        vs = v_kv[:, slab * LANES:(slab + 1) * LANES]
        kr = pltpu.roll(ks, B_HEAD_DIM, axis=1)
        vr = pltpu.roll(vs, B_HEAD_DIM, axis=1)
        for par in range(2):
            kv = 2 * slab + par
            kdup = (jnp.where(lo_k, ks, kr) if par == 0 else jnp.where(lo_k, kr, ks)).astype(BF16)
            vdups.append((jnp.where(lo_k, vs, vr) if par == 0
                          else jnp.where(lo_k, vr, vs)).astype(BF16))
            qs, sk = [], []
            for r in range(B_REP):
                h = kv * B_REP + r
                q_slab = qn_all[:, (h // 2) * LANES:(h // 2 + 1) * LANES]
                keep = lo_q if h % 2 == 0 else jnp.logical_not(lo_q)
                qs.append(jnp.where(keep, q_slab, 0.0).astype(BF16))
                sk.append(jnp.full((BLOCK, LANES), sink_ref[h], F32))
            sinks.append(jnp.concatenate(sk, axis=0))
            sc = lax.dot_general(jnp.concatenate(qs, axis=0), kdup, (((1,), (1,)), ((), ())),
                                 preferred_element_type=F32)
            sb_ref[kv] = jnp.where(valid, sc, -jnp.inf)

    probs_c = [jnp.exp(s - jnp.max(s, axis=-1, keepdims=True)).astype(BF16) for s in scores_c]
    mixed_a = [
        jnp.dot(jnp.where(causal, ws_ref[g], 0.0).astype(BF16),
                vn_all[:, g * A_GROUP_DIM:(g + 1) * A_GROUP_DIM],
                preferred_element_type=F32) + bs_ref[g]
        for g in range(A_GROUPS)]
    maxes_b = [jnp.maximum(jnp.max(sb_ref[kv], axis=-1, keepdims=True), sinks[kv])
               for kv in range(B_KV_HEADS)]
    probs_b = [jnp.exp(sb_ref[kv] - jnp.concatenate([maxes_b[kv]] * (2 * BLOCK // LANES), axis=1)
                       ).astype(BF16) for kv in range(B_KV_HEADS)]
    sink_terms = [jnp.exp(sinks[kv] - maxes_b[kv]) for kv in range(B_KV_HEADS)]

    outs_c = [(jnp.dot(p, kvc_ref[:, C_WIDTH + h * C_HEAD_DIM:C_WIDTH + (h + 1) * C_HEAD_DIM],
                       preferred_element_type=F32),
               jnp.dot(p, ones_k, preferred_element_type=F32))
              for h, p in enumerate(probs_c)]
    for g, mixed in enumerate(mixed_a):
        sl = slice(g * A_GROUP_DIM, (g + 1) * A_GROUP_DIM)
        u = a_ref[:, COL_U + g * A_GROUP_DIM:COL_U + (g + 1) * A_GROUP_DIM].astype(F32)
        y_ref[:, sl] = (u * mixed).astype(y_ref.dtype)
    outs_b = [(jnp.dot(p, vdup, preferred_element_type=F32),
               jnp.dot(p, ones_k, preferred_element_type=F32) + st)
              for p, vdup, st in zip(probs_b, vdups, sink_terms)]

    for h, (o, denom) in enumerate(outs_c):
        o = o / jnp.concatenate([denom] * (C_HEAD_DIM // LANES), axis=1)
        y_ref[:, A_WIDTH + B_Q_WIDTH + h * C_HEAD_DIM:
              A_WIDTH + B_Q_WIDTH + (h + 1) * C_HEAD_DIM] = o.astype(y_ref.dtype)
    for kv, (o, denom) in enumerate(outs_b):
        o = o / denom
        for pr in range(B_REP // 2):
            even = o[(2 * pr) * BLOCK:(2 * pr + 1) * BLOCK]
            odd = o[(2 * pr + 1) * BLOCK:(2 * pr + 2) * BLOCK]
            c0 = A_WIDTH + (kv * B_REP // 2 + pr) * LANES
            y_ref[:, c0:c0 + LANES] = jnp.where(lo_q, even, odd).astype(y_ref.dtype)


def _mixer_call(pa, pb, kvc, vgain_row, w_spatial, bs_bcast, qgb_row, kgb_row, sinks, qgc_row,
                layer, seq):
    t = pa.shape[0]
    bps = seq // BLOCK
    kvb = PB_K // (2 * B_KV_WIDTH)
    s64, e64 = _group_sums(B_Q_WIDTH, B_HEAD_DIM)
    s128, e128 = _group_sums(A_WIDTH, A_GROUP_DIM)
    s256, e256 = _group_sums(C_WIDTH, C_HEAD_DIM)

    def const(shape):
        return pl.BlockSpec(shape, lambda i: (0,) * len(shape))

    return pl.pallas_call(
        functools.partial(_mixer_body, blocks_per_seq=bps),
        grid=(t // BLOCK,),
        in_specs=[
            pl.BlockSpec((BLOCK, SEC_A_WIDTH), lambda i: (i, 0)),
            pl.BlockSpec((BLOCK, SEC_B_WIDTH), lambda i: (i, 0)),
            pl.BlockSpec((BLOCK, 2 * B_KV_WIDTH), lambda i: (jnp.maximum(i - 1, 0), kvb)),
            pl.BlockSpec((MEM_LEN, 2 * C_WIDTH), lambda i: (i // bps, 0)),
            const((1, A_WIDTH)),
            pl.BlockSpec((None, A_GROUPS, CHUNK, CHUNK), lambda i: (layer, 0, 0, 0)),
            const((A_GROUPS, CHUNK, LANES)),
            const((1, B_Q_WIDTH)),
            const((1, B_KV_WIDTH)),
            pl.BlockSpec(memory_space=pltpu.SMEM),
            const((1, C_WIDTH)),
            const((B_Q_WIDTH, LANES)), const((LANES, B_Q_WIDTH)),
            const((A_WIDTH, LANES)), const((LANES, A_WIDTH)),
            const((C_WIDTH, LANES)), const((LANES, C_WIDTH)),
        ],
        out_specs=pl.BlockSpec((BLOCK, N_BRANCH * BRANCH_WIDTH), lambda i: (i, 0)),
        out_shape=jax.ShapeDtypeStruct((t, N_BRANCH * BRANCH_WIDTH), BF16),
        scratch_shapes=[pltpu.VMEM((B_KV_HEADS, B_REP * BLOCK, 2 * BLOCK), F32)],
        compiler_params=_params(("arbitrary",)),
        name="mixer",
    )(pa, pb, pb, kvc, vgain_row, w_spatial, bs_bcast, qgb_row, kgb_row, sinks, qgc_row,
      s64, e64, s128, e128, s256, e256)


def _merge_body(y_ref, g0_ref, g1_ref, g2_ref, w_ref, o_ref, wb_ref):
    m = pl.program_id(1)

    @pl.when(m == 0)
    def _():
        wb_ref[...] = w_ref[...].astype(BF16)

    acc = None
    for b, g_ref in enumerate((g0_ref, g1_ref, g2_ref)):
        wide = jnp.dot(y_ref[:, b * BRANCH_WIDTH:(b + 1) * BRANCH_WIDTH], wb_ref[b],
                       preferred_element_type=F32)
        term = g_ref[...].astype(F32) * wide
        acc = term if acc is None else acc + term
    o_ref[...] = acc.astype(o_ref.dtype)


def _merge_call(y, gates, w_branch, layer):
    t = y.shape[0]
    per = D_MODEL // TN_MERGE

    def gate_spec(b):
        return pl.BlockSpec((TM_MERGE, TN_MERGE), lambda n, m: (m, b * per + n))

    return pl.pallas_call(
        _merge_body,
        grid=(D_MODEL // TN_MERGE, t // TM_MERGE),
        in_specs=[pl.BlockSpec((TM_MERGE, N_BRANCH * BRANCH_WIDTH), lambda n, m: (m, 0)),
                  gate_spec(0), gate_spec(1), gate_spec(2),
                  pl.BlockSpec((None, N_BRANCH, BRANCH_WIDTH, TN_MERGE),
                               lambda n, m: (layer, 0, 0, n))],
        out_specs=pl.BlockSpec((TM_MERGE, TN_MERGE), lambda n, m: (m, n)),
        out_shape=jax.ShapeDtypeStruct((t, D_MODEL), BF16),
        scratch_shapes=[pltpu.VMEM((N_BRANCH, BRANCH_WIDTH, TN_MERGE), BF16)],
        compiler_params=_params(("arbitrary", "arbitrary")),
        name="merge",
    )(y, gates, gates, gates, w_branch)


def _outproj_body(mg_ref, x_ref, w_ref, gn_ref, wr_ref, br_ref, x1_ref, lg_ref,
                  wb_ref, wr2_ref):
    i = pl.program_id(0)

    @pl.when(i == 0)
    def _():
        wb_ref[...] = w_ref[...].astype(BF16)
        wr = wr_ref[...]
        hi = wr.astype(BF16)
        wr2_ref[:, 0:LANES] = hi
        wr2_ref[:, LANES:2 * LANES] = (wr - hi.astype(F32)).astype(BF16)

    x1 = x_ref[...] + jnp.dot(mg_ref[...], wb_ref[...], preferred_element_type=F32)
    x1_ref[...] = x1
    h2 = _rms(x1, gn_ref[...])
    hi = h2.astype(BF16)
    lo = (h2 - hi.astype(F32)).astype(BF16)
    hi_w = jnp.dot(hi, wr2_ref[...], preferred_element_type=F32)
    lo_w = jnp.dot(lo, wr2_ref[...], preferred_element_type=F32)
    lg_ref[...] = hi_w[:, 0:LANES] + lo_w[:, 0:LANES] + hi_w[:, LANES:2 * LANES] + br_ref[...]


def _outproj_call(merged, x, w_out, gn_row, w_router, b_router, layer):
    t = x.shape[0]
    return pl.pallas_call(
        _outproj_body,
        grid=(t // TM_OUT,),
        in_specs=[pl.BlockSpec((TM_OUT, D_MODEL), lambda i: (i, 0)),
                  pl.BlockSpec((TM_OUT, D_MODEL), lambda i: (i, 0)),
                  pl.BlockSpec((None, D_MODEL, D_MODEL), lambda i: (layer, 0, 0),
                               pipeline_mode=pl.Buffered(1)),
                  pl.BlockSpec((1, D_MODEL), lambda i: (0, 0)),
                  pl.BlockSpec((D_MODEL, LANES), lambda i: (0, 0)),
                  pl.BlockSpec((1, LANES), lambda i: (0, 0))],
        out_specs=[pl.BlockSpec((TM_OUT, D_MODEL), lambda i: (i, 0)),
                   pl.BlockSpec((TM_OUT, LANES), lambda i: (i, 0))],
        out_shape=[jax.ShapeDtypeStruct((t, D_MODEL), F32),
                   jax.ShapeDtypeStruct((t, LANES), F32)],
        scratch_shapes=[pltpu.VMEM((D_MODEL, D_MODEL), BF16),
                        pltpu.VMEM((D_MODEL, 2 * LANES), BF16)],
        compiler_params=_params(("arbitrary",)),
        name="outproj",
    )(merged, x, w_out, gn_row, w_router, b_router)


def _route_body(lg_ref, ri_ref, rw_ref, cnt_ref, carry_ref):
    i = pl.program_id(0)

    @pl.when(i == 0)
    def _():
        carry_ref[...] = jnp.zeros_like(carry_ref)

    lg = lg_ref[...]
    tm = lg.shape[0]
    lane = lax.broadcasted_iota(I32, lg.shape, 1)
    neg = -jnp.inf
    big = jnp.int32(LANES)

    is_g = lane < N_GROUPS
    gl = jnp.where(is_g, lg, neg)
    gmax = jnp.max(gl, axis=-1, keepdims=True)
    gidx = jnp.min(jnp.where(gl == gmax, lane, big), axis=-1, keepdims=True)
    p_g = 1.0 / jnp.sum(jnp.where(is_g, jnp.exp(lg - gmax), 0.0), axis=-1, keepdims=True)

    e_lane = lane - N_GROUPS
    in_group = (e_lane >= 0) & (e_lane < N_EXPERTS) & ((e_lane >> 2) == gidx)
    el = jnp.where(in_group, lg, neg)
    v0 = jnp.max(el, axis=-1, keepdims=True)
    i0 = jnp.min(jnp.where(el == v0, lane, big), axis=-1, keepdims=True)
    el1 = jnp.where(lane == i0, neg, el)
    v1 = jnp.max(el1, axis=-1, keepdims=True)
    i1 = jnp.min(jnp.where(el1 == v1, lane, big), axis=-1, keepdims=True)
    e0 = i0 - N_GROUPS
    e1 = i1 - N_GROUPS
    ex = jnp.exp(v1 - v0)
    w0 = p_g / (1.0 + ex)
    w1 = p_g * ex / (1.0 + ex)

    oh0 = lane == e0
    oh1 = lane == e1
    oh = jnp.where(oh0 | oh1, 1.0, 0.0)
    r_i = lax.broadcasted_iota(I32, (tm, tm), 0)
    c_i = lax.broadcasted_iota(I32, (tm, tm), 1)
    lower = jnp.where(c_i < r_i, 1.0, 0.0).astype(BF16)
    before = jnp.dot(lower, oh.astype(BF16), preferred_element_type=F32) + carry_ref[...]
    rank0 = jnp.sum(jnp.where(oh0, before, 0.0), axis=-1, keepdims=True).astype(I32)
    rank1 = jnp.sum(jnp.where(oh1, before, 0.0), axis=-1, keepdims=True).astype(I32)
    carry_ref[...] = carry_ref[...] + jnp.sum(oh, axis=0, keepdims=True)

    zero_i = jnp.zeros_like(lane)
    ri_ref[...] = jnp.where(lane == 0, e0, jnp.where(lane == 1, e1,
                            jnp.where(lane == 2, rank0, jnp.where(lane == 3, rank1, zero_i))))
    rw_ref[...] = jnp.where(lane == 0, w0, jnp.where(lane == 1, w1, 0.0))
    cnt_ref[...] = jnp.broadcast_to(carry_ref[...], cnt_ref.shape).astype(I32)


def _route_call(logits):
    t = logits.shape[0]
    return pl.pallas_call(
        _route_body,
        grid=(t // TM_ROUTE,),
        in_specs=[pl.BlockSpec((TM_ROUTE, LANES), lambda i: (i, 0))],
        out_specs=[pl.BlockSpec((TM_ROUTE, LANES), lambda i: (i, 0)),
                   pl.BlockSpec((TM_ROUTE, LANES), lambda i: (i, 0)),
                   pl.BlockSpec((8, LANES), lambda i: (0, 0))],
        out_shape=[jax.ShapeDtypeStruct((t, LANES), I32),
                   jax.ShapeDtypeStruct((t, LANES), F32),
                   jax.ShapeDtypeStruct((8, LANES), I32)],
        scratch_shapes=[pltpu.VMEM((1, LANES), F32)],
        compiler_params=_params(("arbitrary",)),
        name="route",
    )(logits)


def _plan(route_i, counts, n_tiles):
    c = counts[0, :N_EXPERTS]
    tiles = (c + TM_EXPERT - 1) // TM_EXPERT
    ctiles = jnp.cumsum(tiles)
    start = (ctiles - tiles) * TM_EXPERT
    eid = jnp.arange(N_EXPERTS, dtype=I32)

    def slot(e, r):
        return jnp.sum(jnp.where(e[:, None] == eid[None, :], start[None, :], 0), axis=1) + r

    pos0 = slot(route_i[:, 0], route_i[:, 2]).astype(I32)
    pos1 = slot(route_i[:, 1], route_i[:, 3]).astype(I32)
    n_used = ctiles[-1]
    tid = jnp.minimum(jnp.arange(n_tiles, dtype=I32), n_used - 1)
    tile_expert = jnp.sum(tid[:, None] >= ctiles[None, :], axis=1).astype(I32)
    used_rows = n_used * TM_EXPERT
    pad_lo = jnp.concatenate([start + c, used_rows[None]]).astype(I32)
    pad_hi = jnp.concatenate([start + tiles * TM_EXPERT,
                              jnp.full((1,), n_tiles * TM_EXPERT, I32)]).astype(I32)
    return pos0, pos1, tile_expert, n_used.reshape(1).astype(I32), pad_lo, pad_hi


def _invert_body(p0_ref, p1_ref, lo_ref, hi_ref, src_ref):
    i = pl.program_id(0)

    @pl.when(i == 0)
    def _():
        def fill(r, c):
            src_ref[r] = 0
            return c
        for j in range(N_EXPERTS + 1):
            lax.fori_loop(lo_ref[j], hi_ref[j], fill, 0)

    base = i * TK_INVERT

    def scatter(t, c):
        src_ref[p0_ref[t]] = base + t
        src_ref[p1_ref[t]] = base + t
        return c
    lax.fori_loop(0, TK_INVERT, scatter, 0, unroll=8)


def _invert_call(pos0, pos1, pad_lo, pad_hi, n_rows):
    t = pos0.shape[0]
    smem = pl.BlockSpec((TK_INVERT,), lambda i: (i,), memory_space=pltpu.SMEM)
    whole = pl.BlockSpec(memory_space=pltpu.SMEM)
    return pl.pallas_call(
        _invert_body,
        grid=(t // TK_INVERT,),
        in_specs=[smem, smem, whole, whole],
        out_specs=pl.BlockSpec(memory_space=pltpu.SMEM),
        out_shape=jax.ShapeDtypeStruct((n_rows,), I32),
        compiler_params=_params(("arbitrary",)),
        name="invert",
    )(pos0, pos1, pad_lo, pad_hi)


def _expert_body(te_ref, nu_ref, srcc_ref, srcn_ref, x_hbm, gn_ref, wg_ref, wu_ref, wd_ref,
                 y_ref, xbuf_ref, wgb_ref, wub_ref, wdb_ref, sems):
    i = pl.program_id(0)
    n_used = nu_ref[0]
    slot = lax.rem(i, 2)

    def row_copy(src_ref, r, s):
        return pltpu.make_async_copy(x_hbm.at[pl.ds(src_ref[r], 1)],
                                     xbuf_ref.at[s, pl.ds(r, 1)], sems.at[s])

    def issue(src_ref, s):
        for r in range(TM_EXPERT):
            row_copy(src_ref, r, s).start(priority=r % 2)

    @pl.when(i == 0)
    def _():
        issue(srcc_ref, 0)

    @pl.when(i < n_used)
    def _():
        prev = te_ref[jnp.maximum(i - 1, 0)]

        @pl.when(jnp.logical_or(i == 0, te_ref[i] != prev))
        def _():
            wgb_ref[...] = wg_ref[...].astype(BF16)
            wub_ref[...] = wu_ref[...].astype(BF16)
            wdb_ref[...] = wd_ref[...].astype(BF16)

        def drain(r, c):
            row_copy(srcc_ref, r, slot).wait()
            return c
        lax.fori_loop(0, TM_EXPERT, drain, 0, unroll=8)

    def compute(s, prefetch):
        h = _rms(xbuf_ref[s], gn_ref[...]).astype(BF16)
        if prefetch:
            issue(srcn_ref, 1 - s)
        g = jnp.dot(h, wgb_ref[...], preferred_element_type=F32)
        u = jnp.dot(h, wub_ref[...], preferred_element_type=F32)
        hid = (g * _sigmoid(g) * u).astype(BF16)
        y_ref[...] = jnp.dot(hid, wdb_ref[...], preferred_element_type=F32)

    for s in range(2):
        @pl.when(jnp.logical_and(i + 1 < n_used, slot == s))
        def _():
            compute(s, True)

        @pl.when(jnp.logical_and(i + 1 == n_used, slot == s))
        def _():
            compute(s, False)

    @pl.when(i >= n_used)
    def _():
        y_ref[...] = jnp.zeros_like(y_ref)


def _expert_call(x1, src, tile_expert, n_used, gn_row, w_gate, w_up, w_down, layer):
    n_rows = src.shape[0]
    n_tiles = n_rows // TM_EXPERT

    def w_map(i, te, nu):
        return (layer, te[i], 0, 0)

    grid_spec = pltpu.PrefetchScalarGridSpec(
        num_scalar_prefetch=2,
        grid=(n_tiles,),
        in_specs=[pl.BlockSpec((TM_EXPERT,), lambda i, te, nu: (i,), memory_space=pltpu.SMEM),
                  pl.BlockSpec((TM_EXPERT,), lambda i, te, nu: (jnp.minimum(i + 1, n_tiles - 1),),
                               memory_space=pltpu.SMEM),
                  pl.BlockSpec(memory_space=pl.ANY),
                  pl.BlockSpec((1, D_MODEL), lambda i, te, nu: (0, 0)),
                  pl.BlockSpec((None, None, D_MODEL, D_FF_EXPERT), w_map),
                  pl.BlockSpec((None, None, D_MODEL, D_FF_EXPERT), w_map),
                  pl.BlockSpec((None, None, D_FF_EXPERT, D_MODEL), w_map)],
        out_specs=pl.BlockSpec((TM_EXPERT, D_MODEL), lambda i, te, nu: (i, 0)),
        scratch_shapes=[pltpu.VMEM((2, TM_EXPERT, D_MODEL), F32),
                        pltpu.VMEM((D_MODEL, D_FF_EXPERT), BF16),
                        pltpu.VMEM((D_MODEL, D_FF_EXPERT), BF16),
                        pltpu.VMEM((D_FF_EXPERT, D_MODEL), BF16),
                        pltpu.SemaphoreType.DMA((2,))],
    )
    return pl.pallas_call(
        _expert_body,
        grid_spec=grid_spec,
        out_shape=jax.ShapeDtypeStruct((n_rows, D_MODEL), F32),
        compiler_params=_params(("arbitrary",)),
        name="experts",
    )(tile_expert, n_used, src, src, x1, gn_row, w_gate, w_up, w_down)


def _combine_body(p0c_ref, p1c_ref, p0n_ref, p1n_ref, x1_ref, rw_ref, gn_ref, ys_hbm,
                  x2_ref, h_ref, buf_ref, sems, *, n_steps):
    i = pl.program_id(0)
    slot = lax.rem(i, 2)

    def row_copy(p_ref, k, t, s):
        return pltpu.make_async_copy(ys_hbm.at[pl.ds(p_ref[t], 1)],
                                     buf_ref.at[s, k, pl.ds(t, 1)], sems.at[s])

    def issue(p0_ref, p1_ref, s):
        for t in range(TM_COMBINE):
            row_copy(p0_ref, 0, t, s).start(priority=0)
            row_copy(p1_ref, 1, t, s).start(priority=1)

    @pl.when(i == 0)
    def _():
        issue(p0c_ref, p1c_ref, 0)

    def drain(t, c):
        row_copy(p0c_ref, 0, t, slot).wait()
        row_copy(p1c_ref, 1, t, slot).wait()
        return c
    lax.fori_loop(0, TM_COMBINE, drain, 0, unroll=8)

    def compute(s, prefetch):
        rw = rw_ref[...]
        x2 = x1_ref[...] + rw[:, 0:1] * buf_ref[s, 0] + rw[:, 1:2] * buf_ref[s, 1]
        if prefetch:
            issue(p0n_ref, p1n_ref, 1 - s)
        x2_ref[...] = x2
        h_ref[...] = _rms(x2, gn_ref[...]).astype(h_ref.dtype)

    for s in range(2):
        @pl.when(jnp.logical_and(i + 1 < n_steps, slot == s))
        def _():
            compute(s, True)

        @pl.when(jnp.logical_and(i + 1 == n_steps, slot == s))
        def _():
            compute(s, False)


def _combine_call(x1, route_w, pos0, pos1, ys, gn_row):
    t = x1.shape[0]
    n_steps = t // TM_COMBINE

    def nxt(i):
        return (jnp.minimum(i + 1, n_steps - 1),)

    smem = functools.partial(pl.BlockSpec, (TM_COMBINE,), memory_space=pltpu.SMEM)
    return pl.pallas_call(
        functools.partial(_combine_body, n_steps=n_steps),
        grid=(n_steps,),
        in_specs=[smem(lambda i: (i,)), smem(lambda i: (i,)), smem(nxt), smem(nxt),
                  pl.BlockSpec((TM_COMBINE, D_MODEL), lambda i: (i, 0)),
                  pl.BlockSpec((TM_COMBINE, LANES), lambda i: (i, 0)),
                  pl.BlockSpec((1, D_MODEL), lambda i: (0, 0)),
                  pl.BlockSpec(memory_space=pl.ANY)],
        out_specs=[pl.BlockSpec((TM_COMBINE, D_MODEL), lambda i: (i, 0)),
                   pl.BlockSpec((TM_COMBINE, D_MODEL), lambda i: (i, 0))],
        out_shape=[jax.ShapeDtypeStruct((t, D_MODEL), F32),
                   jax.ShapeDtypeStruct((t, D_MODEL), BF16)],
        scratch_shapes=[pltpu.VMEM((2, TOP_K, TM_COMBINE, D_MODEL), F32),
                        pltpu.SemaphoreType.DMA((2,))],
        compiler_params=_params(("arbitrary",)),
        name="combine",
    )(pos0, pos1, pos0, pos1, x1, route_w, gn_row, ys)


def kernel(x, mem, norm_mix, norm_mem, norm_ffn, w_in, v_gain, w_spatial, b_spatial,
           q_gain_b, k_gain_b, sinks, q_gain_c, k_gain_c, w_mem_kv, w_branch, w_out,
           w_router_group, b_router_group, w_router_expert, b_router_expert,
           w_gate_e, w_up_e, w_down_e):
    b, s, d = x.shape
    t = b * s
    assert d == D_MODEL and s % BLOCK == 0 and mem.shape[1] == MEM_LEN
    assert t % TK_INVERT == 0 and t % TM_PROJ == 0
    n_tiles = (t * TOP_K) // TM_EXPERT + N_EXPERTS
    n_rows = n_tiles * TM_EXPERT

    xf = x.reshape(t, d)
    mem2d = mem.reshape(b * MEM_LEN, d)
    pad = LANES - N_GROUPS - N_EXPERTS

    h = _norm_call(xf, norm_mix[0][None])
    for layer in range(DEPTH):
        kvc = _memkv_call(mem2d, norm_mem[layer][None], w_mem_kv, k_gain_c[layer][None], layer)
        pa = _inproj_call(h, w_in, layer, 0, SEC_A_WIDTH, TN_PROJ_A, _gelu, "inproj_a")
        pb = _inproj_call(h, w_in, layer, COL_QB, SEC_B_WIDTH, TN_PROJ_B, _identity, "inproj_b")
        gates = _inproj_call(h, w_in, layer, COL_GATE, SEC_G_WIDTH, TN_PROJ_G, _sigmoid, "inproj_g")
        bs_bcast = jnp.broadcast_to(b_spatial[layer][:, :, None], (A_GROUPS, CHUNK, LANES))
        y = _mixer_call(pa, pb, kvc, v_gain[layer].reshape(1, A_WIDTH), w_spatial, bs_bcast,
                        jnp.tile(q_gain_b[layer], B_HEADS)[None],
                        jnp.tile(k_gain_b[layer], B_KV_HEADS)[None],
                        sinks[layer], jnp.tile(q_gain_c[layer], C_HEADS)[None], layer, s)
        merged = _merge_call(y, gates, w_branch, layer)
        w_router = jnp.pad(jnp.concatenate([w_router_group[layer], w_router_expert[layer]], axis=1),
                           ((0, 0), (0, pad)))
        b_router = jnp.pad(jnp.concatenate([b_router_group[layer], b_router_expert[layer]]),
                           (0, pad))[None]
        x1, logits = _outproj_call(merged, xf, w_out, norm_ffn[layer][None], w_router, b_router,
                                   layer)
        route_i, route_w, counts = _route_call(logits)
        pos0, pos1, tile_expert, n_used, pad_lo, pad_hi = _plan(route_i, counts, n_tiles)
        src = _invert_call(pos0, pos1, pad_lo, pad_hi, n_rows)
        ys = _expert_call(x1, src, tile_expert, n_used, norm_ffn[layer][None], w_gate_e, w_up_e,
                          w_down_e, layer)
        gn_next = norm_mix[layer + 1] if layer + 1 < DEPTH else norm_mix[layer]
        xf, h = _combine_call(x1, route_w, pos0, pos1, ys, gn_next[None])
    return xf.reshape(b, s, d)
```

```python
import functools

import jax
import jax.numpy as jnp
from jax import lax
from jax.experimental import pallas as pl
from jax.experimental.pallas import tpu as pltpu

F32 = jnp.float32
BF16 = jnp.bfloat16
I32 = jnp.int32

D_MODEL = 2048
DEPTH = 2
MEM_LEN = 256
EPS = 1e-6
BRANCH_WIDTH = 1024
N_BRANCH = 3

CHUNK = 128
A_GROUPS = 8
A_GROUP_DIM = 128
A_WIDTH = A_GROUPS * A_GROUP_DIM

BLOCK = 128
B_HEADS = 16
B_KV_HEADS = 4
B_REP = B_HEADS // B_KV_HEADS
B_HEAD_DIM = 64
B_Q_WIDTH = B_HEADS * B_HEAD_DIM
B_KV_WIDTH = B_KV_HEADS * B_HEAD_DIM

C_HEADS = 4
C_HEAD_DIM = 256
C_WIDTH = C_HEADS * C_HEAD_DIM

COL_U = 0
COL_V = A_WIDTH
COL_QB = 2 * A_WIDTH
COL_KB = COL_QB + B_Q_WIDTH
COL_VB = COL_KB + B_KV_WIDTH
COL_QC = COL_VB + B_KV_WIDTH
COL_GATE = COL_QC + C_WIDTH
IN_WIDTH = COL_GATE + N_BRANCH * D_MODEL
SEC_A_WIDTH = COL_QB
SEC_B_WIDTH = COL_GATE - COL_QB
SEC_G_WIDTH = N_BRANCH * D_MODEL
PB_Q = 0
PB_K = COL_KB - COL_QB
PB_V = COL_VB - COL_QB
PB_QC = COL_QC - COL_QB

N_GROUPS = 4
EXPERTS_PER_GROUP = 4
N_EXPERTS = N_GROUPS * EXPERTS_PER_GROUP
TOP_K = 2
D_FF_EXPERT = 512

LANES = 128
VMEM_LIMIT = 56 * 1024 * 1024
SQRT_HALF = 0.7071067811865476

TM_NORM = 512
TM_PROJ = 1024
TN_PROJ_A, TN_PROJ_B, TN_PROJ_G = 1024, 512, 1536
TM_MERGE, TN_MERGE = 512, 1024
TM_OUT = 512
TM_ROUTE = 512
TK_INVERT = 2048
TM_EXPERT = 256
TM_COMBINE = 256


def _params(sem):
    return pltpu.CompilerParams(dimension_semantics=sem, vmem_limit_bytes=VMEM_LIMIT)


def _rms(x, gain):
    ms = jnp.mean(x * x, axis=-1, keepdims=True)
    return x * lax.rsqrt(ms + EPS) * gain


def _norm_body(x_ref, g_ref, o_ref):
    o_ref[...] = _rms(x_ref[...], g_ref[...]).astype(o_ref.dtype)


def _norm_call(x, gain_row):
    t = x.shape[0]
    return pl.pallas_call(
        _norm_body,
        grid=(t // TM_NORM,),
        in_specs=[pl.BlockSpec((TM_NORM, D_MODEL), lambda i: (i, 0)),
                  pl.BlockSpec((1, D_MODEL), lambda i: (0, 0))],
        out_specs=pl.BlockSpec((TM_NORM, D_MODEL), lambda i: (i, 0)),
        out_shape=jax.ShapeDtypeStruct((t, D_MODEL), BF16),
        compiler_params=_params(("arbitrary",)),
        name="norm",
    )(x, gain_row)


TN_MEMKV = 512


def _memkv_body(mem_ref, gm_ref, w_ref, gk_ref, o_ref):
    n = pl.program_id(0)
    h = _rms(mem_ref[...], gm_ref[...]).astype(BF16)
    acc = jnp.dot(h, w_ref[...].astype(BF16), preferred_element_type=F32)

    @pl.when(n < C_WIDTH // TN_MEMKV)
    def _():
        for j in range(TN_MEMKV // C_HEAD_DIM):
            sl = slice(j * C_HEAD_DIM, (j + 1) * C_HEAD_DIM)
            o_ref[:, sl] = _rms(acc[:, sl], gk_ref[...]).astype(o_ref.dtype)

    @pl.when(n >= C_WIDTH // TN_MEMKV)
    def _():
        o_ref[...] = acc.astype(o_ref.dtype)


def _memkv_call(mem2d, gm_row, w_mem_kv, gk_row, layer):
    rows = mem2d.shape[0]
    return pl.pallas_call(
        _memkv_body,
        grid=(2 * C_WIDTH // TN_MEMKV,),
        in_specs=[pl.BlockSpec((rows, D_MODEL), lambda n: (0, 0)),
                  pl.BlockSpec((1, D_MODEL), lambda n: (0, 0)),
                  pl.BlockSpec((None, D_MODEL, TN_MEMKV), lambda n: (layer, 0, n)),
                  pl.BlockSpec((1, C_HEAD_DIM), lambda n: (0, 0))],
        out_specs=pl.BlockSpec((rows, TN_MEMKV), lambda n: (0, n)),
        out_shape=jax.ShapeDtypeStruct((rows, 2 * C_WIDTH), BF16),
        compiler_params=_params(("arbitrary",)),
        name="memkv",
    )(mem2d, gm_row, w_mem_kv, gk_row)


def _gelu(x):
    return 0.5 * x * (1.0 + lax.erf(x * SQRT_HALF))


def _sigmoid(x):
    return 1.0 / (1.0 + jnp.exp(-x))


def _identity(x):
    return x


def _inproj_body(h_ref, w_ref, o_ref, wb_ref, *, act):
    @pl.when(pl.program_id(1) == 0)
    def _():
        wb_ref[...] = w_ref[...].astype(BF16)

    acc = jnp.dot(h_ref[...], wb_ref[...], preferred_element_type=F32)
    o_ref[...] = act(acc).astype(o_ref.dtype)


def _inproj_call(h, w_in, layer, col0, width, tn, act, name):
    t = h.shape[0]
    assert col0 % tn == 0 and width % tn == 0
    n0 = col0 // tn
    return pl.pallas_call(
        functools.partial(_inproj_body, act=act),
        grid=(width // tn, t // TM_PROJ),
        in_specs=[pl.BlockSpec((TM_PROJ, D_MODEL), lambda n, m: (m, 0)),
                  pl.BlockSpec((None, D_MODEL, tn), lambda n, m: (layer, 0, n0 + n))],
        out_specs=pl.BlockSpec((TM_PROJ, tn), lambda n, m: (m, n)),
        out_shape=jax.ShapeDtypeStruct((t, width), BF16),
        scratch_shapes=[pltpu.VMEM((D_MODEL, tn), BF16)],
        compiler_params=_params(("arbitrary", "arbitrary")),
        name=name,
    )(h, w_in)


def _group_sums(width, group):
    gid = jnp.arange(width, dtype=I32) // group
    s = (gid[:, None] == jnp.arange(LANES, dtype=I32)[None, :]).astype(BF16)
    return s, s.T


def _group_ssq(x, s):
    return jnp.dot((x * x).astype(BF16), s, preferred_element_type=F32)


def _group_bcast(ssq, e, group):
    r = lax.rsqrt(ssq * (1.0 / group) + EPS)
    r_hi = r.astype(BF16)
    r_lo = (r - r_hi.astype(F32)).astype(BF16)
    return (jnp.dot(r_hi, e, preferred_element_type=F32)
            + jnp.dot(r_lo, e, preferred_element_type=F32))


def _mixer_body(a_ref, b_ref, prev_ref, kvc_ref, vgain_ref, ws_ref, bs_ref, qgb_ref, kgb_ref,
                sink_ref, qgc_ref, s64_ref, e64_ref, s128_ref, e128_ref, s256_ref, e256_ref,
                y_ref, sb_ref, *, blocks_per_seq):
    i = pl.program_id(0)
    n = lax.rem(i, blocks_per_seq)
    ones_k = jnp.ones((2 * BLOCK, LANES), BF16)
    row = lax.broadcasted_iota(I32, (CHUNK, CHUNK), 0)
    col = lax.broadcasted_iota(I32, (CHUNK, CHUNK), 1)
    causal = col <= row
    lane_k = lax.broadcasted_iota(I32, (2 * BLOCK, LANES), 1)
    lane_q = lax.broadcasted_iota(I32, (BLOCK, LANES), 1)
    lo_k = lane_k < B_HEAD_DIM
    lo_q = lane_q < B_HEAD_DIM
    qi = lax.broadcasted_iota(I32, (B_REP * BLOCK, 2 * BLOCK), 0) & (BLOCK - 1)
    kj = lax.broadcasted_iota(I32, (B_REP * BLOCK, 2 * BLOCK), 1)
    rel = BLOCK + qi - kj
    valid = (rel >= 0) & (rel < BLOCK) & ((n > 0) | (kj >= BLOCK))

    v_all = a_ref[:, COL_V:COL_V + A_WIDTH].astype(F32)
    q_all = b_ref[:, PB_Q:PB_Q + B_Q_WIDTH].astype(F32)
    k_all = jnp.concatenate([prev_ref[:, 0:B_KV_WIDTH], b_ref[:, PB_K:PB_K + B_KV_WIDTH]],
                            axis=0).astype(F32)
    qc_all = b_ref[:, PB_QC:PB_QC + C_WIDTH].astype(F32)
    ssq_q = _group_ssq(q_all, s64_ref[...])
    ssq_k = _group_ssq(k_all, s64_ref[0:B_KV_WIDTH, :])
    ssq_c = _group_ssq(qc_all, s256_ref[...])
    ssq_v = _group_ssq(v_all, s128_ref[...])
    qn_all = (q_all * _group_bcast(ssq_q, e64_ref[...], B_HEAD_DIM)
              * (qgb_ref[...] * (B_HEAD_DIM ** -0.5)))
    kn_all = k_all * _group_bcast(ssq_k, e64_ref[:, 0:B_KV_WIDTH], B_HEAD_DIM) * kgb_ref[...]
    qcn_all = (qc_all * _group_bcast(ssq_c, e256_ref[...], C_HEAD_DIM)
               * (qgc_ref[...] * (C_HEAD_DIM ** -0.5))).astype(BF16)
    vn_all = (v_all * _group_bcast(ssq_v, e128_ref[...], A_GROUP_DIM)
              * vgain_ref[...]).astype(BF16)
    v_kv = jnp.concatenate([prev_ref[:, B_KV_WIDTH:2 * B_KV_WIDTH],
                            b_ref[:, PB_V:PB_V + B_KV_WIDTH]], axis=0).astype(F32)

    scores_c = [
        lax.dot_general(qcn_all[:, h * C_HEAD_DIM:(h + 1) * C_HEAD_DIM],
                        kvc_ref[:, h * C_HEAD_DIM:(h + 1) * C_HEAD_DIM],
                        (((1,), (1,)), ((), ())), preferred_element_type=F32)
        for h in range(C_HEADS)]
    vdups, sinks = [], []
    for slab in range(B_KV_WIDTH // LANES):
        ks = kn_all[:, slab * LANES:(slab + 1) * LANES]
        vs = v_kv[:, slab * LANES:(slab + 1) * LANES]
        kr = pltpu.roll(ks, B_HEAD_DIM, axis=1)
        vr = pltpu.roll(vs, B_HEAD_DIM, axis=1)
        for par in range(2):
            kv = 2 * slab + par
            kdup = (jnp.where(lo_k, ks, kr) if par == 0 else jnp.where(lo_k, kr, ks)).astype(BF16)
            vdups.append((jnp.where(lo_k, vs, vr) if par == 0
                          else jnp.where(lo_k, vr, vs)).astype(BF16))
            qs, sk = [], []
            for r in range(B_REP):
                h = kv * B_REP + r
                q_slab = qn_all[:, (h // 2) * LANES:(h // 2 + 1) * LANES]
                keep = lo_q if h % 2 == 0 else jnp.logical_not(lo_q)
                qs.append(jnp.where(keep, q_slab, 0.0).astype(BF16))
                sk.append(jnp.full((BLOCK, LANES), sink_ref[h], F32))
            sinks.append(jnp.concatenate(sk, axis=0))
            sc = lax.dot_general(jnp.concatenate(qs, axis=0), kdup, (((1,), (1,)), ((), ())),
                                 preferred_element_type=F32)
            sb_ref[kv] = jnp.where(valid, sc, -jnp.inf)

    probs_c = [jnp.exp(s - jnp.max(s, axis=-1, keepdims=True)).astype(BF16) for s in scores_c]
    mixed_a = [
        jnp.dot(jnp.where(causal, ws_ref[g], 0.0).astype(BF16),
                vn_all[:, g * A_GROUP_DIM:(g + 1) * A_GROUP_DIM],
                preferred_element_type=F32) + bs_ref[g]
        for g in range(A_GROUPS)]
    maxes_b = [jnp.maximum(jnp.max(sb_ref[kv], axis=-1, keepdims=True), sinks[kv])
               for kv in range(B_KV_HEADS)]
    probs_b = [jnp.exp(sb_ref[kv] - jnp.concatenate([maxes_b[kv]] * (2 * BLOCK // LANES), axis=1)
                       ).astype(BF16) for kv in range(B_KV_HEADS)]
    sink_terms = [jnp.exp(sinks[kv] - maxes_b[kv]) for kv in range(B_KV_HEADS)]

    outs_c = [(jnp.dot(p, kvc_ref[:, C_WIDTH + h * C_HEAD_DIM:C_WIDTH + (h + 1) * C_HEAD_DIM],
                       preferred_element_type=F32),
               jnp.dot(p, ones_k, preferred_element_type=F32))
              for h, p in enumerate(probs_c)]
    for g, mixed in enumerate(mixed_a):
        sl = slice(g * A_GROUP_DIM, (g + 1) * A_GROUP_DIM)
        u = a_ref[:, COL_U + g * A_GROUP_DIM:COL_U + (g + 1) * A_GROUP_DIM].astype(F32)
        y_ref[:, sl] = (u * mixed).astype(y_ref.dtype)
    outs_b = [(jnp.dot(p, vdup, preferred_element_type=F32),
               jnp.dot(p, ones_k, preferred_element_type=F32) + st)
              for p, vdup, st in zip(probs_b, vdups, sink_terms)]

    for h, (o, denom) in enumerate(outs_c):
        o = o / jnp.concatenate([denom] * (C_HEAD_DIM // LANES), axis=1)
        y_ref[:, A_WIDTH + B_Q_WIDTH + h * C_HEAD_DIM:
              A_WIDTH + B_Q_WIDTH + (h + 1) * C_HEAD_DIM] = o.astype(y_ref.dtype)
    for kv, (o, denom) in enumerate(outs_b):
        o = o / denom
        for pr in range(B_REP // 2):
            even = o[(2 * pr) * BLOCK:(2 * pr + 1) * BLOCK]
            odd = o[(2 * pr + 1) * BLOCK:(2 * pr + 2) * BLOCK]
            c0 = A_WIDTH + (kv * B_REP // 2 + pr) * LANES
            y_ref[:, c0:c0 + LANES] = jnp.where(lo_q, even, odd).astype(y_ref.dtype)


def _mixer_call(pa, pb, kvc, vgain_row, w_spatial, bs_bcast, qgb_row, kgb_row, sinks, qgc_row,
                layer, seq):
    t = pa.shape[0]
    bps = seq // BLOCK
    kvb = PB_K // (2 * B_KV_WIDTH)
    s64, e64 = _group_sums(B_Q_WIDTH, B_HEAD_DIM)
    s128, e128 = _group_sums(A_WIDTH, A_GROUP_DIM)
    s256, e256 = _group_sums(C_WIDTH, C_HEAD_DIM)

    def const(shape):
        return pl.BlockSpec(shape, lambda i: (0,) * len(shape))

    return pl.pallas_call(
        functools.partial(_mixer_body, blocks_per_seq=bps),
        grid=(t // BLOCK,),
        in_specs=[
            pl.BlockSpec((BLOCK, SEC_A_WIDTH), lambda i: (i, 0)),
            pl.BlockSpec((BLOCK, SEC_B_WIDTH), lambda i: (i, 0)),
            pl.BlockSpec((BLOCK, 2 * B_KV_WIDTH), lambda i: (jnp.maximum(i - 1, 0), kvb)),
            pl.BlockSpec((MEM_LEN, 2 * C_WIDTH), lambda i: (i // bps, 0)),
            const((1, A_WIDTH)),
            pl.BlockSpec((None, A_GROUPS, CHUNK, CHUNK), lambda i: (layer, 0, 0, 0)),
            const((A_GROUPS, CHUNK, LANES)),
            const((1, B_Q_WIDTH)),
            const((1, B_KV_WIDTH)),
            pl.BlockSpec(memory_space=pltpu.SMEM),
            const((1, C_WIDTH)),
            const((B_Q_WIDTH, LANES)), const((LANES, B_Q_WIDTH)),
            const((A_WIDTH, LANES)), const((LANES, A_WIDTH)),
            const((C_WIDTH, LANES)), const((LANES, C_WIDTH)),
        ],
        out_specs=pl.BlockSpec((BLOCK, N_BRANCH * BRANCH_WIDTH), lambda i: (i, 0)),
        out_shape=jax.ShapeDtypeStruct((t, N_BRANCH * BRANCH_WIDTH), BF16),
        scratch_shapes=[pltpu.VMEM((B_KV_HEADS, B_REP * BLOCK, 2 * BLOCK), F32)],
        compiler_params=_params(("arbitrary",)),
        name="mixer",
    )(pa, pb, pb, kvc, vgain_row, w_spatial, bs_bcast, qgb_row, kgb_row, sinks, qgc_row,
      s64, e64, s128, e128, s256, e256)


def _merge_body(y_ref, g0_ref, g1_ref, g2_ref, w_ref, o_ref, wb_ref):
    m = pl.program_id(1)

    @pl.when(m == 0)
    def _():
        wb_ref[...] = w_ref[...].astype(BF16)

    acc = None
    for b, g_ref in enumerate((g0_ref, g1_ref, g2_ref)):
        wide = jnp.dot(y_ref[:, b * BRANCH_WIDTH:(b + 1) * BRANCH_WIDTH], wb_ref[b],
                       preferred_element_type=F32)
        term = g_ref[...].astype(F32) * wide
        acc = term if acc is None else acc + term
    o_ref[...] = acc.astype(o_ref.dtype)


def _merge_call(y, gates, w_branch, layer):
    t = y.shape[0]
    per = D_MODEL // TN_MERGE

    def gate_spec(b):
        return pl.BlockSpec((TM_MERGE, TN_MERGE), lambda n, m: (m, b * per + n))

    return pl.pallas_call(
        _merge_body,
        grid=(D_MODEL // TN_MERGE, t // TM_MERGE),
        in_specs=[pl.BlockSpec((TM_MERGE, N_BRANCH * BRANCH_WIDTH), lambda n, m: (m, 0)),
                  gate_spec(0), gate_spec(1), gate_spec(2),
                  pl.BlockSpec((None, N_BRANCH, BRANCH_WIDTH, TN_MERGE),
                               lambda n, m: (layer, 0, 0, n))],
        out_specs=pl.BlockSpec((TM_MERGE, TN_MERGE), lambda n, m: (m, n)),
        out_shape=jax.ShapeDtypeStruct((t, D_MODEL), BF16),
        scratch_shapes=[pltpu.VMEM((N_BRANCH, BRANCH_WIDTH, TN_MERGE), BF16)],
        compiler_params=_params(("arbitrary", "arbitrary")),
        name="merge",
    )(y, gates, gates, gates, w_branch)


def _outproj_body(mg_ref, x_ref, w_ref, gn_ref, wr_ref, br_ref, x1_ref, lg_ref,
                  wb_ref, wr2_ref):
    i = pl.program_id(0)

    @pl.when(i == 0)
    def _():
        wb_ref[...] = w_ref[...].astype(BF16)
        wr = wr_ref[...]
        hi = wr.astype(BF16)
        wr2_ref[:, 0:LANES] = hi
        wr2_ref[:, LANES:2 * LANES] = (wr - hi.astype(F32)).astype(BF16)

    x1 = x_ref[...] + jnp.dot(mg_ref[...], wb_ref[...], preferred_element_type=F32)
    x1_ref[...] = x1
    h2 = _rms(x1, gn_ref[...])
    hi = h2.astype(BF16)
    lo = (h2 - hi.astype(F32)).astype(BF16)
    hi_w = jnp.dot(hi, wr2_ref[...], preferred_element_type=F32)
    lo_w = jnp.dot(lo, wr2_ref[...], preferred_element_type=F32)
    lg_ref[...] = hi_w[:, 0:LANES] + lo_w[:, 0:LANES] + hi_w[:, LANES:2 * LANES] + br_ref[...]


def _outproj_call(merged, x, w_out, gn_row, w_router, b_router, layer):
    t = x.shape[0]
    return pl.pallas_call(
        _outproj_body,
        grid=(t // TM_OUT,),
        in_specs=[pl.BlockSpec((TM_OUT, D_MODEL), lambda i: (i, 0)),
                  pl.BlockSpec((TM_OUT, D_MODEL), lambda i: (i, 0)),
                  pl.BlockSpec((None, D_MODEL, D_MODEL), lambda i: (layer, 0, 0),
                               pipeline_mode=pl.Buffered(1)),
                  pl.BlockSpec((1, D_MODEL), lambda i: (0, 0)),
                  pl.BlockSpec((D_MODEL, LANES), lambda i: (0, 0)),
                  pl.BlockSpec((1, LANES), lambda i: (0, 0))],
        out_specs=[pl.BlockSpec((TM_OUT, D_MODEL), lambda i: (i, 0)),
                   pl.BlockSpec((TM_OUT, LANES), lambda i: (i, 0))],
        out_shape=[jax.ShapeDtypeStruct((t, D_MODEL), F32),
                   jax.ShapeDtypeStruct((t, LANES), F32)],
        scratch_shapes=[pltpu.VMEM((D_MODEL, D_MODEL), BF16),
                        pltpu.VMEM((D_MODEL, 2 * LANES), BF16)],
        compiler_params=_params(("arbitrary",)),
        name="outproj",
    )(merged, x, w_out, gn_row, w_router, b_router)


def _route_body(lg_ref, ri_ref, rw_ref, cnt_ref, carry_ref):
    i = pl.program_id(0)

    @pl.when(i == 0)
    def _():
        carry_ref[...] = jnp.zeros_like(carry_ref)

    lg = lg_ref[...]
    tm = lg.shape[0]
    lane = lax.broadcasted_iota(I32, lg.shape, 1)
    neg = -jnp.inf
    big = jnp.int32(LANES)

    is_g = lane < N_GROUPS
    gl = jnp.where(is_g, lg, neg)
    gmax = jnp.max(gl, axis=-1, keepdims=True)
    gidx = jnp.min(jnp.where(gl == gmax, lane, big), axis=-1, keepdims=True)
    p_g = 1.0 / jnp.sum(jnp.where(is_g, jnp.exp(lg - gmax), 0.0), axis=-1, keepdims=True)

    e_lane = lane - N_GROUPS
    in_group = (e_lane >= 0) & (e_lane < N_EXPERTS) & ((e_lane >> 2) == gidx)
    el = jnp.where(in_group, lg, neg)
    v0 = jnp.max(el, axis=-1, keepdims=True)
    i0 = jnp.min(jnp.where(el == v0, lane, big), axis=-1, keepdims=True)
    el1 = jnp.where(lane == i0, neg, el)
    v1 = jnp.max(el1, axis=-1, keepdims=True)
    i1 = jnp.min(jnp.where(el1 == v1, lane, big), axis=-1, keepdims=True)
    e0 = i0 - N_GROUPS
    e1 = i1 - N_GROUPS
    ex = jnp.exp(v1 - v0)
    w0 = p_g / (1.0 + ex)
    w1 = p_g * ex / (1.0 + ex)

    oh0 = lane == e0
    oh1 = lane == e1
    oh = jnp.where(oh0 | oh1, 1.0, 0.0)
    r_i = lax.broadcasted_iota(I32, (tm, tm), 0)
    c_i = lax.broadcasted_iota(I32, (tm, tm), 1)
    lower = jnp.where(c_i < r_i, 1.0, 0.0).astype(BF16)
    before = jnp.dot(lower, oh.astype(BF16), preferred_element_type=F32) + carry_ref[...]
    rank0 = jnp.sum(jnp.where(oh0, before, 0.0), axis=-1, keepdims=True).astype(I32)
    rank1 = jnp.sum(jnp.where(oh1, before, 0.0), axis=-1, keepdims=True).astype(I32)
    carry_ref[...] = carry_ref[...] + jnp.sum(oh, axis=0, keepdims=True)

    zero_i = jnp.zeros_like(lane)
    ri_ref[...] = jnp.where(lane == 0, e0, jnp.where(lane == 1, e1,
                            jnp.where(lane == 2, rank0, jnp.where(lane == 3, rank1, zero_i))))
    rw_ref[...] = jnp.where(lane == 0, w0, jnp.where(lane == 1, w1, 0.0))
    cnt_ref[...] = jnp.broadcast_to(carry_ref[...], cnt_ref.shape).astype(I32)


def _route_call(logits):
    t = logits.shape[0]
    return pl.pallas_call(
        _route_body,
        grid=(t // TM_ROUTE,),
        in_specs=[pl.BlockSpec((TM_ROUTE, LANES), lambda i: (i, 0))],
        out_specs=[pl.BlockSpec((TM_ROUTE, LANES), lambda i: (i, 0)),
                   pl.BlockSpec((TM_ROUTE, LANES), lambda i: (i, 0)),
                   pl.BlockSpec((8, LANES), lambda i: (0, 0))],
        out_shape=[jax.ShapeDtypeStruct((t, LANES), I32),
                   jax.ShapeDtypeStruct((t, LANES), F32),
                   jax.ShapeDtypeStruct((8, LANES), I32)],
        scratch_shapes=[pltpu.VMEM((1, LANES), F32)],
        compiler_params=_params(("arbitrary",)),
        name="route",
    )(logits)


def _plan(route_i, counts, n_tiles):
    c = counts[0, :N_EXPERTS]
    tiles = (c + TM_EXPERT - 1) // TM_EXPERT
    ctiles = jnp.cumsum(tiles)
    start = (ctiles - tiles) * TM_EXPERT
    eid = jnp.arange(N_EXPERTS, dtype=I32)

    def slot(e, r):
        return jnp.sum(jnp.where(e[:, None] == eid[None, :], start[None, :], 0), axis=1) + r

    pos0 = slot(route_i[:, 0], route_i[:, 2]).astype(I32)
    pos1 = slot(route_i[:, 1], route_i[:, 3]).astype(I32)
    n_used = ctiles[-1]
    tid = jnp.minimum(jnp.arange(n_tiles, dtype=I32), n_used - 1)
    tile_expert = jnp.sum(tid[:, None] >= ctiles[None, :], axis=1).astype(I32)
    used_rows = n_used * TM_EXPERT
    pad_lo = jnp.concatenate([start + c, used_rows[None]]).astype(I32)
    pad_hi = jnp.concatenate([start + tiles * TM_EXPERT,
                              jnp.full((1,), n_tiles * TM_EXPERT, I32)]).astype(I32)
    return pos0, pos1, tile_expert, n_used.reshape(1).astype(I32), pad_lo, pad_hi


def _invert_body(p0_ref, p1_ref, lo_ref, hi_ref, src_ref):
    i = pl.program_id(0)

    @pl.when(i == 0)
    def _():
        def fill(r, c):
            src_ref[r] = 0
            return c
        for j in range(N_EXPERTS + 1):
            lax.fori_loop(lo_ref[j], hi_ref[j], fill, 0)

    base = i * TK_INVERT

    def scatter(t, c):
        src_ref[p0_ref[t]] = base + t
        src_ref[p1_ref[t]] = base + t
        return c
    lax.fori_loop(0, TK_INVERT, scatter, 0, unroll=8)


def _invert_call(pos0, pos1, pad_lo, pad_hi, n_rows):
    t = pos0.shape[0]
    smem = pl.BlockSpec((TK_INVERT,), lambda i: (i,), memory_space=pltpu.SMEM)
    whole = pl.BlockSpec(memory_space=pltpu.SMEM)
    return pl.pallas_call(
        _invert_body,
        grid=(t // TK_INVERT,),
        in_specs=[smem, smem, whole, whole],
        out_specs=pl.BlockSpec(memory_space=pltpu.SMEM),
        out_shape=jax.ShapeDtypeStruct((n_rows,), I32),
        compiler_params=_params(("arbitrary",)),
        name="invert",
    )(pos0, pos1, pad_lo, pad_hi)


def _expert_body(te_ref, nu_ref, srcc_ref, srcn_ref, x_hbm, gn_ref, wg_ref, wu_ref, wd_ref,
                 y_ref, xbuf_ref, wgb_ref, wub_ref, wdb_ref, sems):
    i = pl.program_id(0)
    n_used = nu_ref[0]
    slot = lax.rem(i, 2)

    def row_copy(src_ref, r, s):
        return pltpu.make_async_copy(x_hbm.at[pl.ds(src_ref[r], 1)],
                                     xbuf_ref.at[s, pl.ds(r, 1)], sems.at[s])

    def issue(src_ref, s):
        for r in range(TM_EXPERT):
            row_copy(src_ref, r, s).start(priority=r % 2)

    @pl.when(i == 0)
    def _():
        issue(srcc_ref, 0)

    @pl.when(i < n_used)
    def _():
        prev = te_ref[jnp.maximum(i - 1, 0)]

        @pl.when(jnp.logical_or(i == 0, te_ref[i] != prev))
        def _():
            wgb_ref[...] = wg_ref[...].astype(BF16)
            wub_ref[...] = wu_ref[...].astype(BF16)
            wdb_ref[...] = wd_ref[...].astype(BF16)

        pltpu.make_async_copy(x_hbm.at[pl.ds(0, TM_EXPERT)], xbuf_ref.at[slot],
                              sems.at[slot]).wait()

    def compute(s, prefetch):
        h = _rms(xbuf_ref[s], gn_ref[...]).astype(BF16)
        if prefetch:
            issue(srcn_ref, 1 - s)
        g = jnp.dot(h, wgb_ref[...], preferred_element_type=F32)
        u = jnp.dot(h, wub_ref[...], preferred_element_type=F32)
        hid = (g * _sigmoid(g) * u).astype(BF16)
        y_ref[...] = jnp.dot(hid, wdb_ref[...], preferred_element_type=F32)

    for s in range(2):
        @pl.when(jnp.logical_and(i + 1 < n_used, slot == s))
        def _():
            compute(s, True)

        @pl.when(jnp.logical_and(i + 1 == n_used, slot == s))
        def _():
            compute(s, False)

    @pl.when(i >= n_used)
    def _():
        y_ref[...] = jnp.zeros_like(y_ref)


def _expert_call(x1, src, tile_expert, n_used, gn_row, w_gate, w_up, w_down, layer):
    n_rows = src.shape[0]
    n_tiles = n_rows // TM_EXPERT

    def w_map(i, te, nu):
        return (layer, te[i], 0, 0)

    grid_spec = pltpu.PrefetchScalarGridSpec(
        num_scalar_prefetch=2,
        grid=(n_tiles,),
        in_specs=[pl.BlockSpec((TM_EXPERT,), lambda i, te, nu: (i,), memory_space=pltpu.SMEM),
                  pl.BlockSpec((TM_EXPERT,), lambda i, te, nu: (jnp.minimum(i + 1, n_tiles - 1),),
                               memory_space=pltpu.SMEM),
                  pl.BlockSpec(memory_space=pl.ANY),
                  pl.BlockSpec((1, D_MODEL), lambda i, te, nu: (0, 0)),
                  pl.BlockSpec((None, None, D_MODEL, D_FF_EXPERT), w_map),
                  pl.BlockSpec((None, None, D_MODEL, D_FF_EXPERT), w_map),
                  pl.BlockSpec((None, None, D_FF_EXPERT, D_MODEL), w_map)],
        out_specs=pl.BlockSpec((TM_EXPERT, D_MODEL), lambda i, te, nu: (i, 0)),
        scratch_shapes=[pltpu.VMEM((2, TM_EXPERT, D_MODEL), F32),
                        pltpu.VMEM((D_MODEL, D_FF_EXPERT), BF16),
                        pltpu.VMEM((D_MODEL, D_FF_EXPERT), BF16),
                        pltpu.VMEM((D_FF_EXPERT, D_MODEL), BF16),
                        pltpu.SemaphoreType.DMA((2,))],
    )
    return pl.pallas_call(
        _expert_body,
        grid_spec=grid_spec,
        out_shape=jax.ShapeDtypeStruct((n_rows, D_MODEL), F32),
        compiler_params=_params(("arbitrary",)),
        name="experts",
    )(tile_expert, n_used, src, src, x1, gn_row, w_gate, w_up, w_down)


def _combine_body(p0c_ref, p1c_ref, p0n_ref, p1n_ref, x1_ref, rw_ref, gn_ref, ys_hbm,
                  x2_ref, h_ref, buf_ref, sems, *, n_steps):
    i = pl.program_id(0)
    slot = lax.rem(i, 2)

    def row_copy(p_ref, k, t, s):
        return pltpu.make_async_copy(ys_hbm.at[pl.ds(p_ref[t], 1)],
                                     buf_ref.at[s, k, pl.ds(t, 1)], sems.at[s])

    def issue(p0_ref, p1_ref, s):
        for t in range(TM_COMBINE):
            row_copy(p0_ref, 0, t, s).start(priority=0)
            row_copy(p1_ref, 1, t, s).start(priority=1)

    @pl.when(i == 0)
    def _():
        issue(p0c_ref, p1c_ref, 0)

    for k in range(TOP_K):
        pltpu.make_async_copy(ys_hbm.at[pl.ds(0, TM_COMBINE)], buf_ref.at[slot, k],
                              sems.at[slot]).wait()

    def compute(s, prefetch):
        rw = rw_ref[...]
        x2 = x1_ref[...] + rw[:, 0:1] * buf_ref[s, 0] + rw[:, 1:2] * buf_ref[s, 1]
        if prefetch:
            issue(p0n_ref, p1n_ref, 1 - s)
        x2_ref[...] = x2
        h_ref[...] = _rms(x2, gn_ref[...]).astype(h_ref.dtype)

    for s in range(2):
        @pl.when(jnp.logical_and(i + 1 < n_steps, slot == s))
        def _():
            compute(s, True)

        @pl.when(jnp.logical_and(i + 1 == n_steps, slot == s))
        def _():
            compute(s, False)


def _combine_call(x1, route_w, pos0, pos1, ys, gn_row):
    t = x1.shape[0]
    n_steps = t // TM_COMBINE

    def nxt(i):
        return (jnp.minimum(i + 1, n_steps - 1),)

    smem = functools.partial(pl.BlockSpec, (TM_COMBINE,), memory_space=pltpu.SMEM)
    return pl.pallas_call(
        functools.partial(_combine_body, n_steps=n_steps),
        grid=(n_steps,),
        in_specs=[smem(lambda i: (i,)), smem(lambda i: (i,)), smem(nxt), smem(nxt),
                  pl.BlockSpec((TM_COMBINE, D_MODEL), lambda i: (i, 0)),
                  pl.BlockSpec((TM_COMBINE, LANES), lambda i: (i, 0)),
                  pl.BlockSpec((1, D_MODEL), lambda i: (0, 0)),
                  pl.BlockSpec(memory_space=pl.ANY)],
        out_specs=[pl.BlockSpec((TM_COMBINE, D_MODEL), lambda i: (i, 0)),
                   pl.BlockSpec((TM_COMBINE, D_MODEL), lambda i: (i, 0))],
        out_shape=[jax.ShapeDtypeStruct((t, D_MODEL), F32),
                   jax.ShapeDtypeStruct((t, D_MODEL), BF16)],
        scratch_shapes=[pltpu.VMEM((2, TOP_K, TM_COMBINE, D_MODEL), F32),
                        pltpu.SemaphoreType.DMA((2,))],
        compiler_params=_params(("arbitrary",)),
        name="combine",
    )(pos0, pos1, pos0, pos1, x1, route_w, gn_row, ys)


def kernel(x, mem, norm_mix, norm_mem, norm_ffn, w_in, v_gain, w_spatial, b_spatial,
           q_gain_b, k_gain_b, sinks, q_gain_c, k_gain_c, w_mem_kv, w_branch, w_out,
           w_router_group, b_router_group, w_router_expert, b_router_expert,
           w_gate_e, w_up_e, w_down_e):
    b, s, d = x.shape
    t = b * s
    assert d == D_MODEL and s % BLOCK == 0 and mem.shape[1] == MEM_LEN
    assert t % TK_INVERT == 0 and t % TM_PROJ == 0
    n_tiles = (t * TOP_K) // TM_EXPERT + N_EXPERTS
    n_rows = n_tiles * TM_EXPERT

    xf = x.reshape(t, d)
    mem2d = mem.reshape(b * MEM_LEN, d)
    pad = LANES - N_GROUPS - N_EXPERTS

    h = _norm_call(xf, norm_mix[0][None])
    for layer in range(DEPTH):
        kvc = _memkv_call(mem2d, norm_mem[layer][None], w_mem_kv, k_gain_c[layer][None], layer)
        pa = _inproj_call(h, w_in, layer, 0, SEC_A_WIDTH, TN_PROJ_A, _gelu, "inproj_a")
        pb = _inproj_call(h, w_in, layer, COL_QB, SEC_B_WIDTH, TN_PROJ_B, _identity, "inproj_b")
        gates = _inproj_call(h, w_in, layer, COL_GATE, SEC_G_WIDTH, TN_PROJ_G, _sigmoid, "inproj_g")
        bs_bcast = jnp.broadcast_to(b_spatial[layer][:, :, None], (A_GROUPS, CHUNK, LANES))
        y = _mixer_call(pa, pb, kvc, v_gain[layer].reshape(1, A_WIDTH), w_spatial, bs_bcast,
                        jnp.tile(q_gain_b[layer], B_HEADS)[None],
                        jnp.tile(k_gain_b[layer], B_KV_HEADS)[None],
                        sinks[layer], jnp.tile(q_gain_c[layer], C_HEADS)[None], layer, s)
        merged = _merge_call(y, gates, w_branch, layer)
        w_router = jnp.pad(jnp.concatenate([w_router_group[layer], w_router_expert[layer]], axis=1),
                           ((0, 0), (0, pad)))
        b_router = jnp.pad(jnp.concatenate([b_router_group[layer], b_router_expert[layer]]),
                           (0, pad))[None]
        x1, logits = _outproj_call(merged, xf, w_out, norm_ffn[layer][None], w_router, b_router,
                                   layer)
        route_i, route_w, counts = _route_call(logits)
        pos0, pos1, tile_expert, n_used, pad_lo, pad_hi = _plan(route_i, counts, n_tiles)
        src = _invert_call(pos0, pos1, pad_lo, pad_hi, n_rows)
        ys = _expert_call(x1, src, tile_expert, n_used, norm_ffn[layer][None], w_gate_e, w_up_e,
                          w_down_e, layer)
        gn_next = norm_mix[layer + 1] if layer + 1 < DEPTH else norm_mix[layer]
        xf, h = _combine_call(x1, route_w, pos0, pos1, ys, gn_next[None])
    return xf.reshape(b, s, d)
```

```python
import functools

import jax
import jax.numpy as jnp
from jax import lax
from jax.experimental import pallas as pl
from jax.experimental.pallas import tpu as pltpu

F32 = jnp.float32
BF16 = jnp.bfloat16
I32 = jnp.int32

D_MODEL = 2048
DEPTH = 2
MEM_LEN = 256
EPS = 1e-6
BRANCH_WIDTH = 1024
N_BRANCH = 3

CHUNK = 128
A_GROUPS = 8
A_GROUP_DIM = 128
A_WIDTH = A_GROUPS * A_GROUP_DIM

BLOCK = 128
B_HEADS = 16
B_KV_HEADS = 4
B_REP = B_HEADS // B_KV_HEADS
B_HEAD_DIM = 64
B_Q_WIDTH = B_HEADS * B_HEAD_DIM
B_KV_WIDTH = B_KV_HEADS * B_HEAD_DIM

C_HEADS = 4
C_HEAD_DIM = 256
C_WIDTH = C_HEADS * C_HEAD_DIM

COL_U = 0
COL_V = A_WIDTH
COL_QB = 2 * A_WIDTH
COL_KB = COL_QB + B_Q_WIDTH
COL_VB = COL_KB + B_KV_WIDTH
COL_QC = COL_VB + B_KV_WIDTH
COL_GATE = COL_QC + C_WIDTH
IN_WIDTH = COL_GATE + N_BRANCH * D_MODEL
SEC_A_WIDTH = COL_QB
SEC_B_WIDTH = COL_GATE - COL_QB
SEC_G_WIDTH = N_BRANCH * D_MODEL
PB_Q = 0
PB_K = COL_KB - COL_QB
PB_V = COL_VB - COL_QB
PB_QC = COL_QC - COL_QB

N_GROUPS = 4
EXPERTS_PER_GROUP = 4
N_EXPERTS = N_GROUPS * EXPERTS_PER_GROUP
TOP_K = 2
D_FF_EXPERT = 512

LANES = 128
VMEM_LIMIT = 56 * 1024 * 1024
SQRT_HALF = 0.7071067811865476

TM_NORM = 512
TM_PROJ = 1024
TN_PROJ_A, TN_PROJ_B, TN_PROJ_G = 1024, 512, 1536
TM_MERGE, TN_MERGE = 512, 1024
TM_OUT = 512
TM_ROUTE = 512
TK_INVERT = 2048
TM_EXPERT = 256
TM_COMBINE = 256


def _params(sem):
    return pltpu.CompilerParams(dimension_semantics=sem, vmem_limit_bytes=VMEM_LIMIT)


def _rms(x, gain):
    ms = jnp.mean(x * x, axis=-1, keepdims=True)
    return x * lax.rsqrt(ms + EPS) * gain


def _norm_body(x_ref, g_ref, o_ref):
    o_ref[...] = _rms(x_ref[...], g_ref[...]).astype(o_ref.dtype)


def _norm_call(x, gain_row):
    t = x.shape[0]
    return pl.pallas_call(
        _norm_body,
        grid=(t // TM_NORM,),
        in_specs=[pl.BlockSpec((TM_NORM, D_MODEL), lambda i: (i, 0)),
                  pl.BlockSpec((1, D_MODEL), lambda i: (0, 0))],
        out_specs=pl.BlockSpec((TM_NORM, D_MODEL), lambda i: (i, 0)),
        out_shape=jax.ShapeDtypeStruct((t, D_MODEL), BF16),
        compiler_params=_params(("arbitrary",)),
        name="norm",
    )(x, gain_row)


TN_MEMKV = 512


def _memkv_body(mem_ref, gm_ref, w_ref, gk_ref, o_ref):
    n = pl.program_id(0)
    h = _rms(mem_ref[...], gm_ref[...]).astype(BF16)
    acc = jnp.dot(h, w_ref[...].astype(BF16), preferred_element_type=F32)

    @pl.when(n < C_WIDTH // TN_MEMKV)
    def _():
        for j in range(TN_MEMKV // C_HEAD_DIM):
            sl = slice(j * C_HEAD_DIM, (j + 1) * C_HEAD_DIM)
            o_ref[:, sl] = _rms(acc[:, sl], gk_ref[...]).astype(o_ref.dtype)

    @pl.when(n >= C_WIDTH // TN_MEMKV)
    def _():
        o_ref[...] = acc.astype(o_ref.dtype)


def _memkv_call(mem2d, gm_row, w_mem_kv, gk_row, layer):
    rows = mem2d.shape[0]
    return pl.pallas_call(
        _memkv_body,
        grid=(2 * C_WIDTH // TN_MEMKV,),
        in_specs=[pl.BlockSpec((rows, D_MODEL), lambda n: (0, 0)),
                  pl.BlockSpec((1, D_MODEL), lambda n: (0, 0)),
                  pl.BlockSpec((None, D_MODEL, TN_MEMKV), lambda n: (layer, 0, n)),
                  pl.BlockSpec((1, C_HEAD_DIM), lambda n: (0, 0))],
        out_specs=pl.BlockSpec((rows, TN_MEMKV), lambda n: (0, n)),
        out_shape=jax.ShapeDtypeStruct((rows, 2 * C_WIDTH), BF16),
        compiler_params=_params(("arbitrary",)),
        name="memkv",
    )(mem2d, gm_row, w_mem_kv, gk_row)


def _gelu(x):
    return 0.5 * x * (1.0 + lax.erf(x * SQRT_HALF))


def _sigmoid(x):
    return 1.0 / (1.0 + jnp.exp(-x))


def _identity(x):
    return x


def _inproj_body(h_ref, w_ref, o_ref, wb_ref, *, act):
    @pl.when(pl.program_id(1) == 0)
    def _():
        wb_ref[...] = w_ref[...].astype(BF16)

    acc = jnp.dot(h_ref[...], wb_ref[...], preferred_element_type=F32)
    o_ref[...] = act(acc).astype(o_ref.dtype)


def _inproj_call(h, w_in, layer, col0, width, tn, act, name):
    t = h.shape[0]
    assert col0 % tn == 0 and width % tn == 0
    n0 = col0 // tn
    return pl.pallas_call(
        functools.partial(_inproj_body, act=act),
        grid=(width // tn, t // TM_PROJ),
        in_specs=[pl.BlockSpec((TM_PROJ, D_MODEL), lambda n, m: (m, 0)),
                  pl.BlockSpec((None, D_MODEL, tn), lambda n, m: (layer, 0, n0 + n))],
        out_specs=pl.BlockSpec((TM_PROJ, tn), lambda n, m: (m, n)),
        out_shape=jax.ShapeDtypeStruct((t, width), BF16),
        scratch_shapes=[pltpu.VMEM((D_MODEL, tn), BF16)],
        compiler_params=_params(("arbitrary", "arbitrary")),
        name=name,
    )(h, w_in)


def _group_sums(width, group):
    gid = jnp.arange(width, dtype=I32) // group
    s = (gid[:, None] == jnp.arange(LANES, dtype=I32)[None, :]).astype(BF16)
    return s, s.T


def _group_ssq(x, s):
    return jnp.dot((x * x).astype(BF16), s, preferred_element_type=F32)


def _group_bcast(ssq, e, group):
    r = lax.rsqrt(ssq * (1.0 / group) + EPS)
    r_hi = r.astype(BF16)
    r_lo = (r - r_hi.astype(F32)).astype(BF16)
    return (jnp.dot(r_hi, e, preferred_element_type=F32)
            + jnp.dot(r_lo, e, preferred_element_type=F32))


def _mixer_body(a_ref, b_ref, prev_ref, kvc_ref, vgain_ref, ws_ref, bs_ref, qgb_ref, kgb_ref,
                sink_ref, qgc_ref, s64_ref, e64_ref, s128_ref, e128_ref, s256_ref, e256_ref,
                y_ref, sb_ref, *, blocks_per_seq):
    i = pl.program_id(0)
    n = lax.rem(i, blocks_per_seq)
    ones_k = jnp.ones((2 * BLOCK, LANES), BF16)
    row = lax.broadcasted_iota(I32, (CHUNK, CHUNK), 0)
    col = lax.broadcasted_iota(I32, (CHUNK, CHUNK), 1)
    causal = col <= row
    lane_k = lax.broadcasted_iota(I32, (2 * BLOCK, LANES), 1)
    lane_q = lax.broadcasted_iota(I32, (BLOCK, LANES), 1)
    lo_k = lane_k < B_HEAD_DIM
    lo_q = lane_q < B_HEAD_DIM
    qi = lax.broadcasted_iota(I32, (B_REP * BLOCK, 2 * BLOCK), 0) & (BLOCK - 1)
    kj = lax.broadcasted_iota(I32, (B_REP * BLOCK, 2 * BLOCK), 1)
    rel = BLOCK + qi - kj
    valid = (rel >= 0) & (rel < BLOCK) & ((n > 0) | (kj >= BLOCK))

    v_all = a_ref[:, COL_V:COL_V + A_WIDTH].astype(F32)
    q_all = b_ref[:, PB_Q:PB_Q + B_Q_WIDTH].astype(F32)
    k_all = jnp.concatenate([prev_ref[:, 0:B_KV_WIDTH], b_ref[:, PB_K:PB_K + B_KV_WIDTH]],
                            axis=0).astype(F32)
    qc_all = b_ref[:, PB_QC:PB_QC + C_WIDTH].astype(F32)
    ssq_q = _group_ssq(q_all, s64_ref[...])
    ssq_k = _group_ssq(k_all, s64_ref[0:B_KV_WIDTH, :])
    ssq_c = _group_ssq(qc_all, s256_ref[...])
    ssq_v = _group_ssq(v_all, s128_ref[...])
    qn_all = (q_all * _group_bcast(ssq_q, e64_ref[...], B_HEAD_DIM)
              * (qgb_ref[...] * (B_HEAD_DIM ** -0.5)))
    kn_all = k_all * _group_bcast(ssq_k, e64_ref[:, 0:B_KV_WIDTH], B_HEAD_DIM) * kgb_ref[...]
    qcn_all = (qc_all * _group_bcast(ssq_c, e256_ref[...], C_HEAD_DIM)
               * (qgc_ref[...] * (C_HEAD_DIM ** -0.5))).astype(BF16)
    vn_all = (v_all * _group_bcast(ssq_v, e128_ref[...], A_GROUP_DIM)
              * vgain_ref[...]).astype(BF16)
    v_kv = jnp.concatenate([prev_ref[:, B_KV_WIDTH:2 * B_KV_WIDTH],
                            b_ref[:, PB_V:PB_V + B_KV_WIDTH]], axis=0).astype(F32)

    scores_c = [
        lax.dot_general(qcn_all[:, h * C_HEAD_DIM:(h + 1) * C_HEAD_DIM],
                        kvc_ref[:, h * C_HEAD_DIM:(h + 1) * C_HEAD_DIM],
                        (((1,), (1,)), ((), ())), preferred_element_type=F32)
        for h in range(C_HEADS)]
    vdups, sinks = [], []
    for slab in range(B_KV_WIDTH // LANES):
        ks = kn_all[:, slab * LANES:(slab + 1) * LANES]
        vs = v_kv[:, slab * LANES:(slab + 1) * LANES]
        kr = pltpu.roll(ks, B_HEAD_DIM, axis=1)
        vr = pltpu.roll(vs, B_HEAD_DIM, axis=1)
        for par in range(2):
            kv = 2 * slab + par
            kdup = (jnp.where(lo_k, ks, kr) if par == 0 else jnp.where(lo_k, kr, ks)).astype(BF16)
            vdups.append((jnp.where(lo_k, vs, vr) if par == 0
                          else jnp.where(lo_k, vr, vs)).astype(BF16))
            qs, sk = [], []
            for r in range(B_REP):
                h = kv * B_REP + r
                q_slab = qn_all[:, (h // 2) * LANES:(h // 2 + 1) * LANES]
                keep = lo_q if h % 2 == 0 else jnp.logical_not(lo_q)
                qs.append(jnp.where(keep, q_slab, 0.0).astype(BF16))
                sk.append(jnp.full((BLOCK, LANES), sink_ref[h], F32))
            sinks.append(jnp.concatenate(sk, axis=0))
            sc = lax.dot_general(jnp.concatenate(qs, axis=0), kdup, (((1,), (1,)), ((), ())),
                                 preferred_element_type=F32)
            sb_ref[kv] = jnp.where(valid, sc, -jnp.inf)

    probs_c = [jnp.exp(s - jnp.max(s, axis=-1, keepdims=True)).astype(BF16) for s in scores_c]
    mixed_a = [
        jnp.dot(jnp.where(causal, ws_ref[g], 0.0).astype(BF16),
                vn_all[:, g * A_GROUP_DIM:(g + 1) * A_GROUP_DIM],
                preferred_element_type=F32) + bs_ref[g]
        for g in range(A_GROUPS)]
    maxes_b = [jnp.maximum(jnp.max(sb_ref[kv], axis=-1, keepdims=True), sinks[kv])
               for kv in range(B_KV_HEADS)]
    probs_b = [jnp.exp(sb_ref[kv] - jnp.concatenate([maxes_b[kv]] * (2 * BLOCK // LANES), axis=1)
                       ).astype(BF16) for kv in range(B_KV_HEADS)]
    sink_terms = [jnp.exp(sinks[kv] - maxes_b[kv]) for kv in range(B_KV_HEADS)]

    outs_c = [(jnp.dot(p, kvc_ref[:, C_WIDTH + h * C_HEAD_DIM:C_WIDTH + (h + 1) * C_HEAD_DIM],
                       preferred_element_type=F32),
               jnp.dot(p, ones_k, preferred_element_type=F32))
              for h, p in enumerate(probs_c)]
    for g, mixed in enumerate(mixed_a):
        sl = slice(g * A_GROUP_DIM, (g + 1) * A_GROUP_DIM)
        u = a_ref[:, COL_U + g * A_GROUP_DIM:COL_U + (g + 1) * A_GROUP_DIM].astype(F32)
        y_ref[:, sl] = (u * mixed).astype(y_ref.dtype)
    outs_b = [(jnp.dot(p, vdup, preferred_element_type=F32),
               jnp.dot(p, ones_k, preferred_element_type=F32) + st)
              for p, vdup, st in zip(probs_b, vdups, sink_terms)]

    for h, (o, denom) in enumerate(outs_c):
        o = o / jnp.concatenate([denom] * (C_HEAD_DIM // LANES), axis=1)
        y_ref[:, A_WIDTH + B_Q_WIDTH + h * C_HEAD_DIM:
              A_WIDTH + B_Q_WIDTH + (h + 1) * C_HEAD_DIM] = o.astype(y_ref.dtype)
    for kv, (o, denom) in enumerate(outs_b):
        o = o / denom
        for pr in range(B_REP // 2):
            even = o[(2 * pr) * BLOCK:(2 * pr + 1) * BLOCK]
            odd = o[(2 * pr + 1) * BLOCK:(2 * pr + 2) * BLOCK]
            c0 = A_WIDTH + (kv * B_REP // 2 + pr) * LANES
            y_ref[:, c0:c0 + LANES] = jnp.where(lo_q, even, odd).astype(y_ref.dtype)


def _mixer_call(pa, pb, kvc, vgain_row, w_spatial, bs_bcast, qgb_row, kgb_row, sinks, qgc_row,
                layer, seq):
    t = pa.shape[0]
    bps = seq // BLOCK
    kvb = PB_K // (2 * B_KV_WIDTH)
    s64, e64 = _group_sums(B_Q_WIDTH, B_HEAD_DIM)
    s128, e128 = _group_sums(A_WIDTH, A_GROUP_DIM)
    s256, e256 = _group_sums(C_WIDTH, C_HEAD_DIM)

    def const(shape):
        return pl.BlockSpec(shape, lambda i: (0,) * len(shape))

    return pl.pallas_call(
        functools.partial(_mixer_body, blocks_per_seq=bps),
        grid=(t // BLOCK,),
        in_specs=[
            pl.BlockSpec((BLOCK, SEC_A_WIDTH), lambda i: (i, 0)),
            pl.BlockSpec((BLOCK, SEC_B_WIDTH), lambda i: (i, 0)),
            pl.BlockSpec((BLOCK, 2 * B_KV_WIDTH), lambda i: (jnp.maximum(i - 1, 0), kvb)),
            pl.BlockSpec((MEM_LEN, 2 * C_WIDTH), lambda i: (i // bps, 0)),
            const((1, A_WIDTH)),
            pl.BlockSpec((None, A_GROUPS, CHUNK, CHUNK), lambda i: (layer, 0, 0, 0)),
            const((A_GROUPS, CHUNK, LANES)),
            const((1, B_Q_WIDTH)),
            const((1, B_KV_WIDTH)),
            pl.BlockSpec(memory_space=pltpu.SMEM),
            const((1, C_WIDTH)),
            const((B_Q_WIDTH, LANES)), const((LANES, B_Q_WIDTH)),
            const((A_WIDTH, LANES)), const((LANES, A_WIDTH)),
            const((C_WIDTH, LANES)), const((LANES, C_WIDTH)),
        ],
        out_specs=pl.BlockSpec((BLOCK, N_BRANCH * BRANCH_WIDTH), lambda i: (i, 0)),
        out_shape=jax.ShapeDtypeStruct((t, N_BRANCH * BRANCH_WIDTH), BF16),
        scratch_shapes=[pltpu.VMEM((B_KV_HEADS, B_REP * BLOCK, 2 * BLOCK), F32)],
        compiler_params=_params(("arbitrary",)),
        name="mixer",
    )(pa, pb, pb, kvc, vgain_row, w_spatial, bs_bcast, qgb_row, kgb_row, sinks, qgc_row,
      s64, e64, s128, e128, s256, e256)


def _merge_body(y_ref, g0_ref, g1_ref, g2_ref, w_ref, o_ref, wb_ref):
    m = pl.program_id(1)

    @pl.when(m == 0)
    def _():
        wb_ref[...] = w_ref[...].astype(BF16)

    acc = None
    for b, g_ref in enumerate((g0_ref, g1_ref, g2_ref)):
        wide = jnp.dot(y_ref[:, b * BRANCH_WIDTH:(b + 1) * BRANCH_WIDTH], wb_ref[b],
                       preferred_element_type=F32)
        term = g_ref[...].astype(F32) * wide
        acc = term if acc is None else acc + term
    o_ref[...] = acc.astype(o_ref.dtype)


def _merge_call(y, gates, w_branch, layer):
    t = y.shape[0]
    per = D_MODEL // TN_MERGE

    def gate_spec(b):
        return pl.BlockSpec((TM_MERGE, TN_MERGE), lambda n, m: (m, b * per + n))

    return pl.pallas_call(
        _merge_body,
        grid=(D_MODEL // TN_MERGE, t // TM_MERGE),
        in_specs=[pl.BlockSpec((TM_MERGE, N_BRANCH * BRANCH_WIDTH), lambda n, m: (m, 0)),
                  gate_spec(0), gate_spec(1), gate_spec(2),
                  pl.BlockSpec((None, N_BRANCH, BRANCH_WIDTH, TN_MERGE),
                               lambda n, m: (layer, 0, 0, n))],
        out_specs=pl.BlockSpec((TM_MERGE, TN_MERGE), lambda n, m: (m, n)),
        out_shape=jax.ShapeDtypeStruct((t, D_MODEL), BF16),
        scratch_shapes=[pltpu.VMEM((N_BRANCH, BRANCH_WIDTH, TN_MERGE), BF16)],
        compiler_params=_params(("arbitrary", "arbitrary")),
        name="merge",
    )(y, gates, gates, gates, w_branch)


def _outproj_body(mg_ref, x_ref, w_ref, gn_ref, wr_ref, br_ref, x1_ref, lg_ref,
                  wb_ref, wr2_ref):
    i = pl.program_id(0)

    @pl.when(i == 0)
    def _():
        wb_ref[...] = w_ref[...].astype(BF16)
        wr = wr_ref[...]
        hi = wr.astype(BF16)
        wr2_ref[:, 0:LANES] = hi
        wr2_ref[:, LANES:2 * LANES] = (wr - hi.astype(F32)).astype(BF16)

    x1 = x_ref[...] + jnp.dot(mg_ref[...], wb_ref[...], preferred_element_type=F32)
    x1_ref[...] = x1
    h2 = _rms(x1, gn_ref[...])
    hi = h2.astype(BF16)
    lo = (h2 - hi.astype(F32)).astype(BF16)
    hi_w = jnp.dot(hi, wr2_ref[...], preferred_element_type=F32)
    lo_w = jnp.dot(lo, wr2_ref[...], preferred_element_type=F32)
    lg_ref[...] = hi_w[:, 0:LANES] + lo_w[:, 0:LANES] + hi_w[:, LANES:2 * LANES] + br_ref[...]


def _outproj_call(merged, x, w_out, gn_row, w_router, b_router, layer):
    t = x.shape[0]
    return pl.pallas_call(
        _outproj_body,
        grid=(t // TM_OUT,),
        in_specs=[pl.BlockSpec((TM_OUT, D_MODEL), lambda i: (i, 0)),
                  pl.BlockSpec((TM_OUT, D_MODEL), lambda i: (i, 0)),
                  pl.BlockSpec((None, D_MODEL, D_MODEL), lambda i: (layer, 0, 0),
                               pipeline_mode=pl.Buffered(1)),
                  pl.BlockSpec((1, D_MODEL), lambda i: (0, 0)),
                  pl.BlockSpec((D_MODEL, LANES), lambda i: (0, 0)),
                  pl.BlockSpec((1, LANES), lambda i: (0, 0))],
        out_specs=[pl.BlockSpec((TM_OUT, D_MODEL), lambda i: (i, 0)),
                   pl.BlockSpec((TM_OUT, LANES), lambda i: (i, 0))],
        out_shape=[jax.ShapeDtypeStruct((t, D_MODEL), F32),
                   jax.ShapeDtypeStruct((t, LANES), F32)],
        scratch_shapes=[pltpu.VMEM((D_MODEL, D_MODEL), BF16),
                        pltpu.VMEM((D_MODEL, 2 * LANES), BF16)],
        compiler_params=_params(("arbitrary",)),
        name="outproj",
    )(merged, x, w_out, gn_row, w_router, b_router)


def _route_body(lg_ref, ri_ref, rw_ref, cnt_ref, carry_ref):
    i = pl.program_id(0)

    @pl.when(i == 0)
    def _():
        carry_ref[...] = jnp.zeros_like(carry_ref)

    lg = lg_ref[...]
    tm = lg.shape[0]
    lane = lax.broadcasted_iota(I32, lg.shape, 1)
    neg = -jnp.inf
    big = jnp.int32(LANES)

    is_g = lane < N_GROUPS
    gl = jnp.where(is_g, lg, neg)
    gmax = jnp.max(gl, axis=-1, keepdims=True)
    gidx = jnp.min(jnp.where(gl == gmax, lane, big), axis=-1, keepdims=True)
    p_g = 1.0 / jnp.sum(jnp.where(is_g, jnp.exp(lg - gmax), 0.0), axis=-1, keepdims=True)

    e_lane = lane - N_GROUPS
    in_group = (e_lane >= 0) & (e_lane < N_EXPERTS) & ((e_lane >> 2) == gidx)
    el = jnp.where(in_group, lg, neg)
    v0 = jnp.max(el, axis=-1, keepdims=True)
    i0 = jnp.min(jnp.where(el == v0, lane, big), axis=-1, keepdims=True)
    el1 = jnp.where(lane == i0, neg, el)
    v1 = jnp.max(el1, axis=-1, keepdims=True)
    i1 = jnp.min(jnp.where(el1 == v1, lane, big), axis=-1, keepdims=True)
    e0 = i0 - N_GROUPS
    e1 = i1 - N_GROUPS
    ex = jnp.exp(v1 - v0)
    w0 = p_g / (1.0 + ex)
    w1 = p_g * ex / (1.0 + ex)

    oh0 = lane == e0
    oh1 = lane == e1
    oh = jnp.where(oh0 | oh1, 1.0, 0.0)
    r_i = lax.broadcasted_iota(I32, (tm, tm), 0)
    c_i = lax.broadcasted_iota(I32, (tm, tm), 1)
    lower = jnp.where(c_i < r_i, 1.0, 0.0).astype(BF16)
    before = jnp.dot(lower, oh.astype(BF16), preferred_element_type=F32) + carry_ref[...]
    rank0 = jnp.sum(jnp.where(oh0, before, 0.0), axis=-1, keepdims=True).astype(I32)
    rank1 = jnp.sum(jnp.where(oh1, before, 0.0), axis=-1, keepdims=True).astype(I32)
    carry_ref[...] = carry_ref[...] + jnp.sum(oh, axis=0, keepdims=True)

    zero_i = jnp.zeros_like(lane)
    ri_ref[...] = jnp.where(lane == 0, e0, jnp.where(lane == 1, e1,
                            jnp.where(lane == 2, rank0, jnp.where(lane == 3, rank1, zero_i))))
    rw_ref[...] = jnp.where(lane == 0, w0, jnp.where(lane == 1, w1, 0.0))
    cnt_ref[...] = jnp.broadcast_to(carry_ref[...], cnt_ref.shape).astype(I32)


def _route_call(logits):
    t = logits.shape[0]
    return pl.pallas_call(
        _route_body,
        grid=(t // TM_ROUTE,),
        in_specs=[pl.BlockSpec((TM_ROUTE, LANES), lambda i: (i, 0))],
        out_specs=[pl.BlockSpec((TM_ROUTE, LANES), lambda i: (i, 0)),
                   pl.BlockSpec((TM_ROUTE, LANES), lambda i: (i, 0)),
                   pl.BlockSpec((8, LANES), lambda i: (0, 0))],
        out_shape=[jax.ShapeDtypeStruct((t, LANES), I32),
                   jax.ShapeDtypeStruct((t, LANES), F32),
                   jax.ShapeDtypeStruct((8, LANES), I32)],
        scratch_shapes=[pltpu.VMEM((1, LANES), F32)],
        compiler_params=_params(("arbitrary",)),
        name="route",
    )(logits)


def _plan(route_i, counts, n_tiles):
    c = counts[0, :N_EXPERTS]
    tiles = (c + TM_EXPERT - 1) // TM_EXPERT
    ctiles = jnp.cumsum(tiles)
    start = (ctiles - tiles) * TM_EXPERT
    eid = jnp.arange(N_EXPERTS, dtype=I32)

    def slot(e, r):
        return jnp.sum(jnp.where(e[:, None] == eid[None, :], start[None, :], 0), axis=1) + r

    pos0 = slot(route_i[:, 0], route_i[:, 2]).astype(I32)
    pos1 = slot(route_i[:, 1], route_i[:, 3]).astype(I32)
    n_used = ctiles[-1]
    tid = jnp.minimum(jnp.arange(n_tiles, dtype=I32), n_used - 1)
    tile_expert = jnp.sum(tid[:, None] >= ctiles[None, :], axis=1).astype(I32)
    used_rows = n_used * TM_EXPERT
    pad_lo = jnp.concatenate([start + c, used_rows[None]]).astype(I32)
    pad_hi = jnp.concatenate([start + tiles * TM_EXPERT,
                              jnp.full((1,), n_tiles * TM_EXPERT, I32)]).astype(I32)
    return pos0, pos1, tile_expert, n_used.reshape(1).astype(I32), pad_lo, pad_hi


def _invert_body(p0_ref, p1_ref, lo_ref, hi_ref, src_ref):
    i = pl.program_id(0)

    @pl.when(i == 0)
    def _():
        def fill(r, c):
            src_ref[r] = 0
            return c
        for j in range(N_EXPERTS + 1):
            lax.fori_loop(lo_ref[j], hi_ref[j], fill, 0)

    base = i * TK_INVERT

    def scatter(t, c):
        src_ref[p0_ref[t]] = base + t
        src_ref[p1_ref[t]] = base + t
        return c
    lax.fori_loop(0, TK_INVERT, scatter, 0, unroll=8)


def _invert_call(pos0, pos1, pad_lo, pad_hi, n_rows):
    t = pos0.shape[0]
    smem = pl.BlockSpec((TK_INVERT,), lambda i: (i,), memory_space=pltpu.SMEM)
    whole = pl.BlockSpec(memory_space=pltpu.SMEM)
    return pl.pallas_call(
        _invert_body,
        grid=(t // TK_INVERT,),
        in_specs=[smem, smem, whole, whole],
        out_specs=pl.BlockSpec(memory_space=pltpu.SMEM),
        out_shape=jax.ShapeDtypeStruct((n_rows,), I32),
        compiler_params=_params(("arbitrary",)),
        name="invert",
    )(pos0, pos1, pad_lo, pad_hi)


def _expert_body(te_ref, nu_ref, srcc_ref, srcn_ref, x_hbm, gn_ref, wg_ref, wu_ref, wd_ref,
                 y_ref, xbuf_ref, wgb_ref, wub_ref, wdb_ref, sems):
    i = pl.program_id(0)
    n_used = nu_ref[0]
    slot = lax.rem(i, 2)

    def row_copy(src_ref, r, s):
        return pltpu.make_async_copy(x_hbm.at[pl.ds(src_ref[r], 1)],
                                     xbuf_ref.at[s, pl.ds(r, 1)], sems.at[s])

    def issue(src_ref, s):
        for r in range(TM_EXPERT):
            row_copy(src_ref, r, s).start()

    @pl.when(i == 0)
    def _():
        issue(srcc_ref, 0)

    @pl.when(i < n_used)
    def _():
        prev = te_ref[jnp.maximum(i - 1, 0)]

        @pl.when(jnp.logical_or(i == 0, te_ref[i] != prev))
        def _():
            wgb_ref[...] = wg_ref[...].astype(BF16)
            wub_ref[...] = wu_ref[...].astype(BF16)
            wdb_ref[...] = wd_ref[...].astype(BF16)

        pltpu.make_async_copy(x_hbm.at[pl.ds(0, TM_EXPERT)], xbuf_ref.at[slot],
                              sems.at[slot]).wait()

    def compute(s, prefetch):
        h = _rms(xbuf_ref[s], gn_ref[...]).astype(BF16)
        if prefetch:
            issue(srcn_ref, 1 - s)
        g = jnp.dot(h, wgb_ref[...], preferred_element_type=F32)
        u = jnp.dot(h, wub_ref[...], preferred_element_type=F32)
        hid = (g * _sigmoid(g) * u).astype(BF16)
        y_ref[...] = jnp.dot(hid, wdb_ref[...], preferred_element_type=F32)

    for s in range(2):
        @pl.when(jnp.logical_and(i + 1 < n_used, slot == s))
        def _():
            compute(s, True)

        @pl.when(jnp.logical_and(i + 1 == n_used, slot == s))
        def _():
            compute(s, False)

    @pl.when(i >= n_used)
    def _():
        y_ref[...] = jnp.zeros_like(y_ref)


def _expert_call(x1, src, tile_expert, n_used, gn_row, w_gate, w_up, w_down, layer):
    n_rows = src.shape[0]
    n_tiles = n_rows // TM_EXPERT

    def w_map(i, te, nu):
        return (layer, te[i], 0, 0)

    grid_spec = pltpu.PrefetchScalarGridSpec(
        num_scalar_prefetch=2,
        grid=(n_tiles,),
        in_specs=[pl.BlockSpec((TM_EXPERT,), lambda i, te, nu: (i,), memory_space=pltpu.SMEM),
                  pl.BlockSpec((TM_EXPERT,), lambda i, te, nu: (jnp.minimum(i + 1, n_tiles - 1),),
                               memory_space=pltpu.SMEM),
                  pl.BlockSpec(memory_space=pl.ANY),
                  pl.BlockSpec((1, D_MODEL), lambda i, te, nu: (0, 0)),
                  pl.BlockSpec((None, None, D_MODEL, D_FF_EXPERT), w_map),
                  pl.BlockSpec((None, None, D_MODEL, D_FF_EXPERT), w_map),
                  pl.BlockSpec((None, None, D_FF_EXPERT, D_MODEL), w_map)],
        out_specs=pl.BlockSpec((TM_EXPERT, D_MODEL), lambda i, te, nu: (i, 0)),
        scratch_shapes=[pltpu.VMEM((2, TM_EXPERT, D_MODEL), F32),
                        pltpu.VMEM((D_MODEL, D_FF_EXPERT), BF16),
                        pltpu.VMEM((D_MODEL, D_FF_EXPERT), BF16),
                        pltpu.VMEM((D_FF_EXPERT, D_MODEL), BF16),
                        pltpu.SemaphoreType.DMA((2,))],
    )
    return pl.pallas_call(
        _expert_body,
        grid_spec=grid_spec,
        out_shape=jax.ShapeDtypeStruct((n_rows, D_MODEL), F32),
        compiler_params=_params(("arbitrary",)),
        name="experts",
    )(tile_expert, n_used, src, src, x1, gn_row, w_gate, w_up, w_down)


def _combine_body(p0c_ref, p1c_ref, p0n_ref, p1n_ref, x1_ref, rw_ref, gn_ref, ys_hbm,
                  x2_ref, h_ref, buf_ref, sems, *, n_steps):
    i = pl.program_id(0)
    slot = lax.rem(i, 2)

    def row_copy(p_ref, k, t, s):
        return pltpu.make_async_copy(ys_hbm.at[pl.ds(p_ref[t], 1)],
                                     buf_ref.at[s, k, pl.ds(t, 1)], sems.at[s])

    def issue(p0_ref, p1_ref, s):
        for t in range(TM_COMBINE):
            row_copy(p0_ref, 0, t, s).start(priority=0)
            row_copy(p1_ref, 1, t, s).start(priority=1)

    @pl.when(i == 0)
    def _():
        issue(p0c_ref, p1c_ref, 0)

    for k in range(TOP_K):
        pltpu.make_async_copy(ys_hbm.at[pl.ds(0, TM_COMBINE)], buf_ref.at[slot, k],
                              sems.at[slot]).wait()

    def compute(s, prefetch):
        rw = rw_ref[...]
        x2 = x1_ref[...] + rw[:, 0:1] * buf_ref[s, 0] + rw[:, 1:2] * buf_ref[s, 1]
        if prefetch:
            issue(p0n_ref, p1n_ref, 1 - s)
        x2_ref[...] = x2
        h_ref[...] = _rms(x2, gn_ref[...]).astype(h_ref.dtype)

    for s in range(2):
        @pl.when(jnp.logical_and(i + 1 < n_steps, slot == s))
        def _():
            compute(s, True)

        @pl.when(jnp.logical_and(i + 1 == n_steps, slot == s))
        def _():
            compute(s, False)


def _combine_call(x1, route_w, pos0, pos1, ys, gn_row):
    t = x1.shape[0]
    n_steps = t // TM_COMBINE

    def nxt(i):
        return (jnp.minimum(i + 1, n_steps - 1),)

    smem = functools.partial(pl.BlockSpec, (TM_COMBINE,), memory_space=pltpu.SMEM)
    return pl.pallas_call(
        functools.partial(_combine_body, n_steps=n_steps),
        grid=(n_steps,),
        in_specs=[smem(lambda i: (i,)), smem(lambda i: (i,)), smem(nxt), smem(nxt),
                  pl.BlockSpec((TM_COMBINE, D_MODEL), lambda i: (i, 0)),
                  pl.BlockSpec((TM_COMBINE, LANES), lambda i: (i, 0)),
                  pl.BlockSpec((1, D_MODEL), lambda i: (0, 0)),
                  pl.BlockSpec(memory_space=pl.ANY)],
        out_specs=[pl.BlockSpec((TM_COMBINE, D_MODEL), lambda i: (i, 0)),
                   pl.BlockSpec((TM_COMBINE, D_MODEL), lambda i: (i, 0))],
        out_shape=[jax.ShapeDtypeStruct((t, D_MODEL), F32),
                   jax.ShapeDtypeStruct((t, D_MODEL), BF16)],
        scratch_shapes=[pltpu.VMEM((2, TOP_K, TM_COMBINE, D_MODEL), F32),
                        pltpu.SemaphoreType.DMA((2,))],
        compiler_params=_params(("arbitrary",)),
        name="combine",
    )(pos0, pos1, pos0, pos1, x1, route_w, gn_row, ys)


def kernel(x, mem, norm_mix, norm_mem, norm_ffn, w_in, v_gain, w_spatial, b_spatial,
           q_gain_b, k_gain_b, sinks, q_gain_c, k_gain_c, w_mem_kv, w_branch, w_out,
           w_router_group, b_router_group, w_router_expert, b_router_expert,
           w_gate_e, w_up_e, w_down_e):
    b, s, d = x.shape
    t = b * s
    assert d == D_MODEL and s % BLOCK == 0 and mem.shape[1] == MEM_LEN
    assert t % TK_INVERT == 0 and t % TM_PROJ == 0
    n_tiles = (t * TOP_K) // TM_EXPERT + N_EXPERTS
    n_rows = n_tiles * TM_EXPERT

    xf = x.reshape(t, d)
    mem2d = mem.reshape(b * MEM_LEN, d)
    pad = LANES - N_GROUPS - N_EXPERTS

    h = _norm_call(xf, norm_mix[0][None])
    for layer in range(DEPTH):
        kvc = _memkv_call(mem2d, norm_mem[layer][None], w_mem_kv, k_gain_c[layer][None], layer)
        pa = _inproj_call(h, w_in, layer, 0, SEC_A_WIDTH, TN_PROJ_A, _gelu, "inproj_a")
        pb = _inproj_call(h, w_in, layer, COL_QB, SEC_B_WIDTH, TN_PROJ_B, _identity, "inproj_b")
        gates = _inproj_call(h, w_in, layer, COL_GATE, SEC_G_WIDTH, TN_PROJ_G, _sigmoid, "inproj_g")
        bs_bcast = jnp.broadcast_to(b_spatial[layer][:, :, None], (A_GROUPS, CHUNK, LANES))
        y = _mixer_call(pa, pb, kvc, v_gain[layer].reshape(1, A_WIDTH), w_spatial, bs_bcast,
                        jnp.tile(q_gain_b[layer], B_HEADS)[None],
                        jnp.tile(k_gain_b[layer], B_KV_HEADS)[None],
                        sinks[layer], jnp.tile(q_gain_c[layer], C_HEADS)[None], layer, s)
        merged = _merge_call(y, gates, w_branch, layer)
        w_router = jnp.pad(jnp.concatenate([w_router_group[layer], w_router_expert[layer]], axis=1),
                           ((0, 0), (0, pad)))
        b_router = jnp.pad(jnp.concatenate([b_router_group[layer], b_router_expert[layer]]),
                           (0, pad))[None]
        x1, logits = _outproj_call(merged, xf, w_out, norm_ffn[layer][None], w_router, b_router,
                                   layer)
        route_i, route_w, counts = _route_call(logits)
        pos0, pos1, tile_expert, n_used, pad_lo, pad_hi = _plan(route_i, counts, n_tiles)
        src = _invert_call(pos0, pos1, pad_lo, pad_hi, n_rows)
        ys = _expert_call(x1, src, tile_expert, n_used, norm_ffn[layer][None], w_gate_e, w_up_e,
                          w_down_e, layer)
        gn_next = norm_mix[layer + 1] if layer + 1 < DEPTH else norm_mix[layer]
        xf, h = _combine_call(x1, route_w, pos0, pos1, ys, gn_next[None])
    return xf.reshape(b, s, d)
```
